```python
import jax, jax.numpy as jnp
from jax import lax
import numpy as np

D_MODEL = 1024
BATCH = 8
SEQ = 4096
DEPTH = 2

CHUNK = 64
MEM_LEN = 256
D_CONV = 3 * D_MODEL // 8
CONV_WIDTH = 31
D_SGU = 3 * D_MODEL // 8
SGU_HEADS = 4
SGU_CHUNK = 128
D_POOL = D_MODEL // 4
POOL_WINDOWS = (2, 4, 8, 16)
POOL_GROUPS = len(POOL_WINDOWS)
D_MIX = D_CONV + D_SGU + D_POOL
D_IN = 2 * D_CONV + 2 * D_SGU + D_POOL
X_HEADS = 4
X_HEAD_DIM = D_MODEL // X_HEADS
D_FF = ((8 * D_MODEL // 3 + 127) // 128) * 128
N_EXPERTS = 8
TOP_K = 2
D_FF_EXPERT = D_FF
N_DENSE = (DEPTH + 1) // 2
N_MOE = DEPTH // 2
EPS = 1e-6

kernel_name = 'hybrid_conv_sgu_pool_moe_block'


def rms_norm(x, g):
    xf = x.astype(jnp.float32)
    y = xf * lax.rsqrt(jnp.mean(xf * xf, axis=-1, keepdims=True) + EPS)
    return (y * g.astype(jnp.float32)).astype(x.dtype)


def layer_norm(x, g, b):
    xf = x.astype(jnp.float32)
    mu = jnp.mean(xf, axis=-1, keepdims=True)
    var = jnp.mean(jnp.square(xf - mu), axis=-1, keepdims=True)
    y = (xf - mu) * lax.rsqrt(var + EPS) * g.astype(jnp.float32) + b.astype(jnp.float32)
    return y.astype(x.dtype)


def conformer_conv(a, conv_w, conv_b, ln_g, ln_b):
    h = a[..., :D_CONV] * jax.nn.sigmoid(a[..., D_CONV:])
    h = lax.conv_general_dilated(
        h, conv_w[:, None, :], window_strides=(1,),
        padding=[(CONV_WIDTH - 1, 0)],
        dimension_numbers=('NWC', 'WIO', 'NWC'),
        feature_group_count=D_CONV) + conv_b
    h = layer_norm(h, ln_g, ln_b)
    return jax.nn.silu(h)


def spatial_gating(z, ln_g, ln_b, sgu_w, sgu_b):
    z = jax.nn.gelu(z)
    u, v = z[..., :D_SGU], z[..., D_SGU:]
    v = layer_norm(v, ln_g, ln_b)
    B, S, _ = v.shape
    v = v.reshape(B, S // SGU_CHUNK, SGU_CHUNK, SGU_HEADS, D_SGU // SGU_HEADS)
    blk = jnp.arange(SGU_CHUNK) // CHUNK
    mask = blk[:, None] >= blk[None, :]
    w = jnp.where(mask[None], sgu_w, 0)
    s = jnp.einsum('hij,bcjhd->bcihd', w, v) + sgu_b.T[None, None, :, :, None]
    return u * s.reshape(B, S, D_SGU)


def multiscale_pool(c, pool_w, pool_b, pool_scale):
    B, S, _ = c.shape
    gd = D_POOL // POOL_GROUPS
    cf = c.astype(jnp.float32)
    pos = jnp.arange(S)
    outs = []
    for gi, w in enumerate(POOL_WINDOWS):
        cg = cf[..., gi * gd:(gi + 1) * gd]
        cs = jnp.cumsum(cg, axis=1)
        prev = jnp.pad(cs, ((0, 0), (w, 0), (0, 0)))[:, :S]
        cnt = jnp.minimum(pos + 1, w).astype(jnp.float32)[None, :, None]
        outs.append((cs - prev) / cnt - cg)
    p = jnp.stack(outs, axis=2).astype(c.dtype)
    y = jnp.einsum('bsgi,gio->bsgo', p, pool_w) + pool_b
    return y.reshape(B, S, D_POOL) * pool_scale


def memory_cross_attention(h, m, wq, wk, wv, wo):
    B, S, D = h.shape
    M = m.shape[1]
    q = (h @ wq).reshape(B, S, X_HEADS, X_HEAD_DIM)
    k = (m @ wk).reshape(B, M, X_HEADS, X_HEAD_DIM)
    v = (m @ wv).reshape(B, M, X_HEADS, X_HEAD_DIM)
    s = jnp.einsum('bshd,bmhd->bhsm', q, k).astype(jnp.float32) * (X_HEAD_DIM ** -0.5)
    p = jax.nn.softmax(s, axis=-1).astype(h.dtype)
    o = jnp.einsum('bhsm,bmhd->bshd', p, v).reshape(B, S, D)
    return o @ wo


def swiglu(h, wg, wu, wd):
    return (jax.nn.silu(h @ wg) * (h @ wu)) @ wd


def moe_swiglu(h, router_w, wg, wu, wd):
    B, S, D = h.shape
    t = h.reshape(B * S, D)
    logits = (t @ router_w).astype(jnp.float32)
    top_v, top_i = lax.top_k(logits, TOP_K)
    gates = jax.nn.softmax(top_v, axis=-1)
    combine = jnp.sum(jax.nn.one_hot(top_i, N_EXPERTS, dtype=jnp.float32) * gates[..., None], axis=1)
    combine = combine.astype(h.dtype)
    y = jnp.zeros_like(t)
    for e in range(N_EXPERTS):
        y = y + combine[:, e:e + 1] * swiglu(t, wg[e], wu[e], wd[e])
    return y.reshape(B, S, D)


def setup_inputs(seed: int = 0) -> dict:
    key = jax.random.key(seed)
    keys = jax.random.split(key, 32)
    cnt = [0]
    f32 = jnp.float32

    def nk():
        k = keys[cnt[0]]
        cnt[0] += 1
        return k

    def nrm(shape, scale):
        return jax.random.normal(nk(), shape, f32) * scale

    def gain(shape):
        return 1.0 + nrm(shape, 0.02)

    L, D = DEPTH, D_MODEL
    gd = D_POOL // POOL_GROUPS
    return {
        'x': nrm((BATCH, SEQ, D), 1.0),
        'mem': nrm((BATCH, MEM_LEN, D), 1.0),
        'mix_norm_g': gain((L, D)),
        'w_in': nrm((L, D, D_IN), D ** -0.5),
        'conv_w': nrm((L, CONV_WIDTH, D_CONV), CONV_WIDTH ** -0.5),
        'conv_b': nrm((L, D_CONV), 0.02),
        'conv_ln_g': gain((L, D_CONV)),
        'conv_ln_b': nrm((L, D_CONV), 0.02),
        'sgu_ln_g': gain((L, D_SGU)),
        'sgu_ln_b': nrm((L, D_SGU), 0.02),
        'sgu_w': nrm((L, SGU_HEADS, SGU_CHUNK, SGU_CHUNK), SGU_CHUNK ** -0.5),
        'sgu_b': 1.0 + nrm((L, SGU_HEADS, SGU_CHUNK), 0.1),
        'pool_w': nrm((L, POOL_GROUPS, gd, gd), gd ** -0.5),
        'pool_b': nrm((L, POOL_GROUPS, gd), 0.02),
        'pool_scale': 1.0 + nrm((L, D_POOL), 0.05),
        'w_out': nrm((L, D_MIX, D), D_MIX ** -0.5),
        'xattn_norm_g': gain((L, D)),
        'mem_norm_g': gain((L, D)),
        'xattn_wq': nrm((L, D, D), D ** -0.5),
        'xattn_wk': nrm((L, D, D), D ** -0.5),
        'xattn_wv': nrm((L, D, D), D ** -0.5),
        'xattn_wo': nrm((L, D, D), D ** -0.5),
        'ffn_norm_g': gain((L, D)),
        'ffn_wg': nrm((N_DENSE, D, D_FF), D ** -0.5),
        'ffn_wu': nrm((N_DENSE, D, D_FF), D ** -0.5),
        'ffn_wd': nrm((N_DENSE, D_FF, D), D_FF ** -0.5),
        'router_w': nrm((N_MOE, D, N_EXPERTS), D ** -0.5),
        'moe_wg': nrm((N_MOE, N_EXPERTS, D, D_FF_EXPERT), D ** -0.5),
        'moe_wu': nrm((N_MOE, N_EXPERTS, D, D_FF_EXPERT), D ** -0.5),
        'moe_wd': nrm((N_MOE, N_EXPERTS, D_FF_EXPERT, D), D_FF_EXPERT ** -0.5),
        'final_norm_g': gain((D,)),
    }


def reference(x, mem, mix_norm_g, w_in, conv_w, conv_b, conv_ln_g, conv_ln_b,
              sgu_ln_g, sgu_ln_b, sgu_w, sgu_b, pool_w, pool_b, pool_scale, w_out,
              xattn_norm_g, mem_norm_g, xattn_wq, xattn_wk, xattn_wv, xattn_wo,
              ffn_norm_g, ffn_wg, ffn_wu, ffn_wd, router_w, moe_wg, moe_wu, moe_wd,
              final_norm_g):
    a_end = 2 * D_CONV
    b_end = a_end + 2 * D_SGU
    for l in range(DEPTH):
        h = rms_norm(x, mix_norm_g[l])
        z = h @ w_in[l]
        y_a = conformer_conv(z[..., :a_end], conv_w[l], conv_b[l], conv_ln_g[l], conv_ln_b[l])
        y_b = spatial_gating(z[..., a_end:b_end], sgu_ln_g[l], sgu_ln_b[l], sgu_w[l], sgu_b[l])
        y_c = multiscale_pool(z[..., b_end:], pool_w[l], pool_b[l], pool_scale[l])
        mixed = jnp.concatenate([y_a, y_b, y_c], axis=-1)
        x = x + mixed @ w_out[l]
        h = rms_norm(x, xattn_norm_g[l])
        m = rms_norm(mem, mem_norm_g[l])
        x = x + memory_cross_attention(h, m, xattn_wq[l], xattn_wk[l], xattn_wv[l], xattn_wo[l])
        h = rms_norm(x, ffn_norm_g[l])
        if l % 2 == 0:
            j = l // 2
            x = x + swiglu(h, ffn_wg[j], ffn_wu[j], ffn_wd[j])
        else:
            j = l // 2
            x = x + moe_swiglu(h, router_w[j], moe_wg[j], moe_wu[j], moe_wd[j])
    return rms_norm(x, final_norm_g)
```

```python
import functools

import jax
import jax.numpy as jnp
from jax import lax
from jax.experimental import pallas as pl
from jax.experimental.pallas import tpu as pltpu

F32 = jnp.float32
BF16 = jnp.bfloat16

D_MODEL = 1024
DEPTH = 2
CHUNK = 64
D_CONV = 384
CONV_WIDTH = 31
D_SGU = 384
SGU_HEADS = 4
SGU_CHUNK = 128
D_POOL = 256
POOL_WINDOWS = (2, 4, 8, 16)
POOL_GROUP_DIM = D_POOL // len(POOL_WINDOWS)
D_MIX = D_CONV + D_SGU + D_POOL
D_IN = 2 * D_CONV + 2 * D_SGU + D_POOL
X_HEADS = 4
X_HEAD_DIM = D_MODEL // X_HEADS
N_EXPERTS = 8
EPS = 1e-6

LANES = 128
HIST = 32
MIX_TS = 512
CONV_RB = 64
ATT_TS = 512
FFN_TM = 512
FFN_FC = 1408
VMEM_LIMIT = 56 * 1024 * 1024


def _rms(x, g):
    return x * lax.rsqrt(jnp.mean(x * x, axis=-1, keepdims=True) + EPS) * g


def _layer_norm(x, g, b):
    mu = jnp.mean(x, axis=-1, keepdims=True)
    xc = x - mu
    var = jnp.mean(xc * xc, axis=-1, keepdims=True)
    return xc * lax.rsqrt(var + EPS) * g + b


def _dot(a, b):
    return jnp.dot(a, b, preferred_element_type=F32)


def _mixer_kernel(x_ref, g_ref, win_ref, convw_ref, convb_ref, clng_ref, clnb_ref,
                  slng_ref, slnb_ref, sguw_ref, sgub_ref, poolw_ref, poolb_ref, pscale_ref,
                  wout_ref, o_ref, cbuf, pb0, pb1, pb2, pb3):
    ts = MIX_TS
    s = pl.program_id(1)

    @pl.when(s == 0)
    def _():
        cbuf[0:HIST, :] = jnp.zeros((HIST, D_CONV), F32)
        pb0[0:HIST, :] = jnp.zeros((HIST, D_POOL), F32)

    x = x_ref[0]
    h = _rms(x, g_ref[...]).astype(BF16)
    z = _dot(h, win_ref[...])

    a_end = 2 * D_CONV
    cbuf[HIST:HIST + ts, :] = z[:, :D_CONV] * jax.nn.sigmoid(z[:, D_CONV:a_end])
    ya_blocks = []
    for rb in range(ts // CONV_RB):
        acc = jnp.zeros((CONV_RB, D_CONV), F32)
        for k in range(CONV_WIDTH):
            start = rb * CONV_RB + HIST - (CONV_WIDTH - 1) + k
            acc = acc + convw_ref[k:k + 1, :] * cbuf[start:start + CONV_RB, :]
        ya_blocks.append(acc)
    ya = jnp.concatenate(ya_blocks, axis=0) + convb_ref[...]
    ya = _layer_norm(ya, clng_ref[...], clnb_ref[...])
    ya = ya * jax.nn.sigmoid(ya)
    cbuf[0:HIST, :] = cbuf[ts:ts + HIST, :]

    b_end = a_end + 2 * D_SGU
    zb = jax.nn.gelu(z[:, a_end:b_end])
    u = zb[:, :D_SGU]
    v = _layer_norm(zb[:, D_SGU:], slng_ref[...], slnb_ref[...]).astype(BF16)
    blk_r = lax.broadcasted_iota(jnp.int32, (SGU_CHUNK, SGU_CHUNK), 0) // CHUNK
    blk_c = lax.broadcasted_iota(jnp.int32, (SGU_CHUNK, SGU_CHUNK), 1) // CHUNK
    head_of_lane = lax.broadcasted_iota(jnp.int32, (SGU_CHUNK, D_SGU), 1) // (D_SGU // SGU_HEADS)
    w_heads = [jnp.where(blk_r >= blk_c, sguw_ref[hd], 0.0).astype(BF16) for hd in range(SGU_HEADS)]
    s_chunks = []
    for c in range(ts // SGU_CHUNK):
        vc = v[c * SGU_CHUNK:(c + 1) * SGU_CHUNK, :]
        sc = jnp.zeros((SGU_CHUNK, D_SGU), F32)
        for hd in range(SGU_HEADS):
            sc = jnp.where(head_of_lane == hd, _dot(w_heads[hd], vc), sc)
        s_chunks.append(sc + sgub_ref[...])
    yb = u * jnp.concatenate(s_chunks, axis=0)

    cc = z[:, b_end:]
    pb0[HIST:HIST + ts, :] = cc
    pb1[0:ts + 24, :] = pb0[8:ts + 32, :] + pb0[7:ts + 31, :]
    pb2[0:ts + 16, :] = pb1[8:ts + 24, :] + pb1[6:ts + 22, :]
    pb3[0:ts + 8, :] = pb2[8:ts + 16, :] + pb2[4:ts + 12, :]
    s16 = pb3[8:ts + 8, :] + pb3[0:ts, :]
    s8 = pb3[8:ts + 8, :]
    s4 = pb2[16:ts + 16, :]
    s2 = pb1[24:ts + 24, :]
    grp = lax.broadcasted_iota(jnp.int32, (ts, D_POOL), 1) // POOL_GROUP_DIM
    wsum = jnp.where(grp == 0, s2, jnp.where(grp == 1, s4, jnp.where(grp == 2, s8, s16)))
    win = jnp.where(grp == 0, 2, jnp.where(grp == 1, 4, jnp.where(grp == 2, 8, 16)))
    pos = s * ts + lax.broadcasted_iota(jnp.int32, (ts, D_POOL), 0)
    cnt = jnp.minimum(pos + 1, win).astype(F32)
    p = (wsum / cnt - cc).astype(BF16)
    yc = (_dot(p, poolw_ref[...]) + poolb_ref[...]) * pscale_ref[...]
    pb0[0:HIST, :] = pb0[ts:ts + HIST, :]

    out = x + _dot(ya.astype(BF16), wout_ref[0:D_CONV, :])
    out = out + _dot(yb.astype(BF16), wout_ref[D_CONV:D_CONV + D_SGU, :])
    out = out + _dot(yc.astype(BF16), wout_ref[D_CONV + D_SGU:D_MIX, :])
    o_ref[0] = out


def _const_spec(shape):
    zeros = (0,) * len(shape)
    return pl.BlockSpec(shape, lambda *_: zeros)


def _mixer(x, g, w_in, conv_w, conv_b, cln_g, cln_b, sln_g, sln_b, sgu_w, sgu_bias, pool_wbd, pool_b,
           pool_scale, w_out):
    B, S, D = x.shape
    ts = MIX_TS
    row = lambda a: a.reshape(1, -1)
    args = (x, row(g), w_in, conv_w, row(conv_b), row(cln_g), row(cln_b), row(sln_g), row(sln_b),
            sgu_w, sgu_bias, pool_wbd, row(pool_b), row(pool_scale), w_out)
    in_specs = [pl.BlockSpec((1, ts, D), lambda b, s: (b, s, 0))]
    in_specs += [_const_spec(a.shape) for a in args[1:]]
    return pl.pallas_call(
        _mixer_kernel,
        grid=(B, S // ts),
        in_specs=in_specs,
        out_specs=pl.BlockSpec((1, ts, D), lambda b, s: (b, s, 0)),
        out_shape=jax.ShapeDtypeStruct((B, S, D), F32),
        scratch_shapes=[pltpu.VMEM((ts + HIST, D_CONV), F32)] + [pltpu.VMEM((ts + HIST, D_POOL), F32)] * 4,
        compiler_params=pltpu.CompilerParams(
            dimension_semantics=("arbitrary", "arbitrary"), vmem_limit_bytes=VMEM_LIMIT),
        name="mixer",
    )(*args)


def _kv_kernel(mem_ref, g_ref, wk_ref, wv_ref, k_ref, v_ref):
    m = _rms(mem_ref[0], g_ref[...]).astype(BF16)
    k_ref[0] = _dot(m, wk_ref[...]).astype(BF16)
    v_ref[0] = _dot(m, wv_ref[...]).astype(BF16)


def _kv(mem, g, wk, wv):
    B, M, D = mem.shape
    return pl.pallas_call(
        _kv_kernel,
        grid=(B,),
        in_specs=[pl.BlockSpec((1, M, D), lambda b: (b, 0, 0)), _const_spec((1, D)),
                  _const_spec((D, D)), _const_spec((D, D))],
        out_specs=[pl.BlockSpec((1, M, D), lambda b: (b, 0, 0))] * 2,
        out_shape=[jax.ShapeDtypeStruct((B, M, D), BF16)] * 2,
        compiler_params=pltpu.CompilerParams(
            dimension_semantics=("arbitrary",), vmem_limit_bytes=VMEM_LIMIT),
        name="mem_kv",
    )(mem, g.reshape(1, -1), wk, wv)


def _xattn_kernel(x_ref, g_ref, wq_ref, k_ref, v_ref, wo_ref, o_ref):
    x = x_ref[0]
    h = _rms(x, g_ref[...]).astype(BF16)
    q = _dot(h, wq_ref[...]).astype(BF16)
    heads = []
    for hd in range(X_HEADS):
        sl = slice(hd * X_HEAD_DIM, (hd + 1) * X_HEAD_DIM)
        sc = lax.dot_general(q[:, sl], k_ref[0, :, sl], (((1,), (1,)), ((), ())),
                             preferred_element_type=F32) * (X_HEAD_DIM ** -0.5)
        e = jnp.exp(sc - jnp.max(sc, axis=-1, keepdims=True))
        heads.append(_dot(e.astype(BF16), v_ref[0, :, sl]) / jnp.sum(e, axis=-1, keepdims=True))
    o = jnp.concatenate(heads, axis=-1).astype(BF16)
    o_ref[0] = x + _dot(o, wo_ref[...])


def _xattn(x, g, wq, k, v, wo):
    B, S, D = x.shape
    M = k.shape[1]
    ts = ATT_TS
    return pl.pallas_call(
        _xattn_kernel,
        grid=(B, S // ts),
        in_specs=[pl.BlockSpec((1, ts, D), lambda b, s: (b, s, 0)), _const_spec((1, D)), _const_spec((D, D)),
                  pl.BlockSpec((1, M, D), lambda b, s: (b, 0, 0)), pl.BlockSpec((1, M, D), lambda b, s: (b, 0, 0)),
                  _const_spec((D, D))],
        out_specs=pl.BlockSpec((1, ts, D), lambda b, s: (b, s, 0)),
        out_shape=jax.ShapeDtypeStruct((B, S, D), F32),
        compiler_params=pltpu.CompilerParams(
            dimension_semantics=("arbitrary", "arbitrary"), vmem_limit_bytes=VMEM_LIMIT),
        name="xattn",
    )(x, g.reshape(1, -1), wq, k, v, wo)


def _router_kernel(x_ref, g_ref, rw_ref, comb_ref):
    h = _rms(x_ref[...], g_ref[...])
    logits = jnp.dot(h, rw_ref[...], preferred_element_type=F32, precision=lax.Precision.HIGHEST)
    lane = lax.broadcasted_iota(jnp.int32, logits.shape, 1)
    neg = jnp.float32(-jnp.inf)
    lg = jnp.where(lane < N_EXPERTS, logits, neg)
    m1 = jnp.max(lg, axis=-1, keepdims=True)
    i1 = jnp.min(jnp.where(lg == m1, lane, LANES), axis=-1, keepdims=True)
    lg2 = jnp.where(lane == i1, neg, lg)
    m2 = jnp.max(lg2, axis=-1, keepdims=True)
    i2 = jnp.min(jnp.where(lg2 == m2, lane, LANES), axis=-1, keepdims=True)
    d = jnp.exp(m2 - m1)
    g1 = 1.0 / (1.0 + d)
    g2 = d / (1.0 + d)
    comb_ref[...] = jnp.where(lane == i1, g1, jnp.where(lane == i2, g2, 0.0))


def _router(x2, g, router_w):
    T, D = x2.shape
    tm = FFN_TM
    rw = jnp.zeros((D, LANES), F32).at[:, :N_EXPERTS].set(router_w)
    return pl.pallas_call(
        _router_kernel,
        grid=(T // tm,),
        in_specs=[pl.BlockSpec((tm, D), lambda i: (i, 0)), _const_spec((1, D)), _const_spec((D, LANES))],
        out_specs=pl.BlockSpec((tm, LANES), lambda i: (i, 0)),
        out_shape=jax.ShapeDtypeStruct((T, LANES), F32),
        compiler_params=pltpu.CompilerParams(
            dimension_semantics=("arbitrary",), vmem_limit_bytes=VMEM_LIMIT),
        name="router",
    )(x2, g.reshape(1, -1), rw)


def _ffn_kernel(x_ref, g_ref, comb_ref, wg_ref, wu_ref, wd_ref, fg_ref, o_ref, h_scr, acc, *, gated, final_norm):
    e = pl.program_id(1)
    f = pl.program_id(2)
    first = jnp.logical_and(e == 0, f == 0)
    last = jnp.logical_and(e == pl.num_programs(1) - 1, f == pl.num_programs(2) - 1)

    @pl.when(first)
    def _():
        x = x_ref[...]
        h_scr[...] = _rms(x, g_ref[...]).astype(BF16)
        acc[...] = x

    h = h_scr[...]
    gate = _dot(h, wg_ref[0])
    up = _dot(h, wu_ref[0])
    a = (gate * jax.nn.sigmoid(gate) * up).astype(BF16)
    y = _dot(a, wd_ref[0])
    if gated:
        lane = lax.broadcasted_iota(jnp.int32, comb_ref.shape, 1)
        y = y * jnp.sum(jnp.where(lane == e, comb_ref[...], 0.0), axis=-1, keepdims=True)
    acc[...] += y

    @pl.when(last)
    def _():
        out = acc[...]
        if final_norm:
            out = _rms(out, fg_ref[...])
        o_ref[...] = out


def _ffn(x2, g, comb, wg, wu, wd, final_g, *, gated, final_norm):
    T, D = x2.shape
    E, _, F = wg.shape
    tm, fc = FFN_TM, FFN_FC
    kern = functools.partial(_ffn_kernel, gated=gated, final_norm=final_norm)
    return pl.pallas_call(
        kern,
        grid=(T // tm, E, F // fc),
        in_specs=[pl.BlockSpec((tm, D), lambda i, e, f: (i, 0)), _const_spec((1, D)),
                  pl.BlockSpec((tm, LANES), lambda i, e, f: (i, 0)),
                  pl.BlockSpec((1, D, fc), lambda i, e, f: (e, 0, f)),
                  pl.BlockSpec((1, D, fc), lambda i, e, f: (e, 0, f)),
                  pl.BlockSpec((1, fc, D), lambda i, e, f: (e, f, 0)),
                  _const_spec((1, D))],
        out_specs=pl.BlockSpec((tm, D), lambda i, e, f: (i, 0)),
        out_shape=jax.ShapeDtypeStruct((T, D), F32),
        scratch_shapes=[pltpu.VMEM((tm, D), BF16), pltpu.VMEM((tm, D), F32)],
        compiler_params=pltpu.CompilerParams(
            dimension_semantics=("arbitrary", "arbitrary", "arbitrary"), vmem_limit_bytes=VMEM_LIMIT),
        name="swiglu_gated" if gated else "swiglu",
    )(x2, g.reshape(1, -1), comb, wg, wu, wd, final_g.reshape(1, -1))


def kernel(x, mem, mix_norm_g, w_in, conv_w, conv_b, conv_ln_g, conv_ln_b, sgu_ln_g, sgu_ln_b, sgu_w, sgu_b,
           pool_w, pool_b, pool_scale, w_out, xattn_norm_g, mem_norm_g, xattn_wq, xattn_wk, xattn_wv, xattn_wo,
           ffn_norm_g, ffn_wg, ffn_wu, ffn_wd, router_w, moe_wg, moe_wu, moe_wd, final_norm_g):
    B, S, D = x.shape
    bf = lambda a: a.astype(BF16)
    for l in range(DEPTH):
        sgu_bias = jnp.repeat(sgu_b[l].T, D_SGU // SGU_HEADS, axis=1)
        pool_wbd = jax.scipy.linalg.block_diag(*[pool_w[l, gi] for gi in range(len(POOL_WINDOWS))])
        x = _mixer(x, mix_norm_g[l], bf(w_in[l]), conv_w[l], conv_b[l], conv_ln_g[l], conv_ln_b[l],
                   sgu_ln_g[l], sgu_ln_b[l], sgu_w[l], sgu_bias, bf(pool_wbd), pool_b[l].reshape(-1),
                   pool_scale[l], bf(w_out[l]))
        k, v = _kv(mem, mem_norm_g[l], bf(xattn_wk[l]), bf(xattn_wv[l]))
        x = _xattn(x, xattn_norm_g[l], bf(xattn_wq[l]), k, v, bf(xattn_wo[l]))
        x2 = x.reshape(B * S, D)
        last = l == DEPTH - 1
        j = l // 2
        if l % 2 == 0:
            ones = jnp.ones((B * S, LANES), F32)
            x2 = _ffn(x2, ffn_norm_g[l], ones, bf(ffn_wg[j])[None], bf(ffn_wu[j])[None], bf(ffn_wd[j])[None],
                      final_norm_g, gated=False, final_norm=last)
        else:
            comb = _router(x2, ffn_norm_g[l], router_w[j])
            x2 = _ffn(x2, ffn_norm_g[l], comb, bf(moe_wg[j]), bf(moe_wu[j]), bf(moe_wd[j]),
                      final_norm_g, gated=True, final_norm=last)
        x = x2.reshape(B, S, D)
    return x
```

```python
import functools

import jax
import jax.numpy as jnp
from jax import lax
from jax.experimental import pallas as pl
from jax.experimental.pallas import tpu as pltpu
from jax.experimental.pallas import tpu_sc as plsc

F32 = jnp.float32
BF16 = jnp.bfloat16
U32 = jnp.uint32
I32 = jnp.int32

D_MODEL = 1024
DEPTH = 2
CHUNK = 64
D_CONV = 384
CONV_WIDTH = 31
D_SGU = 384
SGU_HEADS = 4
SGU_CHUNK = 128
D_POOL = 256
POOL_WINDOWS = (2, 4, 8, 16)
POOL_GROUP_DIM = D_POOL // len(POOL_WINDOWS)
D_MIX = D_CONV + D_SGU + D_POOL
D_IN = 2 * D_CONV + 2 * D_SGU + D_POOL
X_HEADS = 4
X_HEAD_DIM = D_MODEL // X_HEADS
N_EXPERTS = 8
TOP_K = 2
EPS = 1e-6

LANES = 128
HIST = 32
MIX_TS = 512
CONV_RB = 64
ATT_TS = 512
FFN_TM = 512
FFN_FC = 1408
MOE_TM = 512
SC_WIN = 128
SC_ROW_WORDS = 256
HI16 = 0xFFFF0000
VMEM_LIMIT = 56 * 1024 * 1024


def _rms(x, g):
    return x * lax.rsqrt(jnp.mean(x * x, axis=-1, keepdims=True) + EPS) * g


def _layer_norm(x, g, b):
    mu = jnp.mean(x, axis=-1, keepdims=True)
    xc = x - mu
    var = jnp.mean(xc * xc, axis=-1, keepdims=True)
    return xc * lax.rsqrt(var + EPS) * g + b


def _dot(a, b):
    return jnp.dot(a, b, preferred_element_type=F32)


def _mixer_kernel(x_ref, g_ref, win_ref, convw_ref, convb_ref, clng_ref, clnb_ref,
                  slng_ref, slnb_ref, sguw_ref, sgub_ref, poolw_ref, poolb_ref, pscale_ref,
                  wout_ref, o_ref, cbuf, pb0, pb1, pb2, pb3):
    ts = MIX_TS
    s = pl.program_id(1)

    @pl.when(s == 0)
    def _():
        cbuf[0:HIST, :] = jnp.zeros((HIST, D_CONV), F32)
        pb0[0:HIST, :] = jnp.zeros((HIST, D_POOL), F32)

    x = x_ref[0]
    h = _rms(x, g_ref[...]).astype(BF16)
    z = _dot(h, win_ref[...])

    a_end = 2 * D_CONV
    cbuf[HIST:HIST + ts, :] = z[:, :D_CONV] * jax.nn.sigmoid(z[:, D_CONV:a_end])
    ya_blocks = []
    for rb in range(ts // CONV_RB):
        acc = jnp.zeros((CONV_RB, D_CONV), F32)
        for k in range(CONV_WIDTH):
            start = rb * CONV_RB + HIST - (CONV_WIDTH - 1) + k
            acc = acc + convw_ref[k:k + 1, :] * cbuf[start:start + CONV_RB, :]
        ya_blocks.append(acc)
    ya = jnp.concatenate(ya_blocks, axis=0) + convb_ref[...]
    ya = _layer_norm(ya, clng_ref[...], clnb_ref[...])
    ya = ya * jax.nn.sigmoid(ya)
    cbuf[0:HIST, :] = cbuf[ts:ts + HIST, :]

    b_end = a_end + 2 * D_SGU
    zb = jax.nn.gelu(z[:, a_end:b_end])
    u = zb[:, :D_SGU]
    v = _layer_norm(zb[:, D_SGU:], slng_ref[...], slnb_ref[...]).astype(BF16)
    blk_r = lax.broadcasted_iota(jnp.int32, (SGU_CHUNK, SGU_CHUNK), 0) // CHUNK
    blk_c = lax.broadcasted_iota(jnp.int32, (SGU_CHUNK, SGU_CHUNK), 1) // CHUNK
    head_of_lane = lax.broadcasted_iota(jnp.int32, (SGU_CHUNK, D_SGU), 1) // (D_SGU // SGU_HEADS)
    w_heads = [jnp.where(blk_r >= blk_c, sguw_ref[hd], 0.0).astype(BF16) for hd in range(SGU_HEADS)]
    s_chunks = []
    for c in range(ts // SGU_CHUNK):
        vc = v[c * SGU_CHUNK:(c + 1) * SGU_CHUNK, :]
        sc = jnp.zeros((SGU_CHUNK, D_SGU), F32)
        for hd in range(SGU_HEADS):
            sc = jnp.where(head_of_lane == hd, _dot(w_heads[hd], vc), sc)
        s_chunks.append(sc + sgub_ref[...])
    yb = u * jnp.concatenate(s_chunks, axis=0)

    cc = z[:, b_end:]
    pb0[HIST:HIST + ts, :] = cc
    pb1[0:ts + 24, :] = pb0[8:ts + 32, :] + pb0[7:ts + 31, :]
    pb2[0:ts + 16, :] = pb1[8:ts + 24, :] + pb1[6:ts + 22, :]
    pb3[0:ts + 8, :] = pb2[8:ts + 16, :] + pb2[4:ts + 12, :]
    s16 = pb3[8:ts + 8, :] + pb3[0:ts, :]
    s8 = pb3[8:ts + 8, :]
    s4 = pb2[16:ts + 16, :]
    s2 = pb1[24:ts + 24, :]
    grp = lax.broadcasted_iota(jnp.int32, (ts, D_POOL), 1) // POOL_GROUP_DIM
    wsum = jnp.where(grp == 0, s2, jnp.where(grp == 1, s4, jnp.where(grp == 2, s8, s16)))
    win = jnp.where(grp == 0, 2, jnp.where(grp == 1, 4, jnp.where(grp == 2, 8, 16)))
    pos = s * ts + lax.broadcasted_iota(jnp.int32, (ts, D_POOL), 0)
    cnt = jnp.minimum(pos + 1, win).astype(F32)
    p = (wsum / cnt - cc).astype(BF16)
    yc = (_dot(p, poolw_ref[...]) + poolb_ref[...]) * pscale_ref[...]
    pb0[0:HIST, :] = pb0[ts:ts + HIST, :]

    out = x + _dot(ya.astype(BF16), wout_ref[0:D_CONV, :])
    out = out + _dot(yb.astype(BF16), wout_ref[D_CONV:D_CONV + D_SGU, :])
    out = out + _dot(yc.astype(BF16), wout_ref[D_CONV + D_SGU:D_MIX, :])
    o_ref[0] = out


def _const_spec(shape):
    zeros = (0,) * len(shape)
    return pl.BlockSpec(shape, lambda *_: zeros)


def _mixer(x, g, w_in, conv_w, conv_b, cln_g, cln_b, sln_g, sln_b, sgu_w, sgu_bias, pool_wbd, pool_b,
           pool_scale, w_out):
    B, S, D = x.shape
    ts = MIX_TS
    row = lambda a: a.reshape(1, -1)
    args = (x, row(g), w_in, conv_w, row(conv_b), row(cln_g), row(cln_b), row(sln_g), row(sln_b),
            sgu_w, sgu_bias, pool_wbd, row(pool_b), row(pool_scale), w_out)
    in_specs = [pl.BlockSpec((1, ts, D), lambda b, s: (b, s, 0))]
    in_specs += [_const_spec(a.shape) for a in args[1:]]
    return pl.pallas_call(
        _mixer_kernel,
        grid=(B, S // ts),
        in_specs=in_specs,
        out_specs=pl.BlockSpec((1, ts, D), lambda b, s: (b, s, 0)),
        out_shape=jax.ShapeDtypeStruct((B, S, D), F32),
        scratch_shapes=[pltpu.VMEM((ts + HIST, D_CONV), F32)] + [pltpu.VMEM((ts + HIST, D_POOL), F32)] * 4,
        compiler_params=pltpu.CompilerParams(
            dimension_semantics=("arbitrary", "arbitrary"), vmem_limit_bytes=VMEM_LIMIT),
        name="mixer",
    )(*args)


def _kv_kernel(mem_ref, g_ref, wk_ref, wv_ref, k_ref, v_ref):
    m = _rms(mem_ref[0], g_ref[...]).astype(BF16)
    k_ref[0] = _dot(m, wk_ref[...]).astype(BF16)
    v_ref[0] = _dot(m, wv_ref[...]).astype(BF16)


def _kv(mem, g, wk, wv):
    B, M, D = mem.shape
    return pl.pallas_call(
        _kv_kernel,
        grid=(B,),
        in_specs=[pl.BlockSpec((1, M, D), lambda b: (b, 0, 0)), _const_spec((1, D)),
                  _const_spec((D, D)), _const_spec((D, D))],
        out_specs=[pl.BlockSpec((1, M, D), lambda b: (b, 0, 0))] * 2,
        out_shape=[jax.ShapeDtypeStruct((B, M, D), BF16)] * 2,
        compiler_params=pltpu.CompilerParams(
            dimension_semantics=("arbitrary",), vmem_limit_bytes=VMEM_LIMIT),
        name="mem_kv",
    )(mem, g.reshape(1, -1), wk, wv)


def _xattn_kernel(x_ref, g_ref, wq_ref, k_ref, v_ref, wo_ref, o_ref):
    x = x_ref[0]
    h = _rms(x, g_ref[...]).astype(BF16)
    q = _dot(h, wq_ref[...]).astype(BF16)
    heads = []
    for hd in range(X_HEADS):
        sl = slice(hd * X_HEAD_DIM, (hd + 1) * X_HEAD_DIM)
        sc = lax.dot_general(q[:, sl], k_ref[0, :, sl], (((1,), (1,)), ((), ())),
                             preferred_element_type=F32) * (X_HEAD_DIM ** -0.5)
        e = jnp.exp(sc - jnp.max(sc, axis=-1, keepdims=True))
        heads.append(_dot(e.astype(BF16), v_ref[0, :, sl]) / jnp.sum(e, axis=-1, keepdims=True))
    o = jnp.concatenate(heads, axis=-1).astype(BF16)
    o_ref[0] = x + _dot(o, wo_ref[...])


def _xattn(x, g, wq, k, v, wo):
    B, S, D = x.shape
    M = k.shape[1]
    ts = ATT_TS
    return pl.pallas_call(
        _xattn_kernel,
        grid=(B, S // ts),
        in_specs=[pl.BlockSpec((1, ts, D), lambda b, s: (b, s, 0)), _const_spec((1, D)), _const_spec((D, D)),
                  pl.BlockSpec((1, M, D), lambda b, s: (b, 0, 0)), pl.BlockSpec((1, M, D), lambda b, s: (b, 0, 0)),
                  _const_spec((D, D))],
        out_specs=pl.BlockSpec((1, ts, D), lambda b, s: (b, s, 0)),
        out_shape=jax.ShapeDtypeStruct((B, S, D), F32),
        compiler_params=pltpu.CompilerParams(
            dimension_semantics=("arbitrary", "arbitrary"), vmem_limit_bytes=VMEM_LIMIT),
        name="xattn",
    )(x, g.reshape(1, -1), wq, k, v, wo)


META_I1, META_I2, META_G1, META_G2, META_R1, META_R2 = range(6)


def _meta_col(meta, col):
    lane = lax.broadcasted_iota(I32, meta.shape, 1)
    return jnp.sum(jnp.where(lane == col, meta, 0.0), axis=-1, keepdims=True)


def _router_kernel(x_ref, g_ref, rw_ref, meta_ref, hp_ref, cnt_ref, carry):
    tm = x_ref.shape[0]

    @pl.when(pl.program_id(0) == 0)
    def _():
        carry[...] = jnp.zeros_like(carry)

    h = _rms(x_ref[...], g_ref[...])
    logits = jnp.dot(h, rw_ref[...], preferred_element_type=F32, precision=lax.Precision.HIGHEST)
    lane = lax.broadcasted_iota(I32, logits.shape, 1)
    neg = jnp.float32(-jnp.inf)
    lg = jnp.where(lane < N_EXPERTS, logits, neg)
    m1 = jnp.max(lg, axis=-1, keepdims=True)
    i1 = jnp.min(jnp.where(lg == m1, lane, LANES), axis=-1, keepdims=True)
    lg2 = jnp.where(lane == i1, neg, lg)
    m2 = jnp.max(lg2, axis=-1, keepdims=True)
    i2 = jnp.min(jnp.where(lg2 == m2, lane, LANES), axis=-1, keepdims=True)
    d = jnp.exp(m2 - m1)
    g1 = 1.0 / (1.0 + d)
    g2 = d / (1.0 + d)

    sel1 = lane == i1
    sel2 = lane == i2
    onehot = jnp.where(jnp.logical_or(sel1, sel2), 1.0, 0.0)
    row = lax.broadcasted_iota(I32, (tm, tm), 0)
    col = lax.broadcasted_iota(I32, (tm, tm), 1)
    before = _dot(jnp.where(col < row, 1.0, 0.0).astype(BF16), onehot.astype(BF16)) + carry[...]
    r1 = jnp.sum(jnp.where(sel1, before, 0.0), axis=-1, keepdims=True)
    r2 = jnp.sum(jnp.where(sel2, before, 0.0), axis=-1, keepdims=True)
    carry[...] += jnp.sum(onehot, axis=0, keepdims=True)
    cnt_ref[...] = carry[...]

    meta = jnp.zeros_like(logits)
    for c, val in ((META_I1, i1.astype(F32)), (META_I2, i2.astype(F32)), (META_G1, g1), (META_G2, g2),
                   (META_R1, r1), (META_R2, r2)):
        meta = jnp.where(lane == c, val, meta)
    meta_ref[...] = meta

    bits = lax.bitcast_convert_type(h.astype(BF16).astype(F32), U32)
    half = bits.shape[1] // 2
    hp_ref[...] = (bits[:, :half] & jnp.uint32(HI16)) | (bits[:, half:] >> 16)


def _router(x2, g, router_w):
    T, D = x2.shape
    tm = FFN_TM
    rw = jnp.zeros((D, LANES), F32).at[:, :N_EXPERTS].set(router_w)
    return pl.pallas_call(
        _router_kernel,
        grid=(T // tm,),
        in_specs=[pl.BlockSpec((tm, D), lambda i: (i, 0)), _const_spec((1, D)), _const_spec((D, LANES))],
        out_specs=[pl.BlockSpec((tm, LANES), lambda i: (i, 0)), pl.BlockSpec((tm, D // 2), lambda i: (i, 0)),
                   _const_spec((1, LANES))],
        out_shape=[jax.ShapeDtypeStruct((T, LANES), F32), jax.ShapeDtypeStruct((T, D // 2), U32),
                   jax.ShapeDtypeStruct((1, LANES), F32)],
        scratch_shapes=[pltpu.VMEM((1, LANES), F32)],
        compiler_params=pltpu.CompilerParams(
            dimension_semantics=("arbitrary",), vmem_limit_bytes=VMEM_LIMIT),
        name="router",
    )(x2, g.reshape(1, -1), rw)


def _sc_mesh():
    return plsc.VectorSubcoreMesh(core_axis_name="core", subcore_axis_name="subcore")


def _split_rows(rows, idx_list):
    n, w = rows.shape
    k = w // SC_ROW_WORDS
    piece = jnp.arange(k, dtype=I32)
    return rows.reshape(n * k, SC_ROW_WORDS), [(i[:, None] * k + piece).reshape(-1) for i in idx_list], k


def _sc_dispatch(rows, dest1, dest2, n_out):
    width = rows.shape[1]
    rows, (dest1, dest2), k = _split_rows(rows, [dest1, dest2])
    return _sc_scatter_pieces(rows, dest1, dest2, n_out * k).reshape(n_out, width)


def _sc_gather(rows, idx):
    width = rows.shape[1]
    rows, (idx,), _ = _split_rows(rows, [idx])
    return _sc_gather_pieces(rows, idx).reshape(-1, width)


def _sc_scatter_pieces(rows, dest1, dest2, n_out):
    T, W = rows.shape
    win = SC_WIN

    @pl.kernel(out_type=jax.ShapeDtypeStruct((n_out, W), rows.dtype), mesh=_sc_mesh(), scratch_types=[])
    def scatter_kernel(x_hbm, i1_hbm, i2_hbm, o_hbm):
        def body(x_vmem, i1_vmem, i2_vmem):
            pltpu.sync_copy(x_vmem, o_hbm.at[i1_vmem.at[0]])
            pltpu.sync_copy(x_vmem, o_hbm.at[i2_vmem.at[0]])

        pltpu.emit_pipeline(
            body,
            grid=(T // win,),
            in_specs=[pl.BlockSpec((win, W), lambda i: (i, 0)),
                      pl.BlockSpec((1, win), lambda i: (0, i)),
                      pl.BlockSpec((1, win), lambda i: (0, i))],
            out_specs=[],
            core_axis_name=("core", "subcore"),
            dimension_semantics=(pltpu.PARALLEL,),
        )(x_hbm, i1_hbm, i2_hbm)

    return scatter_kernel(rows, dest1.reshape(1, T), dest2.reshape(1, T))


def _sc_gather_pieces(rows, idx):
    n = idx.shape[0]
    W = rows.shape[1]
    win = SC_WIN

    @pl.kernel(out_type=jax.ShapeDtypeStruct((n, W), rows.dtype), mesh=_sc_mesh(), scratch_types=[])
    def gather_kernel(x_hbm, i_hbm, o_hbm):
        def body(i_vmem, o_vmem):
            pltpu.sync_copy(x_hbm.at[i_vmem.at[0]], o_vmem)

        pltpu.emit_pipeline(
            body,
            grid=(n // win,),
            in_specs=[pl.BlockSpec((1, win), lambda i: (0, i))],
            out_specs=[pl.BlockSpec((win, W), lambda i: (i, 0))],
            core_axis_name=("core", "subcore"),
            dimension_semantics=(pltpu.PARALLEL,),
        )(i_hbm, o_hbm)

    return gather_kernel(rows, idx.reshape(1, n))


def _swiglu(h, wg, wu, wd):
    gate = _dot(h, wg)
    up = _dot(h, wu)
    return _dot((gate * jax.nn.sigmoid(gate) * up).astype(BF16), wd)


def _ffn_kernel(x_ref, g_ref, wg_ref, wu_ref, wd_ref, o_ref, h_scr, acc):
    f = pl.program_id(1)

    @pl.when(f == 0)
    def _():
        x = x_ref[...]
        h_scr[...] = _rms(x, g_ref[...]).astype(BF16)
        acc[...] = x

    acc[...] += _swiglu(h_scr[...], wg_ref[...], wu_ref[...], wd_ref[...])

    @pl.when(f == pl.num_programs(1) - 1)
    def _():
        o_ref[...] = acc[...]


def _ffn(x2, g, wg, wu, wd):
    T, D = x2.shape
    F = wg.shape[1]
    tm, fc = FFN_TM, FFN_FC
    return pl.pallas_call(
        _ffn_kernel,
        grid=(T // tm, F // fc),
        in_specs=[pl.BlockSpec((tm, D), lambda i, f: (i, 0)), _const_spec((1, D)),
                  pl.BlockSpec((D, fc), lambda i, f: (0, f)),
                  pl.BlockSpec((D, fc), lambda i, f: (0, f)),
                  pl.BlockSpec((fc, D), lambda i, f: (f, 0))],
        out_specs=pl.BlockSpec((tm, D), lambda i, f: (i, 0)),
        out_shape=jax.ShapeDtypeStruct((T, D), F32),
        scratch_shapes=[pltpu.VMEM((tm, D), BF16), pltpu.VMEM((tm, D), F32)],
        compiler_params=pltpu.CompilerParams(
            dimension_semantics=("arbitrary", "arbitrary"), vmem_limit_bytes=VMEM_LIMIT),
        name="swiglu",
    )(x2, g.reshape(1, -1), wg, wu, wd)


def _moe_kernel(tidx_ref, texp_ref, nvalid_ref, xs_ref, wg_ref, wu_ref, wd_ref, y_ref):
    del tidx_ref, texp_ref
    F = wg_ref.shape[2]

    @pl.when(pl.program_id(0) < nvalid_ref[0])
    def _():
        w = xs_ref[...]
        hi = lax.bitcast_convert_type(w & jnp.uint32(HI16), F32)
        lo = lax.bitcast_convert_type(w << 16, F32)
        h = jnp.concatenate([hi, lo], axis=1).astype(BF16)
        acc = None
        for c in range(F // FFN_FC):
            sl = slice(c * FFN_FC, (c + 1) * FFN_FC)
            y = _swiglu(h, wg_ref[0, :, sl], wu_ref[0, :, sl], wd_ref[0, sl, :])
            acc = y if acc is None else acc + y
        y_ref[...] = acc


def _moe_experts(xs, tile_idx, tile_expert, n_valid, wg, wu, wd):
    P, half = xs.shape
    E, D, F = wg.shape
    tm = MOE_TM
    row_map = lambda g, ti, te, nv: (ti[g], 0)
    w_map = lambda g, ti, te, nv: (te[g], 0, 0)
    return pl.pallas_call(
        _moe_kernel,
        grid_spec=pltpu.PrefetchScalarGridSpec(
            num_scalar_prefetch=3,
            grid=(P // tm,),
            in_specs=[pl.BlockSpec((tm, half), row_map),
                      pl.BlockSpec((1, D, F), w_map), pl.BlockSpec((1, D, F), w_map),
                      pl.BlockSpec((1, F, D), w_map)],
            out_specs=pl.BlockSpec((tm, D), row_map)),
        out_shape=jax.ShapeDtypeStruct((P, D), F32),
        compiler_params=pltpu.CompilerParams(
            dimension_semantics=("arbitrary",), vmem_limit_bytes=VMEM_LIMIT),
        name="moe_experts",
    )(tile_idx, tile_expert, n_valid, xs, wg, wu, wd)


def _combine_kernel(x_ref, y1_ref, y2_ref, meta_ref, fg_ref, o_ref, *, final_norm):
    meta = meta_ref[...]
    out = x_ref[...] + _meta_col(meta, META_G1) * y1_ref[0] + _meta_col(meta, META_G2) * y2_ref[0]
    if final_norm:
        out = _rms(out, fg_ref[...])
    o_ref[...] = out


def _combine(x2, y12, meta, final_g, *, final_norm):
    T, D = x2.shape
    tm = FFN_TM
    return pl.pallas_call(
        functools.partial(_combine_kernel, final_norm=final_norm),
        grid=(T // tm,),
        in_specs=[pl.BlockSpec((tm, D), lambda i: (i, 0)),
                  pl.BlockSpec((1, tm, D), lambda i: (0, i, 0)), pl.BlockSpec((1, tm, D), lambda i: (1, i, 0)),
                  pl.BlockSpec((tm, LANES), lambda i: (i, 0)), _const_spec((1, D))],
        out_specs=pl.BlockSpec((tm, D), lambda i: (i, 0)),
        out_shape=jax.ShapeDtypeStruct((T, D), F32),
        compiler_params=pltpu.CompilerParams(
            dimension_semantics=("arbitrary",), vmem_limit_bytes=VMEM_LIMIT),
        name="moe_combine",
    )(x2, y12, y12, meta, final_g.reshape(1, -1))


def _moe_layer(x2, g, router_w, wg, wu, wd, final_g, *, final_norm):
    T, D = x2.shape
    tm = MOE_TM
    n_tiles = (T * TOP_K) // tm + N_EXPERTS
    meta, hp, counts = _router(x2, g, router_w)

    cnt = counts[0, :N_EXPERTS].astype(I32)
    tiles_e = (cnt + tm - 1) // tm
    tile_end = jnp.cumsum(tiles_e)
    row_start = (tile_end - tiles_e) * tm
    i1 = meta[:, META_I1].astype(I32)
    i2 = meta[:, META_I2].astype(I32)
    dest1 = row_start[i1] + meta[:, META_R1].astype(I32)
    dest2 = row_start[i2] + meta[:, META_R2].astype(I32)
    n_valid = tile_end[-1:]
    tile_idx = jnp.minimum(jnp.arange(n_tiles, dtype=I32), n_valid - 1)
    tile_expert = jnp.minimum(jnp.searchsorted(tile_end, tile_idx, side="right"), N_EXPERTS - 1).astype(I32)

    xs = _sc_dispatch(hp, dest1, dest2, n_tiles * tm)
    ys = _moe_experts(xs, tile_idx, tile_expert, n_valid.astype(I32), wg, wu, wd)
    y12 = _sc_gather(ys, jnp.concatenate([dest1, dest2])).reshape(TOP_K, T, D)
    return _combine(x2, y12, meta, final_g, final_norm=final_norm)


def kernel(x, mem, mix_norm_g, w_in, conv_w, conv_b, conv_ln_g, conv_ln_b, sgu_ln_g, sgu_ln_b, sgu_w, sgu_b,
           pool_w, pool_b, pool_scale, w_out, xattn_norm_g, mem_norm_g, xattn_wq, xattn_wk, xattn_wv, xattn_wo,
           ffn_norm_g, ffn_wg, ffn_wu, ffn_wd, router_w, moe_wg, moe_wu, moe_wd, final_norm_g):
    B, S, D = x.shape
    bf = lambda a: a.astype(BF16)
    for l in range(DEPTH):
        sgu_bias = jnp.repeat(sgu_b[l].T, D_SGU // SGU_HEADS, axis=1)
        pool_wbd = jax.scipy.linalg.block_diag(*[pool_w[l, gi] for gi in range(len(POOL_WINDOWS))])
        x = _mixer(x, mix_norm_g[l], bf(w_in[l]), conv_w[l], conv_b[l], conv_ln_g[l], conv_ln_b[l],
                   sgu_ln_g[l], sgu_ln_b[l], sgu_w[l], sgu_bias, bf(pool_wbd), pool_b[l].reshape(-1),
                   pool_scale[l], bf(w_out[l]))
        k, v = _kv(mem, mem_norm_g[l], bf(xattn_wk[l]), bf(xattn_wv[l]))
        x = _xattn(x, xattn_norm_g[l], bf(xattn_wq[l]), k, v, bf(xattn_wo[l]))
        x2 = x.reshape(B * S, D)
        j = l // 2
        if l % 2 == 0:
            assert l != DEPTH - 1, "the final RMSNorm is fused into the routed layer's combine kernel"
            x2 = _ffn(x2, ffn_norm_g[l], bf(ffn_wg[j]), bf(ffn_wu[j]), bf(ffn_wd[j]))
        else:
            x2 = _moe_layer(x2, ffn_norm_g[l], router_w[j], bf(moe_wg[j]), bf(moe_wu[j]), bf(moe_wd[j]),
                            final_norm_g, final_norm=l == DEPTH - 1)
        x = x2.reshape(B, S, D)
    return x
```

```python
import functools

import jax
import jax.numpy as jnp
from jax import lax
from jax.experimental import pallas as pl
from jax.experimental.pallas import tpu as pltpu
from jax.experimental.pallas import tpu_sc as plsc

F32 = jnp.float32
BF16 = jnp.bfloat16
U32 = jnp.uint32
I32 = jnp.int32

D_MODEL = 1024
DEPTH = 2
CHUNK = 64
D_CONV = 384
CONV_WIDTH = 31
D_SGU = 384
SGU_HEADS = 4
SGU_CHUNK = 128
D_POOL = 256
POOL_WINDOWS = (2, 4, 8, 16)
POOL_GROUP_DIM = D_POOL // len(POOL_WINDOWS)
D_MIX = D_CONV + D_SGU + D_POOL
D_IN = 2 * D_CONV + 2 * D_SGU + D_POOL
X_HEADS = 4
X_HEAD_DIM = D_MODEL // X_HEADS
N_EXPERTS = 8
TOP_K = 2
EPS = 1e-6

LANES = 128
SUBLANES = 8
HIST = 32
MIX_TS = 512
CONV_RB = 64
ATT_TS = 512
FFN_TM = 512
FFN_FC = 1408
MOE_TM = 512
SC_WIN = 128
HI16 = 0xFFFF0000
VMEM_LIMIT = 56 * 1024 * 1024


def _rms(x, g):
    return x * lax.rsqrt(jnp.mean(x * x, axis=-1, keepdims=True) + EPS) * g


def _layer_norm(x, g, b):
    mu = jnp.mean(x, axis=-1, keepdims=True)
    xc = x - mu
    var = jnp.mean(xc * xc, axis=-1, keepdims=True)
    return xc * lax.rsqrt(var + EPS) * g + b


def _dot(a, b):
    return jnp.dot(a, b, preferred_element_type=F32)


def _mixer_kernel(x_ref, g_ref, win_ref, convw_ref, convb_ref, clng_ref, clnb_ref,
                  slng_ref, slnb_ref, sguw_ref, sgub_ref, poolw_ref, poolb_ref, pscale_ref,
                  wout_ref, o_ref, cbuf, pb0, pb1, pb2, pb3):
    ts = MIX_TS
    s = pl.program_id(1)

    @pl.when(s == 0)
    def _():
        cbuf[0:HIST, :] = jnp.zeros((HIST, D_CONV), F32)
        pb0[0:HIST, :] = jnp.zeros((HIST, D_POOL), F32)

    x = x_ref[0]
    h = _rms(x, g_ref[...]).astype(BF16)
    z = _dot(h, win_ref[...])

    a_end = 2 * D_CONV
    cbuf[HIST:HIST + ts, :] = z[:, :D_CONV] * jax.nn.sigmoid(z[:, D_CONV:a_end])
    ya_blocks = []
    for rb in range(ts // CONV_RB):
        acc = jnp.zeros((CONV_RB, D_CONV), F32)
        for k in range(CONV_WIDTH):
            start = rb * CONV_RB + HIST - (CONV_WIDTH - 1) + k
            acc = acc + convw_ref[k:k + 1, :] * cbuf[start:start + CONV_RB, :]
        ya_blocks.append(acc)
    ya = jnp.concatenate(ya_blocks, axis=0) + convb_ref[...]
    ya = _layer_norm(ya, clng_ref[...], clnb_ref[...])
    ya = ya * jax.nn.sigmoid(ya)
    cbuf[0:HIST, :] = cbuf[ts:ts + HIST, :]

    b_end = a_end + 2 * D_SGU
    zb = jax.nn.gelu(z[:, a_end:b_end])
    u = zb[:, :D_SGU]
    v = _layer_norm(zb[:, D_SGU:], slng_ref[...], slnb_ref[...]).astype(BF16)
    blk_r = lax.broadcasted_iota(jnp.int32, (SGU_CHUNK, SGU_CHUNK), 0) // CHUNK
    blk_c = lax.broadcasted_iota(jnp.int32, (SGU_CHUNK, SGU_CHUNK), 1) // CHUNK
    head_of_lane = lax.broadcasted_iota(jnp.int32, (SGU_CHUNK, D_SGU), 1) // (D_SGU // SGU_HEADS)
    w_heads = [jnp.where(blk_r >= blk_c, sguw_ref[hd], 0.0).astype(BF16) for hd in range(SGU_HEADS)]
    s_chunks = []
    for c in range(ts // SGU_CHUNK):
        vc = v[c * SGU_CHUNK:(c + 1) * SGU_CHUNK, :]
        sc = jnp.zeros((SGU_CHUNK, D_SGU), F32)
        for hd in range(SGU_HEADS):
            sc = jnp.where(head_of_lane == hd, _dot(w_heads[hd], vc), sc)
        s_chunks.append(sc + sgub_ref[...])
    yb = u * jnp.concatenate(s_chunks, axis=0)

    cc = z[:, b_end:]
    pb0[HIST:HIST + ts, :] = cc
    pb1[0:ts + 24, :] = pb0[8:ts + 32, :] + pb0[7:ts + 31, :]
    pb2[0:ts + 16, :] = pb1[8:ts + 24, :] + pb1[6:ts + 22, :]
    pb3[0:ts + 8, :] = pb2[8:ts + 16, :] + pb2[4:ts + 12, :]
    s16 = pb3[8:ts + 8, :] + pb3[0:ts, :]
    s8 = pb3[8:ts + 8, :]
    s4 = pb2[16:ts + 16, :]
    s2 = pb1[24:ts + 24, :]
    grp = lax.broadcasted_iota(jnp.int32, (ts, D_POOL), 1) // POOL_GROUP_DIM
    wsum = jnp.where(grp == 0, s2, jnp.where(grp == 1, s4, jnp.where(grp == 2, s8, s16)))
    win = jnp.where(grp == 0, 2, jnp.where(grp == 1, 4, jnp.where(grp == 2, 8, 16)))
    pos = s * ts + lax.broadcasted_iota(jnp.int32, (ts, D_POOL), 0)
    cnt = jnp.minimum(pos + 1, win).astype(F32)
    p = (wsum / cnt - cc).astype(BF16)
    yc = (_dot(p, poolw_ref[...]) + poolb_ref[...]) * pscale_ref[...]
    pb0[0:HIST, :] = pb0[ts:ts + HIST, :]

    out = x + _dot(ya.astype(BF16), wout_ref[0:D_CONV, :])
    out = out + _dot(yb.astype(BF16), wout_ref[D_CONV:D_CONV + D_SGU, :])
    out = out + _dot(yc.astype(BF16), wout_ref[D_CONV + D_SGU:D_MIX, :])
    o_ref[0] = out


def _const_spec(shape):
    zeros = (0,) * len(shape)
    return pl.BlockSpec(shape, lambda *_: zeros)


def _mixer(x, g, w_in, conv_w, conv_b, cln_g, cln_b, sln_g, sln_b, sgu_w, sgu_bias, pool_wbd, pool_b,
           pool_scale, w_out):
    B, S, D = x.shape
    ts = MIX_TS
    row = lambda a: a.reshape(1, -1)
    args = (x, row(g), w_in, conv_w, row(conv_b), row(cln_g), row(cln_b), row(sln_g), row(sln_b),
            sgu_w, sgu_bias, pool_wbd, row(pool_b), row(pool_scale), w_out)
    in_specs = [pl.BlockSpec((1, ts, D), lambda b, s: (b, s, 0))]
    in_specs += [_const_spec(a.shape) for a in args[1:]]
    return pl.pallas_call(
        _mixer_kernel,
        grid=(B, S // ts),
        in_specs=in_specs,
        out_specs=pl.BlockSpec((1, ts, D), lambda b, s: (b, s, 0)),
        out_shape=jax.ShapeDtypeStruct((B, S, D), F32),
        scratch_shapes=[pltpu.VMEM((ts + HIST, D_CONV), F32)] + [pltpu.VMEM((ts + HIST, D_POOL), F32)] * 4,
        compiler_params=pltpu.CompilerParams(
            dimension_semantics=("arbitrary", "arbitrary"), vmem_limit_bytes=VMEM_LIMIT),
        name="mixer",
    )(*args)


def _kv_kernel(mem_ref, g_ref, wk_ref, wv_ref, k_ref, v_ref):
    m = _rms(mem_ref[0], g_ref[...]).astype(BF16)
    k_ref[0] = _dot(m, wk_ref[...]).astype(BF16)
    v_ref[0] = _dot(m, wv_ref[...]).astype(BF16)


def _kv(mem, g, wk, wv):
    B, M, D = mem.shape
    return pl.pallas_call(
        _kv_kernel,
        grid=(B,),
        in_specs=[pl.BlockSpec((1, M, D), lambda b: (b, 0, 0)), _const_spec((1, D)),
                  _const_spec((D, D)), _const_spec((D, D))],
        out_specs=[pl.BlockSpec((1, M, D), lambda b: (b, 0, 0))] * 2,
        out_shape=[jax.ShapeDtypeStruct((B, M, D), BF16)] * 2,
        compiler_params=pltpu.CompilerParams(
            dimension_semantics=("arbitrary",), vmem_limit_bytes=VMEM_LIMIT),
        name="mem_kv",
    )(mem, g.reshape(1, -1), wk, wv)


def _xattn_kernel(x_ref, g_ref, wq_ref, k_ref, v_ref, wo_ref, o_ref):
    x = x_ref[0]
    h = _rms(x, g_ref[...]).astype(BF16)
    q = _dot(h, wq_ref[...]).astype(BF16)
    heads = []
    for hd in range(X_HEADS):
        sl = slice(hd * X_HEAD_DIM, (hd + 1) * X_HEAD_DIM)
        sc = lax.dot_general(q[:, sl], k_ref[0, :, sl], (((1,), (1,)), ((), ())),
                             preferred_element_type=F32) * (X_HEAD_DIM ** -0.5)
        e = jnp.exp(sc - jnp.max(sc, axis=-1, keepdims=True))
        heads.append(_dot(e.astype(BF16), v_ref[0, :, sl]) / jnp.sum(e, axis=-1, keepdims=True))
    o = jnp.concatenate(heads, axis=-1).astype(BF16)
    o_ref[0] = x + _dot(o, wo_ref[...])


def _xattn(x, g, wq, k, v, wo):
    B, S, D = x.shape
    M = k.shape[1]
    ts = ATT_TS
    return pl.pallas_call(
        _xattn_kernel,
        grid=(B, S // ts),
        in_specs=[pl.BlockSpec((1, ts, D), lambda b, s: (b, s, 0)), _const_spec((1, D)), _const_spec((D, D)),
                  pl.BlockSpec((1, M, D), lambda b, s: (b, 0, 0)), pl.BlockSpec((1, M, D), lambda b, s: (b, 0, 0)),
                  _const_spec((D, D))],
        out_specs=pl.BlockSpec((1, ts, D), lambda b, s: (b, s, 0)),
        out_shape=jax.ShapeDtypeStruct((B, S, D), F32),
        compiler_params=pltpu.CompilerParams(
            dimension_semantics=("arbitrary", "arbitrary"), vmem_limit_bytes=VMEM_LIMIT),
        name="xattn",
    )(x, g.reshape(1, -1), wq, k, v, wo)


def _store_tiles(ref, val):
    rows, width = val.shape
    for j in range(width // LANES):
        ref[:, j, :, :] = val[:, j * LANES:(j + 1) * LANES].reshape(rows // SUBLANES, SUBLANES, LANES)


def _load_tiles(ref):
    nb, k, _, _ = ref.shape
    return jnp.concatenate([ref[:, j, :, :].reshape(nb * SUBLANES, LANES) for j in range(k)], axis=1)


def _tiled_shape(rows, width, dtype):
    return jax.ShapeDtypeStruct((rows // SUBLANES, width // LANES, SUBLANES, LANES), dtype)


def _tiled_spec(rows, width, row_block, lead=None):
    block = (rows // SUBLANES, width // LANES, SUBLANES, LANES)
    if lead is None:
        return pl.BlockSpec(block, lambda *a: (row_block(*a), 0, 0, 0))
    return pl.BlockSpec((1,) + block, lambda *a: (lead, row_block(*a), 0, 0, 0))


def _piece_index(rows, k):
    rows = rows[..., None]
    return ((rows // SUBLANES) * k + jnp.arange(k, dtype=I32)) * SUBLANES + rows % SUBLANES


def _tile_order(piece_of_row):
    *lead, T, k = piece_of_row.shape
    x = piece_of_row.reshape(*lead, T // SUBLANES, SUBLANES, k)
    return jnp.swapaxes(x, -1, -2).reshape(*lead, T * k)


META_I1, META_I2, META_G1, META_G2, META_R1, META_R2 = range(6)


def _meta_col(meta, col):
    lane = lax.broadcasted_iota(I32, meta.shape, 1)
    return jnp.sum(jnp.where(lane == col, meta, 0.0), axis=-1, keepdims=True)


def _router_kernel(x_ref, g_ref, rw_ref, meta_ref, hp_ref, cnt_ref, carry):
    tm = x_ref.shape[0]

    @pl.when(pl.program_id(0) == 0)
    def _():
        carry[...] = jnp.zeros_like(carry)

    h = _rms(x_ref[...], g_ref[...])
    logits = jnp.dot(h, rw_ref[...], preferred_element_type=F32, precision=lax.Precision.HIGHEST)
    lane = lax.broadcasted_iota(I32, logits.shape, 1)
    neg = jnp.float32(-jnp.inf)
    lg = jnp.where(lane < N_EXPERTS, logits, neg)
    m1 = jnp.max(lg, axis=-1, keepdims=True)
    i1 = jnp.min(jnp.where(lg == m1, lane, LANES), axis=-1, keepdims=True)
    lg2 = jnp.where(lane == i1, neg, lg)
    m2 = jnp.max(lg2, axis=-1, keepdims=True)
    i2 = jnp.min(jnp.where(lg2 == m2, lane, LANES), axis=-1, keepdims=True)
    d = jnp.exp(m2 - m1)
    g1 = 1.0 / (1.0 + d)
    g2 = d / (1.0 + d)

    sel1 = lane == i1
    sel2 = lane == i2
    onehot = jnp.where(jnp.logical_or(sel1, sel2), 1.0, 0.0)
    row = lax.broadcasted_iota(I32, (tm, tm), 0)
    col = lax.broadcasted_iota(I32, (tm, tm), 1)
    before = _dot(jnp.where(col < row, 1.0, 0.0).astype(BF16), onehot.astype(BF16)) + carry[...]
    r1 = jnp.sum(jnp.where(sel1, before, 0.0), axis=-1, keepdims=True)
    r2 = jnp.sum(jnp.where(sel2, before, 0.0), axis=-1, keepdims=True)
    carry[...] += jnp.sum(onehot, axis=0, keepdims=True)
    cnt_ref[...] = carry[...]

    meta = jnp.zeros_like(logits)
    for c, val in ((META_I1, i1.astype(F32)), (META_I2, i2.astype(F32)), (META_G1, g1), (META_G2, g2),
                   (META_R1, r1), (META_R2, r2)):
        meta = jnp.where(lane == c, val, meta)
    meta_ref[...] = meta

    bits = lax.bitcast_convert_type(h.astype(BF16).astype(F32), U32)
    half = bits.shape[1] // 2
    _store_tiles(hp_ref, (bits[:, :half] & jnp.uint32(HI16)) | (bits[:, half:] >> 16))


def _router(x2, g, router_w):
    T, D = x2.shape
    tm = FFN_TM
    rw = jnp.zeros((D, LANES), F32).at[:, :N_EXPERTS].set(router_w)
    return pl.pallas_call(
        _router_kernel,
        grid=(T // tm,),
        in_specs=[pl.BlockSpec((tm, D), lambda i: (i, 0)), _const_spec((1, D)), _const_spec((D, LANES))],
        out_specs=[pl.BlockSpec((tm, LANES), lambda i: (i, 0)), _tiled_spec(tm, D // 2, lambda i: i),
                   _const_spec((1, LANES))],
        out_shape=[jax.ShapeDtypeStruct((T, LANES), F32), _tiled_shape(T, D // 2, U32),
                   jax.ShapeDtypeStruct((1, LANES), F32)],
        scratch_shapes=[pltpu.VMEM((1, LANES), F32)],
        compiler_params=pltpu.CompilerParams(
            dimension_semantics=("arbitrary",), vmem_limit_bytes=VMEM_LIMIT),
        name="router",
    )(x2, g.reshape(1, -1), rw)


def _sc_mesh():
    return plsc.VectorSubcoreMesh(core_axis_name="core", subcore_axis_name="subcore")


def _sc_dispatch(tiled, dest1, dest2, n_out):
    nb, k, _, _ = tiled.shape
    pieces = tiled.reshape(nb * k * SUBLANES, LANES)
    d1, d2 = (_tile_order(_piece_index(d, k)) for d in (dest1, dest2))
    out = _sc_scatter_pieces(pieces, d1, d2, n_out * k)
    return out.reshape(n_out // SUBLANES, k, SUBLANES, LANES)


def _sc_gather(tiled, idx):
    nb, k, _, _ = tiled.shape
    G, n = idx.shape
    pieces = tiled.reshape(nb * k * SUBLANES, LANES)
    out = _sc_gather_pieces(pieces, _tile_order(_piece_index(idx, k)).reshape(-1))
    return out.reshape(G, n // SUBLANES, k, SUBLANES, LANES)


def _sc_scatter_pieces(rows, dest1, dest2, n_out):
    T, W = rows.shape
    win = SC_WIN

    @pl.kernel(out_type=jax.ShapeDtypeStruct((n_out, W), rows.dtype), mesh=_sc_mesh(), scratch_types=[])
    def scatter_kernel(x_hbm, i1_hbm, i2_hbm, o_hbm):
        def body(x_vmem, i1_vmem, i2_vmem):
            pltpu.sync_copy(x_vmem, o_hbm.at[i1_vmem.at[0]])
            pltpu.sync_copy(x_vmem, o_hbm.at[i2_vmem.at[0]])

        pltpu.emit_pipeline(
            body,
            grid=(T // win,),
            in_specs=[pl.BlockSpec((win, W), lambda i: (i, 0)),
                      pl.BlockSpec((1, win), lambda i: (0, i)),
                      pl.BlockSpec((1, win), lambda i: (0, i))],
            out_specs=[],
            core_axis_name=("core", "subcore"),
            dimension_semantics=(pltpu.PARALLEL,),
        )(x_hbm, i1_hbm, i2_hbm)

    return scatter_kernel(rows, dest1.reshape(1, T), dest2.reshape(1, T))


def _sc_gather_pieces(rows, idx):
    n = idx.shape[0]
    W = rows.shape[1]
    win = SC_WIN

    @pl.kernel(out_type=jax.ShapeDtypeStruct((n, W), rows.dtype), mesh=_sc_mesh(), scratch_types=[])
    def gather_kernel(x_hbm, i_hbm, o_hbm):
        def body(i_vmem, o_vmem):
            pltpu.sync_copy(x_hbm.at[i_vmem.at[0]], o_vmem)

        pltpu.emit_pipeline(
            body,
            grid=(n // win,),
            in_specs=[pl.BlockSpec((1, win), lambda i: (0, i))],
            out_specs=[pl.BlockSpec((win, W), lambda i: (i, 0))],
            core_axis_name=("core", "subcore"),
            dimension_semantics=(pltpu.PARALLEL,),
        )(i_hbm, o_hbm)

    return gather_kernel(rows, idx.reshape(1, n))


def _swiglu(h, wg, wu, wd):
    gate = _dot(h, wg)
    up = _dot(h, wu)
    return _dot((gate * jax.nn.sigmoid(gate) * up).astype(BF16), wd)


def _ffn_kernel(x_ref, g_ref, wg_ref, wu_ref, wd_ref, o_ref, h_scr, acc):
    f = pl.program_id(1)

    @pl.when(f == 0)
    def _():
        x = x_ref[...]
        h_scr[...] = _rms(x, g_ref[...]).astype(BF16)
        acc[...] = x

    acc[...] += _swiglu(h_scr[...], wg_ref[...], wu_ref[...], wd_ref[...])

    @pl.when(f == pl.num_programs(1) - 1)
    def _():
        o_ref[...] = acc[...]


def _ffn(x2, g, wg, wu, wd):
    T, D = x2.shape
    F = wg.shape[1]
    tm, fc = FFN_TM, FFN_FC
    return pl.pallas_call(
        _ffn_kernel,
        grid=(T // tm, F // fc),
        in_specs=[pl.BlockSpec((tm, D), lambda i, f: (i, 0)), _const_spec((1, D)),
                  pl.BlockSpec((D, fc), lambda i, f: (0, f)),
                  pl.BlockSpec((D, fc), lambda i, f: (0, f)),
                  pl.BlockSpec((fc, D), lambda i, f: (f, 0))],
        out_specs=pl.BlockSpec((tm, D), lambda i, f: (i, 0)),
        out_shape=jax.ShapeDtypeStruct((T, D), F32),
        scratch_shapes=[pltpu.VMEM((tm, D), BF16), pltpu.VMEM((tm, D), F32)],
        compiler_params=pltpu.CompilerParams(
            dimension_semantics=("arbitrary", "arbitrary"), vmem_limit_bytes=VMEM_LIMIT),
        name="swiglu",
    )(x2, g.reshape(1, -1), wg, wu, wd)


def _moe_kernel(tidx_ref, texp_ref, nvalid_ref, xs_ref, wg_ref, wu_ref, wd_ref, y_ref):
    del tidx_ref, texp_ref
    F = wg_ref.shape[2]

    @pl.when(pl.program_id(0) < nvalid_ref[0])
    def _():
        w = _load_tiles(xs_ref)
        hi = lax.bitcast_convert_type(w & jnp.uint32(HI16), F32)
        lo = lax.bitcast_convert_type(w << 16, F32)
        h = jnp.concatenate([hi, lo], axis=1).astype(BF16)
        acc = None
        for c in range(F // FFN_FC):
            sl = slice(c * FFN_FC, (c + 1) * FFN_FC)
            y = _swiglu(h, wg_ref[0, :, sl], wu_ref[0, :, sl], wd_ref[0, sl, :])
            acc = y if acc is None else acc + y
        _store_tiles(y_ref, acc)


def _moe_experts(xs, tile_idx, tile_expert, n_valid, wg, wu, wd):
    E, D, F = wg.shape
    tm = MOE_TM
    P = xs.shape[0] * SUBLANES
    half = xs.shape[1] * LANES
    row_block = lambda g, ti, te, nv: ti[g]
    w_map = lambda g, ti, te, nv: (te[g], 0, 0)
    return pl.pallas_call(
        _moe_kernel,
        grid_spec=pltpu.PrefetchScalarGridSpec(
            num_scalar_prefetch=3,
            grid=(P // tm,),
            in_specs=[_tiled_spec(tm, half, row_block),
                      pl.BlockSpec((1, D, F), w_map), pl.BlockSpec((1, D, F), w_map),
                      pl.BlockSpec((1, F, D), w_map)],
            out_specs=_tiled_spec(tm, D, row_block)),
        out_shape=_tiled_shape(P, D, F32),
        compiler_params=pltpu.CompilerParams(
            dimension_semantics=("arbitrary",), vmem_limit_bytes=VMEM_LIMIT),
        name="moe_experts",
    )(tile_idx, tile_expert, n_valid, xs, wg, wu, wd)


def _combine_kernel(x_ref, y1_ref, y2_ref, meta_ref, fg_ref, o_ref, *, final_norm):
    meta = meta_ref[...]
    out = x_ref[...] + _meta_col(meta, META_G1) * _load_tiles(y1_ref.at[0])
    out = out + _meta_col(meta, META_G2) * _load_tiles(y2_ref.at[0])
    if final_norm:
        out = _rms(out, fg_ref[...])
    o_ref[...] = out


def _combine(x2, y12, meta, final_g, *, final_norm):
    T, D = x2.shape
    tm = FFN_TM
    return pl.pallas_call(
        functools.partial(_combine_kernel, final_norm=final_norm),
        grid=(T // tm,),
        in_specs=[pl.BlockSpec((tm, D), lambda i: (i, 0)),
                  _tiled_spec(tm, D, lambda i: i, lead=0), _tiled_spec(tm, D, lambda i: i, lead=1),
                  pl.BlockSpec((tm, LANES), lambda i: (i, 0)), _const_spec((1, D))],
        out_specs=pl.BlockSpec((tm, D), lambda i: (i, 0)),
        out_shape=jax.ShapeDtypeStruct((T, D), F32),
        compiler_params=pltpu.CompilerParams(
            dimension_semantics=("arbitrary",), vmem_limit_bytes=VMEM_LIMIT),
        name="moe_combine",
    )(x2, y12, y12, meta, final_g.reshape(1, -1))


def _moe_layer(x2, g, router_w, wg, wu, wd, final_g, *, final_norm):
    T, D = x2.shape
    tm = MOE_TM
    n_tiles = (T * TOP_K) // tm + N_EXPERTS
    meta, hp, counts = _router(x2, g, router_w)

    cnt = counts[0, :N_EXPERTS].astype(I32)
    tiles_e = (cnt + tm - 1) // tm
    tile_end = jnp.cumsum(tiles_e)
    row_start = (tile_end - tiles_e) * tm
    i1 = meta[:, META_I1].astype(I32)
    i2 = meta[:, META_I2].astype(I32)
    dest1 = row_start[i1] + meta[:, META_R1].astype(I32)
    dest2 = row_start[i2] + meta[:, META_R2].astype(I32)
    n_valid = tile_end[-1:]
    tile_idx = jnp.minimum(jnp.arange(n_tiles, dtype=I32), n_valid - 1)
    tile_expert = jnp.minimum(jnp.searchsorted(tile_end, tile_idx, side="right"), N_EXPERTS - 1).astype(I32)

    xs = _sc_dispatch(hp, dest1, dest2, n_tiles * tm)
    ys = _moe_experts(xs, tile_idx, tile_expert, n_valid.astype(I32), wg, wu, wd)
    y12 = _sc_gather(ys, jnp.stack([dest1, dest2]))
    return _combine(x2, y12, meta, final_g, final_norm=final_norm)


def kernel(x, mem, mix_norm_g, w_in, conv_w, conv_b, conv_ln_g, conv_ln_b, sgu_ln_g, sgu_ln_b, sgu_w, sgu_b,
           pool_w, pool_b, pool_scale, w_out, xattn_norm_g, mem_norm_g, xattn_wq, xattn_wk, xattn_wv, xattn_wo,
           ffn_norm_g, ffn_wg, ffn_wu, ffn_wd, router_w, moe_wg, moe_wu, moe_wd, final_norm_g):
    B, S, D = x.shape
    bf = lambda a: a.astype(BF16)
    for l in range(DEPTH):
        sgu_bias = jnp.repeat(sgu_b[l].T, D_SGU // SGU_HEADS, axis=1)
        pool_wbd = jax.scipy.linalg.block_diag(*[pool_w[l, gi] for gi in range(len(POOL_WINDOWS))])
        x = _mixer(x, mix_norm_g[l], bf(w_in[l]), conv_w[l], conv_b[l], conv_ln_g[l], conv_ln_b[l],
                   sgu_ln_g[l], sgu_ln_b[l], sgu_w[l], sgu_bias, bf(pool_wbd), pool_b[l].reshape(-1),
                   pool_scale[l], bf(w_out[l]))
        k, v = _kv(mem, mem_norm_g[l], bf(xattn_wk[l]), bf(xattn_wv[l]))
        x = _xattn(x, xattn_norm_g[l], bf(xattn_wq[l]), k, v, bf(xattn_wo[l]))
        x2 = x.reshape(B * S, D)
        j = l // 2
        if l % 2 == 0:
            assert l != DEPTH - 1, "the final RMSNorm is fused into the routed layer's combine kernel"
            x2 = _ffn(x2, ffn_norm_g[l], bf(ffn_wg[j]), bf(ffn_wu[j]), bf(ffn_wd[j]))
        else:
            x2 = _moe_layer(x2, ffn_norm_g[l], router_w[j], bf(moe_wg[j]), bf(moe_wu[j]), bf(moe_wd[j]),
                            final_norm_g, final_norm=l == DEPTH - 1)
        x = x2.reshape(B, S, D)
    return x
```

```python
import functools

import jax
import jax.numpy as jnp
from jax import lax
from jax.experimental import pallas as pl
from jax.experimental.pallas import tpu as pltpu
from jax.experimental.pallas import tpu_sc as plsc

F32 = jnp.float32
BF16 = jnp.bfloat16
U32 = jnp.uint32
I32 = jnp.int32

D_MODEL = 1024
DEPTH = 2
CHUNK = 64
D_CONV = 384
CONV_WIDTH = 31
D_SGU = 384
SGU_HEADS = 4
SGU_CHUNK = 128
D_POOL = 256
POOL_WINDOWS = (2, 4, 8, 16)
POOL_GROUP_DIM = D_POOL // len(POOL_WINDOWS)
D_MIX = D_CONV + D_SGU + D_POOL
D_IN = 2 * D_CONV + 2 * D_SGU + D_POOL
X_HEADS = 4
X_HEAD_DIM = D_MODEL // X_HEADS
N_EXPERTS = 8
TOP_K = 2
EPS = 1e-6

LANES = 128
SUBLANES = 8
HIST = 32
MIX_TS = 512
CONV_RB = 64
ATT_TS = 512
FFN_TM = 512
FFN_FC = 1408
MOE_TM = 512
SC_WIN = 128
HI16 = 0xFFFF0000
VMEM_LIMIT = 56 * 1024 * 1024


def _rms(x, g):
    return x * lax.rsqrt(jnp.mean(x * x, axis=-1, keepdims=True) + EPS) * g


def _layer_norm(x, g, b):
    mu = jnp.mean(x, axis=-1, keepdims=True)
    xc = x - mu
    var = jnp.mean(xc * xc, axis=-1, keepdims=True)
    return xc * lax.rsqrt(var + EPS) * g + b


def _dot(a, b):
    return jnp.dot(a, b, preferred_element_type=F32)


def _mixer_kernel(x_ref, g_ref, win_ref, convw_ref, convb_ref, clng_ref, clnb_ref,
                  slng_ref, slnb_ref, sguw_ref, sgub_ref, poolw_ref, poolb_ref, pscale_ref,
                  wout_ref, o_ref, cbuf, cshift, pb0, pb1, pb2, pb3):
    ts = MIX_TS
    s = pl.program_id(1)

    @pl.when(s == 0)
    def _():
        cbuf[0:HIST, :] = jnp.zeros((HIST, D_CONV), F32)
        pb0[0:HIST, :] = jnp.zeros((HIST, D_POOL), F32)

    x = x_ref[0]
    h = _rms(x, g_ref[...]).astype(BF16)
    z = _dot(h, win_ref[...])

    a_end = 2 * D_CONV
    cbuf[HIST:HIST + ts, :] = z[:, :D_CONV] * jax.nn.sigmoid(z[:, D_CONV:a_end])
    span = ts + HIST - SUBLANES
    for r in range(1, SUBLANES):
        cshift[r - 1, 0:span, :] = cbuf[r:r + span, :]
    ya_blocks = []
    for rb in range(ts // CONV_RB):
        acc = jnp.zeros((CONV_RB, D_CONV), F32)
        for k in range(CONV_WIDTH):
            start = rb * CONV_RB + HIST - (CONV_WIDTH - 1) + k
            r, base = start % SUBLANES, start - start % SUBLANES
            src = cbuf[base:base + CONV_RB, :] if r == 0 else cshift[r - 1, base:base + CONV_RB, :]
            acc = acc + convw_ref[k:k + 1, :] * src
        ya_blocks.append(acc)
    ya = jnp.concatenate(ya_blocks, axis=0) + convb_ref[...]
    ya = _layer_norm(ya, clng_ref[...], clnb_ref[...])
    ya = ya * jax.nn.sigmoid(ya)
    cbuf[0:HIST, :] = cbuf[ts:ts + HIST, :]

    b_end = a_end + 2 * D_SGU
    zb = jax.nn.gelu(z[:, a_end:b_end])
    u = zb[:, :D_SGU]
    v = _layer_norm(zb[:, D_SGU:], slng_ref[...], slnb_ref[...]).astype(BF16)
    blk_r = lax.broadcasted_iota(jnp.int32, (SGU_CHUNK, SGU_CHUNK), 0) // CHUNK
    blk_c = lax.broadcasted_iota(jnp.int32, (SGU_CHUNK, SGU_CHUNK), 1) // CHUNK
    head_of_lane = lax.broadcasted_iota(jnp.int32, (SGU_CHUNK, D_SGU), 1) // (D_SGU // SGU_HEADS)
    w_heads = [jnp.where(blk_r >= blk_c, sguw_ref[hd], 0.0).astype(BF16) for hd in range(SGU_HEADS)]
    s_chunks = []
    for c in range(ts // SGU_CHUNK):
        vc = v[c * SGU_CHUNK:(c + 1) * SGU_CHUNK, :]
        sc = jnp.zeros((SGU_CHUNK, D_SGU), F32)
        for hd in range(SGU_HEADS):
            sc = jnp.where(head_of_lane == hd, _dot(w_heads[hd], vc), sc)
        s_chunks.append(sc + sgub_ref[...])
    yb = u * jnp.concatenate(s_chunks, axis=0)

    cc = z[:, b_end:]
    pb0[HIST:HIST + ts, :] = cc
    pb1[0:ts + 24, :] = pb0[8:ts + 32, :] + pb0[7:ts + 31, :]
    pb2[0:ts + 16, :] = pb1[8:ts + 24, :] + pb1[6:ts + 22, :]
    pb3[0:ts + 8, :] = pb2[8:ts + 16, :] + pb2[4:ts + 12, :]
    s16 = pb3[8:ts + 8, :] + pb3[0:ts, :]
    s8 = pb3[8:ts + 8, :]
    s4 = pb2[16:ts + 16, :]
    s2 = pb1[24:ts + 24, :]
    grp = lax.broadcasted_iota(jnp.int32, (ts, D_POOL), 1) // POOL_GROUP_DIM
    wsum = jnp.where(grp == 0, s2, jnp.where(grp == 1, s4, jnp.where(grp == 2, s8, s16)))
    win = jnp.where(grp == 0, 2, jnp.where(grp == 1, 4, jnp.where(grp == 2, 8, 16)))
    pos = s * ts + lax.broadcasted_iota(jnp.int32, (ts, D_POOL), 0)
    cnt = jnp.minimum(pos + 1, win).astype(F32)
    p = (wsum / cnt - cc).astype(BF16)
    yc = (_dot(p, poolw_ref[...]) + poolb_ref[...]) * pscale_ref[...]
    pb0[0:HIST, :] = pb0[ts:ts + HIST, :]

    out = x + _dot(ya.astype(BF16), wout_ref[0:D_CONV, :])
    out = out + _dot(yb.astype(BF16), wout_ref[D_CONV:D_CONV + D_SGU, :])
    out = out + _dot(yc.astype(BF16), wout_ref[D_CONV + D_SGU:D_MIX, :])
    o_ref[0] = out


def _const_spec(shape):
    zeros = (0,) * len(shape)
    return pl.BlockSpec(shape, lambda *_: zeros)


def _mixer(x, g, w_in, conv_w, conv_b, cln_g, cln_b, sln_g, sln_b, sgu_w, sgu_bias, pool_wbd, pool_b,
           pool_scale, w_out):
    B, S, D = x.shape
    ts = MIX_TS
    row = lambda a: a.reshape(1, -1)
    args = (x, row(g), w_in, conv_w, row(conv_b), row(cln_g), row(cln_b), row(sln_g), row(sln_b),
            sgu_w, sgu_bias, pool_wbd, row(pool_b), row(pool_scale), w_out)
    in_specs = [pl.BlockSpec((1, ts, D), lambda b, s: (b, s, 0))]
    in_specs += [_const_spec(a.shape) for a in args[1:]]
    return pl.pallas_call(
        _mixer_kernel,
        grid=(B, S // ts),
        in_specs=in_specs,
        out_specs=pl.BlockSpec((1, ts, D), lambda b, s: (b, s, 0)),
        out_shape=jax.ShapeDtypeStruct((B, S, D), F32),
        scratch_shapes=[pltpu.VMEM((ts + HIST, D_CONV), F32),
                        pltpu.VMEM((SUBLANES - 1, ts + HIST - SUBLANES, D_CONV), F32)]
                       + [pltpu.VMEM((ts + HIST, D_POOL), F32)] * 4,
        compiler_params=pltpu.CompilerParams(
            dimension_semantics=("arbitrary", "arbitrary"), vmem_limit_bytes=VMEM_LIMIT),
        name="mixer",
    )(*args)


def _kv_kernel(mem_ref, g_ref, wk_ref, wv_ref, k_ref, v_ref):
    m = _rms(mem_ref[0], g_ref[...]).astype(BF16)
    k_ref[0] = _dot(m, wk_ref[...]).astype(BF16)
    v_ref[0] = _dot(m, wv_ref[...]).astype(BF16)


def _kv(mem, g, wk, wv):
    B, M, D = mem.shape
    return pl.pallas_call(
        _kv_kernel,
        grid=(B,),
        in_specs=[pl.BlockSpec((1, M, D), lambda b: (b, 0, 0)), _const_spec((1, D)),
                  _const_spec((D, D)), _const_spec((D, D))],
        out_specs=[pl.BlockSpec((1, M, D), lambda b: (b, 0, 0))] * 2,
        out_shape=[jax.ShapeDtypeStruct((B, M, D), BF16)] * 2,
        compiler_params=pltpu.CompilerParams(
            dimension_semantics=("arbitrary",), vmem_limit_bytes=VMEM_LIMIT),
        name="mem_kv",
    )(mem, g.reshape(1, -1), wk, wv)


def _xattn_kernel(x_ref, g_ref, wq_ref, k_ref, v_ref, wo_ref, o_ref):
    x = x_ref[0]
    h = _rms(x, g_ref[...]).astype(BF16)
    q = _dot(h, wq_ref[...]).astype(BF16)
    heads = []
    for hd in range(X_HEADS):
        sl = slice(hd * X_HEAD_DIM, (hd + 1) * X_HEAD_DIM)
        sc = lax.dot_general(q[:, sl], k_ref[0, :, sl], (((1,), (1,)), ((), ())),
                             preferred_element_type=F32) * (X_HEAD_DIM ** -0.5)
        e = jnp.exp(sc - jnp.max(sc, axis=-1, keepdims=True))
        heads.append(_dot(e.astype(BF16), v_ref[0, :, sl]) / jnp.sum(e, axis=-1, keepdims=True))
    o = jnp.concatenate(heads, axis=-1).astype(BF16)
    o_ref[0] = x + _dot(o, wo_ref[...])


def _xattn(x, g, wq, k, v, wo):
    B, S, D = x.shape
    M = k.shape[1]
    ts = ATT_TS
    return pl.pallas_call(
        _xattn_kernel,
        grid=(B, S // ts),
        in_specs=[pl.BlockSpec((1, ts, D), lambda b, s: (b, s, 0)), _const_spec((1, D)), _const_spec((D, D)),
                  pl.BlockSpec((1, M, D), lambda b, s: (b, 0, 0)), pl.BlockSpec((1, M, D), lambda b, s: (b, 0, 0)),
                  _const_spec((D, D))],
        out_specs=pl.BlockSpec((1, ts, D), lambda b, s: (b, s, 0)),
        out_shape=jax.ShapeDtypeStruct((B, S, D), F32),
        compiler_params=pltpu.CompilerParams(
            dimension_semantics=("arbitrary", "arbitrary"), vmem_limit_bytes=VMEM_LIMIT),
        name="xattn",
    )(x, g.reshape(1, -1), wq, k, v, wo)


def _store_tiles(ref, val):
    rows, width = val.shape
    for j in range(width // LANES):
        ref[:, j, :, :] = val[:, j * LANES:(j + 1) * LANES].reshape(rows // SUBLANES, SUBLANES, LANES)


def _load_tiles(ref):
    nb, k, _, _ = ref.shape
    return jnp.concatenate([ref[:, j, :, :].reshape(nb * SUBLANES, LANES) for j in range(k)], axis=1)


def _pack_bf16_pairs(v):
    bits = lax.bitcast_convert_type(v.astype(BF16).astype(F32), U32)
    half = bits.shape[1] // 2
    return (bits[:, :half] & jnp.uint32(HI16)) | (bits[:, half:] >> 16)


def _unpack_bf16_pairs(w):
    hi = lax.bitcast_convert_type(w & jnp.uint32(HI16), F32)
    lo = lax.bitcast_convert_type(w << 16, F32)
    return jnp.concatenate([hi, lo], axis=1)


def _tiled_shape(rows, width, dtype):
    return jax.ShapeDtypeStruct((rows // SUBLANES, width // LANES, SUBLANES, LANES), dtype)


def _tiled_spec(rows, width, row_block, lead=None):
    block = (rows // SUBLANES, width // LANES, SUBLANES, LANES)
    if lead is None:
        return pl.BlockSpec(block, lambda *a: (row_block(*a), 0, 0, 0))
    return pl.BlockSpec((1,) + block, lambda *a: (lead, row_block(*a), 0, 0, 0))


def _piece_index(rows, k):
    rows = rows[..., None]
    return ((rows // SUBLANES) * k + jnp.arange(k, dtype=I32)) * SUBLANES + rows % SUBLANES


def _tile_order(piece_of_row):
    *lead, T, k = piece_of_row.shape
    x = piece_of_row.reshape(*lead, T // SUBLANES, SUBLANES, k)
    return jnp.swapaxes(x, -1, -2).reshape(*lead, T * k)


META_I1, META_I2, META_G1, META_G2, META_R1, META_R2 = range(6)


def _meta_col(meta, col):
    lane = lax.broadcasted_iota(I32, meta.shape, 1)
    return jnp.sum(jnp.where(lane == col, meta, 0.0), axis=-1, keepdims=True)


def _router_kernel(x_ref, g_ref, rw_ref, meta_ref, metat_ref, hp_ref, cnt_ref, carry):
    tm = x_ref.shape[0]

    @pl.when(pl.program_id(0) == 0)
    def _():
        carry[...] = jnp.zeros_like(carry)

    h = _rms(x_ref[...], g_ref[...])
    logits = jnp.dot(h, rw_ref[...], preferred_element_type=F32, precision=lax.Precision.HIGHEST)
    lane = lax.broadcasted_iota(I32, logits.shape, 1)
    neg = jnp.float32(-jnp.inf)
    lg = jnp.where(lane < N_EXPERTS, logits, neg)
    m1 = jnp.max(lg, axis=-1, keepdims=True)
    i1 = jnp.min(jnp.where(lg == m1, lane, LANES), axis=-1, keepdims=True)
    lg2 = jnp.where(lane == i1, neg, lg)
    m2 = jnp.max(lg2, axis=-1, keepdims=True)
    i2 = jnp.min(jnp.where(lg2 == m2, lane, LANES), axis=-1, keepdims=True)
    d = jnp.exp(m2 - m1)
    g1 = 1.0 / (1.0 + d)
    g2 = d / (1.0 + d)

    sel1 = lane == i1
    sel2 = lane == i2
    onehot = jnp.where(jnp.logical_or(sel1, sel2), 1.0, 0.0)
    row = lax.broadcasted_iota(I32, (tm, tm), 0)
    col = lax.broadcasted_iota(I32, (tm, tm), 1)
    before = _dot(jnp.where(col < row, 1.0, 0.0).astype(BF16), onehot.astype(BF16)) + carry[...]
    r1 = jnp.sum(jnp.where(sel1, before, 0.0), axis=-1, keepdims=True)
    r2 = jnp.sum(jnp.where(sel2, before, 0.0), axis=-1, keepdims=True)
    carry[...] += jnp.sum(onehot, axis=0, keepdims=True)
    cnt_ref[...] = carry[...]

    meta = jnp.zeros_like(logits)
    for c, val in ((META_I1, i1.astype(F32)), (META_I2, i2.astype(F32)), (META_G1, g1), (META_G2, g2),
                   (META_R1, r1), (META_R2, r2)):
        meta = jnp.where(lane == c, val, meta)
    meta_ref[...] = meta

    _store_tiles(hp_ref, _pack_bf16_pairs(h))
    metat_ref[...] = meta.T[:SUBLANES, :]


def _router(x2, g, router_w):
    T, D = x2.shape
    tm = FFN_TM
    rw = jnp.zeros((D, LANES), F32).at[:, :N_EXPERTS].set(router_w)
    return pl.pallas_call(
        _router_kernel,
        grid=(T // tm,),
        in_specs=[pl.BlockSpec((tm, D), lambda i: (i, 0)), _const_spec((1, D)), _const_spec((D, LANES))],
        out_specs=[pl.BlockSpec((tm, LANES), lambda i: (i, 0)), pl.BlockSpec((SUBLANES, tm), lambda i: (0, i)),
                   _tiled_spec(tm, D // 2, lambda i: i), _const_spec((1, LANES))],
        out_shape=[jax.ShapeDtypeStruct((T, LANES), F32), jax.ShapeDtypeStruct((SUBLANES, T), F32),
                   _tiled_shape(T, D // 2, U32), jax.ShapeDtypeStruct((1, LANES), F32)],
        scratch_shapes=[pltpu.VMEM((1, LANES), F32)],
        compiler_params=pltpu.CompilerParams(
            dimension_semantics=("arbitrary",), vmem_limit_bytes=VMEM_LIMIT),
        name="router",
    )(x2, g.reshape(1, -1), rw)


def _sc_mesh():
    return plsc.VectorSubcoreMesh(core_axis_name="core", subcore_axis_name="subcore")


def _sc_dispatch(tiled, dest1, dest2, n_out):
    nb, k, _, _ = tiled.shape
    pieces = tiled.reshape(nb * k * SUBLANES, LANES)
    d1, d2 = (_tile_order(_piece_index(d, k)) for d in (dest1, dest2))
    out = _sc_scatter_pieces(pieces, d1, d2, n_out * k)
    return out.reshape(n_out // SUBLANES, k, SUBLANES, LANES)


def _sc_gather(tiled, idx):
    nb, k, _, _ = tiled.shape
    G, n = idx.shape
    pieces = tiled.reshape(nb * k * SUBLANES, LANES)
    out = _sc_gather_pieces(pieces, _tile_order(_piece_index(idx, k)).reshape(-1))
    return out.reshape(G, n // SUBLANES, k, SUBLANES, LANES)


def _sc_scatter_pieces(rows, dest1, dest2, n_out):
    T, W = rows.shape
    win = SC_WIN

    @pl.kernel(out_type=jax.ShapeDtypeStruct((n_out, W), rows.dtype), mesh=_sc_mesh(), scratch_types=[])
    def scatter_kernel(x_hbm, i1_hbm, i2_hbm, o_hbm):
        def body(x_vmem, i1_vmem, i2_vmem):
            pltpu.sync_copy(x_vmem, o_hbm.at[i1_vmem.at[0]])
            pltpu.sync_copy(x_vmem, o_hbm.at[i2_vmem.at[0]])

        pltpu.emit_pipeline(
            body,
            grid=(T // win,),
            in_specs=[pl.BlockSpec((win, W), lambda i: (i, 0)),
                      pl.BlockSpec((1, win), lambda i: (0, i)),
                      pl.BlockSpec((1, win), lambda i: (0, i))],
            out_specs=[],
            core_axis_name=("core", "subcore"),
            dimension_semantics=(pltpu.PARALLEL,),
        )(x_hbm, i1_hbm, i2_hbm)

    return scatter_kernel(rows, dest1.reshape(1, T), dest2.reshape(1, T))


def _sc_gather_pieces(rows, idx):
    n = idx.shape[0]
    W = rows.shape[1]
    win = SC_WIN

    @pl.kernel(out_type=jax.ShapeDtypeStruct((n, W), rows.dtype), mesh=_sc_mesh(), scratch_types=[])
    def gather_kernel(x_hbm, i_hbm, o_hbm):
        def body(i_vmem, o_vmem):
            pltpu.sync_copy(x_hbm.at[i_vmem.at[0]], o_vmem)

        pltpu.emit_pipeline(
            body,
            grid=(n // win,),
            in_specs=[pl.BlockSpec((1, win), lambda i: (0, i))],
            out_specs=[pl.BlockSpec((win, W), lambda i: (i, 0))],
            core_axis_name=("core", "subcore"),
            dimension_semantics=(pltpu.PARALLEL,),
        )(i_hbm, o_hbm)

    return gather_kernel(rows, idx.reshape(1, n))


def _swiglu(h, wg, wu, wd):
    gate = _dot(h, wg)
    up = _dot(h, wu)
    return _dot((gate * jax.nn.sigmoid(gate) * up).astype(BF16), wd)


def _swiglu_chunked(h, wg_ref, wu_ref, wd_ref, acc=None):
    F = wg_ref.shape[1]
    for c in range(F // FFN_FC):
        sl = slice(c * FFN_FC, (c + 1) * FFN_FC)
        y = _swiglu(h, wg_ref[:, sl], wu_ref[:, sl], wd_ref[sl, :])
        acc = y if acc is None else acc + y
    return acc


def _ffn_kernel(x_ref, g_ref, wg_ref, wu_ref, wd_ref, o_ref):
    x = x_ref[...]
    h = _rms(x, g_ref[...]).astype(BF16)
    o_ref[...] = _swiglu_chunked(h, wg_ref, wu_ref, wd_ref, acc=x)


def _ffn(x2, g, wg, wu, wd):
    T, D = x2.shape
    F = wg.shape[1]
    tm = FFN_TM
    return pl.pallas_call(
        _ffn_kernel,
        grid=(T // tm,),
        in_specs=[pl.BlockSpec((tm, D), lambda i: (i, 0)), _const_spec((1, D)),
                  _const_spec((D, F)), _const_spec((D, F)), _const_spec((F, D))],
        out_specs=pl.BlockSpec((tm, D), lambda i: (i, 0)),
        out_shape=jax.ShapeDtypeStruct((T, D), F32),
        compiler_params=pltpu.CompilerParams(
            dimension_semantics=("arbitrary",), vmem_limit_bytes=VMEM_LIMIT),
        name="swiglu",
    )(x2, g.reshape(1, -1), wg, wu, wd)


def _moe_kernel(tidx_ref, texp_ref, nvalid_ref, xs_ref, wg_ref, wu_ref, wd_ref, y_ref):
    del tidx_ref, texp_ref

    @pl.when(pl.program_id(0) < nvalid_ref[0])
    def _():
        h = _unpack_bf16_pairs(_load_tiles(xs_ref)).astype(BF16)
        _store_tiles(y_ref, _pack_bf16_pairs(_swiglu_chunked(h, wg_ref.at[0], wu_ref.at[0], wd_ref.at[0])))


def _moe_experts(xs, tile_idx, tile_expert, n_valid, wg, wu, wd):
    E, D, F = wg.shape
    tm = MOE_TM
    P = xs.shape[0] * SUBLANES
    half = xs.shape[1] * LANES
    row_block = lambda g, ti, te, nv: ti[g]
    w_map = lambda g, ti, te, nv: (te[g], 0, 0)
    return pl.pallas_call(
        _moe_kernel,
        grid_spec=pltpu.PrefetchScalarGridSpec(
            num_scalar_prefetch=3,
            grid=(P // tm,),
            in_specs=[_tiled_spec(tm, half, row_block),
                      pl.BlockSpec((1, D, F), w_map), pl.BlockSpec((1, D, F), w_map),
                      pl.BlockSpec((1, F, D), w_map)],
            out_specs=_tiled_spec(tm, D // 2, row_block)),
        out_shape=_tiled_shape(P, D // 2, U32),
        compiler_params=pltpu.CompilerParams(
            dimension_semantics=("arbitrary",), vmem_limit_bytes=VMEM_LIMIT),
        name="moe_experts",
    )(tile_idx, tile_expert, n_valid, xs, wg, wu, wd)


def _combine_kernel(x_ref, y1_ref, y2_ref, meta_ref, fg_ref, o_ref, *, final_norm):
    meta = meta_ref[...]
    out = x_ref[...] + _meta_col(meta, META_G1) * _unpack_bf16_pairs(_load_tiles(y1_ref.at[0]))
    out = out + _meta_col(meta, META_G2) * _unpack_bf16_pairs(_load_tiles(y2_ref.at[0]))
    if final_norm:
        out = _rms(out, fg_ref[...])
    o_ref[...] = out


def _combine(x2, y12, meta, final_g, *, final_norm):
    T, D = x2.shape
    tm = FFN_TM
    return pl.pallas_call(
        functools.partial(_combine_kernel, final_norm=final_norm),
        grid=(T // tm,),
        in_specs=[pl.BlockSpec((tm, D), lambda i: (i, 0)),
                  _tiled_spec(tm, D // 2, lambda i: i, lead=0), _tiled_spec(tm, D // 2, lambda i: i, lead=1),
                  pl.BlockSpec((tm, LANES), lambda i: (i, 0)), _const_spec((1, D))],
        out_specs=pl.BlockSpec((tm, D), lambda i: (i, 0)),
        out_shape=jax.ShapeDtypeStruct((T, D), F32),
        compiler_params=pltpu.CompilerParams(
            dimension_semantics=("arbitrary",), vmem_limit_bytes=VMEM_LIMIT),
        name="moe_combine",
    )(x2, y12, y12, meta, final_g.reshape(1, -1))


def _moe_layer(x2, g, router_w, wg, wu, wd, final_g, *, final_norm):
    T, D = x2.shape
    tm = MOE_TM
    n_tiles = (T * TOP_K) // tm + N_EXPERTS
    meta, meta_t, hp, counts = _router(x2, g, router_w)

    cnt = counts[0, :N_EXPERTS].astype(I32)
    tiles_e = (cnt + tm - 1) // tm
    tile_end = jnp.cumsum(tiles_e)
    row_start = (tile_end - tiles_e) * tm
    experts = jnp.arange(N_EXPERTS, dtype=I32)[:, None]

    def dest(i_row, r_row):
        start = jnp.sum(jnp.where(meta_t[i_row].astype(I32) == experts, row_start[:, None], 0), axis=0)
        return start + meta_t[r_row].astype(I32)

    dest1 = dest(META_I1, META_R1)
    dest2 = dest(META_I2, META_R2)
    n_valid = tile_end[-1:]
    tile_idx = jnp.minimum(jnp.arange(n_tiles, dtype=I32), n_valid - 1)
    tile_expert = jnp.minimum(jnp.sum(tile_end[:, None] <= tile_idx, axis=0), N_EXPERTS - 1).astype(I32)

    xs = _sc_dispatch(hp, dest1, dest2, n_tiles * tm)
    ys = _moe_experts(xs, tile_idx, tile_expert, n_valid.astype(I32), wg, wu, wd)
    y12 = _sc_gather(ys, jnp.stack([dest1, dest2]))
    return _combine(x2, y12, meta, final_g, final_norm=final_norm)


def kernel(x, mem, mix_norm_g, w_in, conv_w, conv_b, conv_ln_g, conv_ln_b, sgu_ln_g, sgu_ln_b, sgu_w, sgu_b,
           pool_w, pool_b, pool_scale, w_out, xattn_norm_g, mem_norm_g, xattn_wq, xattn_wk, xattn_wv, xattn_wo,
           ffn_norm_g, ffn_wg, ffn_wu, ffn_wd, router_w, moe_wg, moe_wu, moe_wd, final_norm_g):
    B, S, D = x.shape
    bf = lambda a: a.astype(BF16)
    for l in range(DEPTH):
        sgu_bias = jnp.repeat(sgu_b[l].T, D_SGU // SGU_HEADS, axis=1)
        pool_wbd = jax.scipy.linalg.block_diag(*[pool_w[l, gi] for gi in range(len(POOL_WINDOWS))])
        x = _mixer(x, mix_norm_g[l], bf(w_in[l]), conv_w[l], conv_b[l], conv_ln_g[l], conv_ln_b[l],
                   sgu_ln_g[l], sgu_ln_b[l], sgu_w[l], sgu_bias, bf(pool_wbd), pool_b[l].reshape(-1),
                   pool_scale[l], bf(w_out[l]))
        k, v = _kv(mem, mem_norm_g[l], bf(xattn_wk[l]), bf(xattn_wv[l]))
        x = _xattn(x, xattn_norm_g[l], bf(xattn_wq[l]), k, v, bf(xattn_wo[l]))
        x2 = x.reshape(B * S, D)
        j = l // 2
        if l % 2 == 0:
            assert l != DEPTH - 1, "the final RMSNorm is fused into the routed layer's combine kernel"
            x2 = _ffn(x2, ffn_norm_g[l], bf(ffn_wg[j]), bf(ffn_wu[j]), bf(ffn_wd[j]))
        else:
            x2 = _moe_layer(x2, ffn_norm_g[l], router_w[j], bf(moe_wg[j]), bf(moe_wu[j]), bf(moe_wd[j]),
                            final_norm_g, final_norm=l == DEPTH - 1)
        x = x2.reshape(B, S, D)
    return x
```

```python
import functools

import jax
import jax.numpy as jnp
from jax import lax
from jax.experimental import pallas as pl
from jax.experimental.pallas import tpu as pltpu
from jax.experimental.pallas import tpu_sc as plsc

F32 = jnp.float32
BF16 = jnp.bfloat16
U32 = jnp.uint32
I32 = jnp.int32

D_MODEL = 1024
DEPTH = 2
CHUNK = 64
D_CONV = 384
CONV_WIDTH = 31
D_SGU = 384
SGU_HEADS = 4
SGU_CHUNK = 128
D_POOL = 256
POOL_WINDOWS = (2, 4, 8, 16)
POOL_GROUP_DIM = D_POOL // len(POOL_WINDOWS)
D_MIX = D_CONV + D_SGU + D_POOL
D_IN = 2 * D_CONV + 2 * D_SGU + D_POOL
X_HEADS = 4
X_HEAD_DIM = D_MODEL // X_HEADS
N_EXPERTS = 8
TOP_K = 2
EPS = 1e-6

LANES = 128
SUBLANES = 8
HIST = 32
MIX_TS = 512
CONV_RB = 64
ATT_TS = 512
FFN_TM = 512
FFN_FC = 1408
MXU_WIDTH = 256
MOE_TM = 512
SC_WIN = 128
HI16 = 0xFFFF0000
VMEM_LIMIT = 56 * 1024 * 1024


def _rms(x, g):
    return x * lax.rsqrt(jnp.mean(x * x, axis=-1, keepdims=True) + EPS) * g


def _layer_norm(x, g, b):
    mu = jnp.mean(x, axis=-1, keepdims=True)
    xc = x - mu
    var = jnp.mean(xc * xc, axis=-1, keepdims=True)
    return xc * lax.rsqrt(var + EPS) * g + b


def _dot(a, b):
    return jnp.dot(a, b, preferred_element_type=F32)


def _mixer_kernel(x_ref, g_ref, win_ref, convw_ref, convb_ref, clng_ref, clnb_ref,
                  slng_ref, slnb_ref, sguw_ref, sgub_ref, poolw_ref, poolb_ref, pscale_ref,
                  wout_ref, o_ref, cbuf, cshift, pb0, pb1, pb2, pb3):
    ts = MIX_TS
    s = pl.program_id(1)

    @pl.when(s == 0)
    def _():
        cbuf[0:HIST, :] = jnp.zeros((HIST, D_CONV), F32)
        pb0[0:HIST, :] = jnp.zeros((HIST, D_POOL), F32)

    x = x_ref[0]
    h = _rms(x, g_ref[...]).astype(BF16)
    z = _dot(h, win_ref[...])

    a_end = 2 * D_CONV
    cbuf[HIST:HIST + ts, :] = z[:, :D_CONV] * jax.nn.sigmoid(z[:, D_CONV:a_end])
    span = ts + HIST - SUBLANES
    for r in range(1, SUBLANES):
        cshift[r - 1, 0:span, :] = cbuf[r:r + span, :]
    ya_blocks = []
    for rb in range(ts // CONV_RB):
        acc = jnp.zeros((CONV_RB, D_CONV), F32)
        for k in range(CONV_WIDTH):
            start = rb * CONV_RB + HIST - (CONV_WIDTH - 1) + k
            r, base = start % SUBLANES, start - start % SUBLANES
            src = cbuf[base:base + CONV_RB, :] if r == 0 else cshift[r - 1, base:base + CONV_RB, :]
            acc = acc + convw_ref[k:k + 1, :] * src
        ya_blocks.append(acc)
    ya = jnp.concatenate(ya_blocks, axis=0) + convb_ref[...]
    ya = _layer_norm(ya, clng_ref[...], clnb_ref[...])
    ya = ya * jax.nn.sigmoid(ya)
    cbuf[0:HIST, :] = cbuf[ts:ts + HIST, :]

    b_end = a_end + 2 * D_SGU
    zb = jax.nn.gelu(z[:, a_end:b_end])
    u = zb[:, :D_SGU]
    v = _layer_norm(zb[:, D_SGU:], slng_ref[...], slnb_ref[...]).astype(BF16)
    blk_r = lax.broadcasted_iota(jnp.int32, (SGU_CHUNK, SGU_CHUNK), 0) // CHUNK
    blk_c = lax.broadcasted_iota(jnp.int32, (SGU_CHUNK, SGU_CHUNK), 1) // CHUNK
    head_of_lane = lax.broadcasted_iota(jnp.int32, (SGU_CHUNK, D_SGU), 1) // (D_SGU // SGU_HEADS)
    w_heads = [jnp.where(blk_r >= blk_c, sguw_ref[hd], 0.0).astype(BF16) for hd in range(SGU_HEADS)]
    s_chunks = []
    for c in range(ts // SGU_CHUNK):
        vc = v[c * SGU_CHUNK:(c + 1) * SGU_CHUNK, :]
        sc = jnp.zeros((SGU_CHUNK, D_SGU), F32)
        for hd in range(SGU_HEADS):
            sc = jnp.where(head_of_lane == hd, _dot(w_heads[hd], vc), sc)
        s_chunks.append(sc + sgub_ref[...])
    yb = u * jnp.concatenate(s_chunks, axis=0)

    cc = z[:, b_end:]
    pb0[HIST:HIST + ts, :] = cc
    pb1[0:ts + 24, :] = pb0[8:ts + 32, :] + pb0[7:ts + 31, :]
    pb2[0:ts + 16, :] = pb1[8:ts + 24, :] + pb1[6:ts + 22, :]
    pb3[0:ts + 8, :] = pb2[8:ts + 16, :] + pb2[4:ts + 12, :]
    s16 = pb3[8:ts + 8, :] + pb3[0:ts, :]
    s8 = pb3[8:ts + 8, :]
    s4 = pb2[16:ts + 16, :]
    s2 = pb1[24:ts + 24, :]
    grp = lax.broadcasted_iota(jnp.int32, (ts, D_POOL), 1) // POOL_GROUP_DIM
    wsum = jnp.where(grp == 0, s2, jnp.where(grp == 1, s4, jnp.where(grp == 2, s8, s16)))
    win = jnp.where(grp == 0, 2, jnp.where(grp == 1, 4, jnp.where(grp == 2, 8, 16)))
    pos = s * ts + lax.broadcasted_iota(jnp.int32, (ts, D_POOL), 0)
    cnt = jnp.minimum(pos + 1, win).astype(F32)
    p = (wsum / cnt - cc).astype(BF16)
    yc = (_dot(p, poolw_ref[...]) + poolb_ref[...]) * pscale_ref[...]
    pb0[0:HIST, :] = pb0[ts:ts + HIST, :]

    mixed = jnp.concatenate([ya.astype(BF16), yb.astype(BF16), yc.astype(BF16)], axis=-1)
    o_ref[0] = x + _dot(mixed, wout_ref[...])


def _const_spec(shape):
    zeros = (0,) * len(shape)
    return pl.BlockSpec(shape, lambda *_: zeros)


def _mixer(x, g, w_in, conv_w, conv_b, cln_g, cln_b, sln_g, sln_b, sgu_w, sgu_bias, pool_wbd, pool_b,
           pool_scale, w_out):
    B, S, D = x.shape
    ts = MIX_TS
    row = lambda a: a.reshape(1, -1)
    args = (x, row(g), w_in, conv_w, row(conv_b), row(cln_g), row(cln_b), row(sln_g), row(sln_b),
            sgu_w, sgu_bias, pool_wbd, row(pool_b), row(pool_scale), w_out)
    in_specs = [pl.BlockSpec((1, ts, D), lambda b, s: (b, s, 0))]
    in_specs += [_const_spec(a.shape) for a in args[1:]]
    return pl.pallas_call(
        _mixer_kernel,
        grid=(B, S // ts),
        in_specs=in_specs,
        out_specs=pl.BlockSpec((1, ts, D), lambda b, s: (b, s, 0)),
        out_shape=jax.ShapeDtypeStruct((B, S, D), F32),
        scratch_shapes=[pltpu.VMEM((ts + HIST, D_CONV), F32),
                        pltpu.VMEM((SUBLANES - 1, ts + HIST - SUBLANES, D_CONV), F32)]
                       + [pltpu.VMEM((ts + HIST, D_POOL), F32)] * 4,
        compiler_params=pltpu.CompilerParams(
            dimension_semantics=("arbitrary", "arbitrary"), vmem_limit_bytes=VMEM_LIMIT),
        name="mixer",
    )(*args)


def _kv_kernel(mem_ref, g_ref, wk_ref, wv_ref, k_ref, v_ref):
    m = _rms(mem_ref[0], g_ref[...]).astype(BF16)
    k_ref[0] = _dot(m, wk_ref[...]).astype(BF16)
    v_ref[0] = _dot(m, wv_ref[...]).astype(BF16)


def _kv(mem, g, wk, wv):
    B, M, D = mem.shape
    return pl.pallas_call(
        _kv_kernel,
        grid=(B,),
        in_specs=[pl.BlockSpec((1, M, D), lambda b: (b, 0, 0)), _const_spec((1, D)),
                  _const_spec((D, D)), _const_spec((D, D))],
        out_specs=[pl.BlockSpec((1, M, D), lambda b: (b, 0, 0))] * 2,
        out_shape=[jax.ShapeDtypeStruct((B, M, D), BF16)] * 2,
        compiler_params=pltpu.CompilerParams(
            dimension_semantics=("arbitrary",), vmem_limit_bytes=VMEM_LIMIT),
        name="mem_kv",
    )(mem, g.reshape(1, -1), wk, wv)


def _xattn_kernel(x_ref, g_ref, wq_ref, k_ref, v_ref, wo_ref, *rest, route):
    x = x_ref[0]
    h = _rms(x, g_ref[...]).astype(BF16)
    q = _dot(h, wq_ref[...]).astype(BF16)
    heads = []
    for hd in range(X_HEADS):
        sl = slice(hd * X_HEAD_DIM, (hd + 1) * X_HEAD_DIM)
        sc = lax.dot_general(q[:, sl], k_ref[0, :, sl], (((1,), (1,)), ((), ())),
                             preferred_element_type=F32) * (X_HEAD_DIM ** -0.5)
        e = jnp.exp(sc - jnp.max(sc, axis=-1, keepdims=True))
        heads.append(_dot(e.astype(BF16), v_ref[0, :, sl]) / jnp.sum(e, axis=-1, keepdims=True))
    o = jnp.concatenate(heads, axis=-1).astype(BF16)
    out = x + _dot(o, wo_ref[...])
    if route:
        g2_ref, rw_ref, o_ref, *route_refs = rest
        first = jnp.logical_and(pl.program_id(0) == 0, pl.program_id(1) == 0)
        _route_tile(out, first, g2_ref, rw_ref, *route_refs)
    else:
        (o_ref,) = rest
    o_ref[0] = out


def _xattn(x, g, wq, k, v, wo, router=None):
    B, S, D = x.shape
    M = k.shape[1]
    ts = ATT_TS
    T = B * S
    n_s = S // ts
    args = [x, g.reshape(1, -1), wq, k, v, wo]
    in_specs = [pl.BlockSpec((1, ts, D), lambda b, s: (b, s, 0)), _const_spec((1, D)), _const_spec((D, D)),
                pl.BlockSpec((1, M, D), lambda b, s: (b, 0, 0)), pl.BlockSpec((1, M, D), lambda b, s: (b, 0, 0)),
                _const_spec((D, D))]
    out_specs = [pl.BlockSpec((1, ts, D), lambda b, s: (b, s, 0))]
    out_shape = [jax.ShapeDtypeStruct((B, S, D), F32)]
    scratch = []
    if router is not None:
        g2, router_w = router
        args += [g2.reshape(1, -1), jnp.zeros((D, LANES), F32).at[:, :N_EXPERTS].set(router_w)]
        in_specs += [_const_spec((1, D)), _const_spec((D, LANES))]
        tile = lambda b, s: b * n_s + s
        out_specs += [pl.BlockSpec((ts, LANES), lambda b, s: (tile(b, s), 0)),
                      pl.BlockSpec((SUBLANES, ts), lambda b, s: (0, tile(b, s))),
                      _tiled_spec(ts, D // 2, tile), _const_spec((1, LANES))]
        out_shape += [jax.ShapeDtypeStruct((T, LANES), F32), jax.ShapeDtypeStruct((SUBLANES, T), F32),
                      _tiled_shape(T, D // 2, U32), jax.ShapeDtypeStruct((1, LANES), F32)]
        scratch = [pltpu.VMEM((1, LANES), F32)]
    outs = pl.pallas_call(
        functools.partial(_xattn_kernel, route=router is not None),
        grid=(B, n_s),
        in_specs=in_specs,
        out_specs=out_specs,
        out_shape=out_shape,
        scratch_shapes=scratch,
        compiler_params=pltpu.CompilerParams(
            dimension_semantics=("arbitrary", "arbitrary"), vmem_limit_bytes=VMEM_LIMIT),
        name="xattn_route" if router is not None else "xattn",
    )(*args)
    return outs if router is not None else outs[0]


def _store_tiles(ref, val):
    rows, width = val.shape
    for j in range(width // LANES):
        ref[:, j, :, :] = val[:, j * LANES:(j + 1) * LANES].reshape(rows // SUBLANES, SUBLANES, LANES)


def _load_tiles(ref):
    nb, k, _, _ = ref.shape
    return jnp.concatenate([ref[:, j, :, :].reshape(nb * SUBLANES, LANES) for j in range(k)], axis=1)


def _pack_bf16_pairs(v):
    bits = lax.bitcast_convert_type(v.astype(BF16).astype(F32), U32)
    half = bits.shape[1] // 2
    return (bits[:, :half] & jnp.uint32(HI16)) | (bits[:, half:] >> 16)


def _unpack_bf16_pairs(w):
    hi = lax.bitcast_convert_type(w & jnp.uint32(HI16), F32)
    lo = lax.bitcast_convert_type(w << 16, F32)
    return jnp.concatenate([hi, lo], axis=1)


def _tiled_shape(rows, width, dtype):
    return jax.ShapeDtypeStruct((rows // SUBLANES, width // LANES, SUBLANES, LANES), dtype)


def _tiled_spec(rows, width, row_block, lead=None):
    block = (rows // SUBLANES, width // LANES, SUBLANES, LANES)
    if lead is None:
        return pl.BlockSpec(block, lambda *a: (row_block(*a), 0, 0, 0))
    return pl.BlockSpec((1,) + block, lambda *a: (lead, row_block(*a), 0, 0, 0))


def _piece_index(rows, k):
    rows = rows[..., None]
    return ((rows // SUBLANES) * k + jnp.arange(k, dtype=I32)) * SUBLANES + rows % SUBLANES


def _tile_order(piece_of_row):
    *lead, T, k = piece_of_row.shape
    x = piece_of_row.reshape(*lead, T // SUBLANES, SUBLANES, k)
    return jnp.swapaxes(x, -1, -2).reshape(*lead, T * k)


META_I1, META_I2, META_G1, META_G2, META_R1, META_R2 = range(6)


def _meta_col(meta, col):
    lane = lax.broadcasted_iota(I32, meta.shape, 1)
    return jnp.sum(jnp.where(lane == col, meta, 0.0), axis=-1, keepdims=True)


def _route_tile(x, first, g_ref, rw_ref, meta_ref, metat_ref, hp_ref, cnt_ref, carry):
    tm = x.shape[0]

    @pl.when(first)
    def _():
        carry[...] = jnp.zeros_like(carry)

    h = _rms(x, g_ref[...])
    logits = jnp.dot(h, rw_ref[...], preferred_element_type=F32, precision=lax.Precision.HIGHEST)
    lane = lax.broadcasted_iota(I32, logits.shape, 1)
    neg = jnp.float32(-jnp.inf)
    lg = jnp.where(lane < N_EXPERTS, logits, neg)
    m1 = jnp.max(lg, axis=-1, keepdims=True)
    i1 = jnp.min(jnp.where(lg == m1, lane, LANES), axis=-1, keepdims=True)
    lg2 = jnp.where(lane == i1, neg, lg)
    m2 = jnp.max(lg2, axis=-1, keepdims=True)
    i2 = jnp.min(jnp.where(lg2 == m2, lane, LANES), axis=-1, keepdims=True)
    d = jnp.exp(m2 - m1)
    g1 = 1.0 / (1.0 + d)
    g2 = d / (1.0 + d)

    sel1 = lane == i1
    sel2 = lane == i2
    onehot = jnp.where(jnp.logical_or(sel1, sel2), 1.0, 0.0)
    row = lax.broadcasted_iota(I32, (tm, tm), 0)
    col = lax.broadcasted_iota(I32, (tm, tm), 1)
    before = _dot(jnp.where(col < row, 1.0, 0.0).astype(BF16), onehot.astype(BF16)) + carry[...]
    r1 = jnp.sum(jnp.where(sel1, before, 0.0), axis=-1, keepdims=True)
    r2 = jnp.sum(jnp.where(sel2, before, 0.0), axis=-1, keepdims=True)
    carry[...] += jnp.sum(onehot, axis=0, keepdims=True)
    cnt_ref[...] = carry[...]

    meta = jnp.zeros_like(logits)
    for c, val in ((META_I1, i1.astype(F32)), (META_I2, i2.astype(F32)), (META_G1, g1), (META_G2, g2),
                   (META_R1, r1), (META_R2, r2)):
        meta = jnp.where(lane == c, val, meta)
    meta_ref[...] = meta

    _store_tiles(hp_ref, _pack_bf16_pairs(h))
    metat_ref[...] = meta.T[:SUBLANES, :]


def _sc_mesh():
    return plsc.VectorSubcoreMesh(core_axis_name="core", subcore_axis_name="subcore")


def _sc_dispatch(tiled, dest1, dest2, n_out):
    nb, k, _, _ = tiled.shape
    pieces = tiled.reshape(nb * k * SUBLANES, LANES)
    d1, d2 = (_tile_order(_piece_index(d, k)) for d in (dest1, dest2))
    out = _sc_scatter_pieces(pieces, d1, d2, n_out * k)
    return out.reshape(n_out // SUBLANES, k, SUBLANES, LANES)


def _sc_gather(tiled, idx):
    nb, k, _, _ = tiled.shape
    G, n = idx.shape
    pieces = tiled.reshape(nb * k * SUBLANES, LANES)
    out = _sc_gather_pieces(pieces, _tile_order(_piece_index(idx, k)).reshape(-1))
    return out.reshape(G, n // SUBLANES, k, SUBLANES, LANES)


def _sc_scatter_pieces(rows, dest1, dest2, n_out):
    T, W = rows.shape
    win = SC_WIN

    @pl.kernel(out_type=jax.ShapeDtypeStruct((n_out, W), rows.dtype), mesh=_sc_mesh(), scratch_types=[])
    def scatter_kernel(x_hbm, i1_hbm, i2_hbm, o_hbm):
        def body(x_vmem, i1_vmem, i2_vmem):
            pltpu.sync_copy(x_vmem, o_hbm.at[i1_vmem.at[0]])
            pltpu.sync_copy(x_vmem, o_hbm.at[i2_vmem.at[0]])

        pltpu.emit_pipeline(
            body,
            grid=(T // win,),
            in_specs=[pl.BlockSpec((win, W), lambda i: (i, 0)),
                      pl.BlockSpec((1, win), lambda i: (0, i)),
                      pl.BlockSpec((1, win), lambda i: (0, i))],
            out_specs=[],
            core_axis_name=("core", "subcore"),
            dimension_semantics=(pltpu.PARALLEL,),
        )(x_hbm, i1_hbm, i2_hbm)

    return scatter_kernel(rows, dest1.reshape(1, T), dest2.reshape(1, T))


def _sc_gather_pieces(rows, idx):
    n = idx.shape[0]
    W = rows.shape[1]
    win = SC_WIN

    @pl.kernel(out_type=jax.ShapeDtypeStruct((n, W), rows.dtype), mesh=_sc_mesh(), scratch_types=[])
    def gather_kernel(x_hbm, i_hbm, o_hbm):
        def body(i_vmem, o_vmem):
            pltpu.sync_copy(x_hbm.at[i_vmem.at[0]], o_vmem)

        pltpu.emit_pipeline(
            body,
            grid=(n // win,),
            in_specs=[pl.BlockSpec((1, win), lambda i: (0, i))],
            out_specs=[pl.BlockSpec((win, W), lambda i: (i, 0))],
            core_axis_name=("core", "subcore"),
            dimension_semantics=(pltpu.PARALLEL,),
        )(i_hbm, o_hbm)

    return gather_kernel(rows, idx.reshape(1, n))


def _swiglu(h, wg, wu, wd):
    gate = _dot(h, wg)
    up = _dot(h, wu)
    return _dot((gate * jax.nn.sigmoid(gate) * up).astype(BF16), wd)


def _swiglu_chunked(h, wg_ref, wu_ref, wd_ref, acc=None):
    F = wg_ref.shape[1]
    n_chunks = -(-F // FFN_FC)
    cols = -(-F // (n_chunks * MXU_WIDTH)) * MXU_WIDTH
    for lo in range(0, F, cols):
        sl = slice(lo, min(lo + cols, F))
        y = _swiglu(h, wg_ref[:, sl], wu_ref[:, sl], wd_ref[sl, :])
        acc = y if acc is None else acc + y
    return acc


def _ffn_kernel(x_ref, g_ref, wg_ref, wu_ref, wd_ref, o_ref):
    x = x_ref[...]
    h = _rms(x, g_ref[...]).astype(BF16)
    o_ref[...] = _swiglu_chunked(h, wg_ref, wu_ref, wd_ref, acc=x)


def _ffn(x2, g, wg, wu, wd):
    T, D = x2.shape
    F = wg.shape[1]
    tm = FFN_TM
    return pl.pallas_call(
        _ffn_kernel,
        grid=(T // tm,),
        in_specs=[pl.BlockSpec((tm, D), lambda i: (i, 0)), _const_spec((1, D)),
                  _const_spec((D, F)), _const_spec((D, F)), _const_spec((F, D))],
        out_specs=pl.BlockSpec((tm, D), lambda i: (i, 0)),
        out_shape=jax.ShapeDtypeStruct((T, D), F32),
        compiler_params=pltpu.CompilerParams(
            dimension_semantics=("arbitrary",), vmem_limit_bytes=VMEM_LIMIT),
        name="swiglu",
    )(x2, g.reshape(1, -1), wg, wu, wd)


def _moe_kernel(tidx_ref, texp_ref, nvalid_ref, xs_ref, wg_ref, wu_ref, wd_ref, y_ref):
    del tidx_ref, texp_ref

    @pl.when(pl.program_id(0) < nvalid_ref[0])
    def _():
        h = _unpack_bf16_pairs(_load_tiles(xs_ref)).astype(BF16)
        _store_tiles(y_ref, _pack_bf16_pairs(_swiglu_chunked(h, wg_ref.at[0], wu_ref.at[0], wd_ref.at[0])))


def _moe_experts(xs, tile_idx, tile_expert, n_valid, wg, wu, wd):
    E, D, F = wg.shape
    tm = MOE_TM
    P = xs.shape[0] * SUBLANES
    half = xs.shape[1] * LANES
    row_block = lambda g, ti, te, nv: ti[g]
    w_map = lambda g, ti, te, nv: (te[g], 0, 0)
    return pl.pallas_call(
        _moe_kernel,
        grid_spec=pltpu.PrefetchScalarGridSpec(
            num_scalar_prefetch=3,
            grid=(P // tm,),
            in_specs=[_tiled_spec(tm, half, row_block),
                      pl.BlockSpec((1, D, F), w_map), pl.BlockSpec((1, D, F), w_map),
                      pl.BlockSpec((1, F, D), w_map)],
            out_specs=_tiled_spec(tm, D // 2, row_block)),
        out_shape=_tiled_shape(P, D // 2, U32),
        compiler_params=pltpu.CompilerParams(
            dimension_semantics=("arbitrary",), vmem_limit_bytes=VMEM_LIMIT),
        name="moe_experts",
    )(tile_idx, tile_expert, n_valid, xs, wg, wu, wd)


def _combine_kernel(x_ref, y1_ref, y2_ref, meta_ref, fg_ref, o_ref, *, final_norm):
    meta = meta_ref[...]
    out = x_ref[...] + _meta_col(meta, META_G1) * _unpack_bf16_pairs(_load_tiles(y1_ref.at[0]))
    out = out + _meta_col(meta, META_G2) * _unpack_bf16_pairs(_load_tiles(y2_ref.at[0]))
    if final_norm:
        out = _rms(out, fg_ref[...])
    o_ref[...] = out


def _combine(x2, y12, meta, final_g, *, final_norm):
    T, D = x2.shape
    tm = FFN_TM
    return pl.pallas_call(
        functools.partial(_combine_kernel, final_norm=final_norm),
        grid=(T // tm,),
        in_specs=[pl.BlockSpec((tm, D), lambda i: (i, 0)),
                  _tiled_spec(tm, D // 2, lambda i: i, lead=0), _tiled_spec(tm, D // 2, lambda i: i, lead=1),
                  pl.BlockSpec((tm, LANES), lambda i: (i, 0)), _const_spec((1, D))],
        out_specs=pl.BlockSpec((tm, D), lambda i: (i, 0)),
        out_shape=jax.ShapeDtypeStruct((T, D), F32),
        compiler_params=pltpu.CompilerParams(
            dimension_semantics=("arbitrary",), vmem_limit_bytes=VMEM_LIMIT),
        name="moe_combine",
    )(x2, y12, y12, meta, final_g.reshape(1, -1))


def _moe_layer(x2, routing, wg, wu, wd, final_g, *, final_norm):
    T, D = x2.shape
    tm = MOE_TM
    n_tiles = (T * TOP_K) // tm + N_EXPERTS
    meta, meta_t, hp, counts = routing

    cnt = counts[0, :N_EXPERTS].astype(I32)
    tiles_e = (cnt + tm - 1) // tm
    tile_end = jnp.cumsum(tiles_e)
    row_start = (tile_end - tiles_e) * tm
    experts = jnp.arange(N_EXPERTS, dtype=I32)[:, None]

    def dest(i_row, r_row):
        start = jnp.sum(jnp.where(meta_t[i_row].astype(I32) == experts, row_start[:, None], 0), axis=0)
        return start + meta_t[r_row].astype(I32)

    dest1 = dest(META_I1, META_R1)
    dest2 = dest(META_I2, META_R2)
    n_valid = tile_end[-1:]
    tile_idx = jnp.minimum(jnp.arange(n_tiles, dtype=I32), n_valid - 1)
    tile_expert = jnp.minimum(jnp.sum(tile_end[:, None] <= tile_idx, axis=0), N_EXPERTS - 1).astype(I32)

    xs = _sc_dispatch(hp, dest1, dest2, n_tiles * tm)
    ys = _moe_experts(xs, tile_idx, tile_expert, n_valid.astype(I32), wg, wu, wd)
    y12 = _sc_gather(ys, jnp.stack([dest1, dest2]))
    return _combine(x2, y12, meta, final_g, final_norm=final_norm)


def kernel(x, mem, mix_norm_g, w_in, conv_w, conv_b, conv_ln_g, conv_ln_b, sgu_ln_g, sgu_ln_b, sgu_w, sgu_b,
           pool_w, pool_b, pool_scale, w_out, xattn_norm_g, mem_norm_g, xattn_wq, xattn_wk, xattn_wv, xattn_wo,
           ffn_norm_g, ffn_wg, ffn_wu, ffn_wd, router_w, moe_wg, moe_wu, moe_wd, final_norm_g):
    B, S, D = x.shape
    bf = lambda a: a.astype(BF16)
    for l in range(DEPTH):
        sgu_bias = jnp.repeat(sgu_b[l].T, D_SGU // SGU_HEADS, axis=1)
        pool_wbd = jax.scipy.linalg.block_diag(*[pool_w[l, gi] for gi in range(len(POOL_WINDOWS))])
        x = _mixer(x, mix_norm_g[l], bf(w_in[l]), conv_w[l], conv_b[l], conv_ln_g[l], conv_ln_b[l],
                   sgu_ln_g[l], sgu_ln_b[l], sgu_w[l], sgu_bias, bf(pool_wbd), pool_b[l].reshape(-1),
                   pool_scale[l], bf(w_out[l]))
        k, v = _kv(mem, mem_norm_g[l], bf(xattn_wk[l]), bf(xattn_wv[l]))
        j = l // 2
        attn = (xattn_norm_g[l], bf(xattn_wq[l]), k, v, bf(xattn_wo[l]))
        if l % 2 == 0:
            assert l != DEPTH - 1, "the final RMSNorm is fused into the routed layer's combine kernel"
            x = _xattn(x, *attn)
            x2 = _ffn(x.reshape(B * S, D), ffn_norm_g[l], bf(ffn_wg[j]), bf(ffn_wu[j]), bf(ffn_wd[j]))
        else:
            x, *routing = _xattn(x, *attn, router=(ffn_norm_g[l], router_w[j]))
            x2 = _moe_layer(x.reshape(B * S, D), routing, bf(moe_wg[j]), bf(moe_wu[j]), bf(moe_wd[j]),
                            final_norm_g, final_norm=l == DEPTH - 1)
        x = x2.reshape(B, S, D)
    return x
```

```python
import functools

import jax
import jax.numpy as jnp
from jax import lax
from jax.experimental import pallas as pl
from jax.experimental.pallas import tpu as pltpu
from jax.experimental.pallas import tpu_sc as plsc

F32 = jnp.float32
BF16 = jnp.bfloat16
U32 = jnp.uint32
I32 = jnp.int32

D_MODEL = 1024
DEPTH = 2
CHUNK = 64
D_CONV = 384
CONV_WIDTH = 31
D_SGU = 384
SGU_HEADS = 4
SGU_CHUNK = 128
D_POOL = 256
POOL_WINDOWS = (2, 4, 8, 16)
POOL_GROUP_DIM = D_POOL // len(POOL_WINDOWS)
D_MIX = D_CONV + D_SGU + D_POOL
D_IN = 2 * D_CONV + 2 * D_SGU + D_POOL
X_HEADS = 4
X_HEAD_DIM = D_MODEL // X_HEADS
N_EXPERTS = 8
TOP_K = 2
EPS = 1e-6

LANES = 128
SUBLANES = 8
HIST = 32
MIX_TS = 512
CONV_RB = 64
ATT_TS = 512
FFN_TM = 512
FFN_FC = 1408
MXU_WIDTH = 256
MOE_TM = 512
SC_WIN = 128
HI16 = 0xFFFF0000
VMEM_LIMIT = 56 * 1024 * 1024


def _rms(x, g):
    return x * lax.rsqrt(jnp.mean(x * x, axis=-1, keepdims=True) + EPS) * g


def _layer_norm(x, g, b):
    mu = jnp.mean(x, axis=-1, keepdims=True)
    xc = x - mu
    var = jnp.mean(xc * xc, axis=-1, keepdims=True)
    return xc * lax.rsqrt(var + EPS) * g + b


def _dot(a, b):
    return jnp.dot(a, b, preferred_element_type=F32)


def _mixer_kernel(x_ref, g_ref, win_ref, convw_ref, convb_ref, clng_ref, clnb_ref,
                  slng_ref, slnb_ref, sguw_ref, sgub_ref, poolw_ref, poolb_ref, pscale_ref,
                  wout_ref, o_ref, cbuf, cshift, pb0, pb1, pb2, pb3):
    ts = MIX_TS
    s = pl.program_id(1)

    @pl.when(s == 0)
    def _():
        cbuf[0:HIST, :] = jnp.zeros((HIST, D_CONV), F32)
        pb0[0:HIST, :] = jnp.zeros((HIST, D_POOL), F32)

    x = x_ref[0]
    h = _rms(x, g_ref[...]).astype(BF16)
    z = _dot(h, win_ref[...])

    a_end = 2 * D_CONV
    cbuf[HIST:HIST + ts, :] = z[:, :D_CONV] * jax.nn.sigmoid(z[:, D_CONV:a_end])
    span = ts + HIST - SUBLANES
    for r in range(1, SUBLANES):
        cshift[r - 1, 0:span, :] = cbuf[r:r + span, :]
    ya_blocks = []
    for rb in range(ts // CONV_RB):
        acc = jnp.zeros((CONV_RB, D_CONV), F32)
        for k in range(CONV_WIDTH):
            start = rb * CONV_RB + HIST - (CONV_WIDTH - 1) + k
            r, base = start % SUBLANES, start - start % SUBLANES
            src = cbuf[base:base + CONV_RB, :] if r == 0 else cshift[r - 1, base:base + CONV_RB, :]
            acc = acc + convw_ref[k:k + 1, :] * src
        ya_blocks.append(acc)
    ya = jnp.concatenate(ya_blocks, axis=0) + convb_ref[...]
    ya = _layer_norm(ya, clng_ref[...], clnb_ref[...])
    ya = ya * jax.nn.sigmoid(ya)
    cbuf[0:HIST, :] = cbuf[ts:ts + HIST, :]

    b_end = a_end + 2 * D_SGU
    zb = jax.nn.gelu(z[:, a_end:b_end])
    u = zb[:, :D_SGU]
    v = _layer_norm(zb[:, D_SGU:], slng_ref[...], slnb_ref[...]).astype(BF16)
    blk_r = lax.broadcasted_iota(jnp.int32, (SGU_CHUNK, SGU_CHUNK), 0) // CHUNK
    blk_c = lax.broadcasted_iota(jnp.int32, (SGU_CHUNK, SGU_CHUNK), 1) // CHUNK
    head_of_lane = lax.broadcasted_iota(jnp.int32, (SGU_CHUNK, D_SGU), 1) // (D_SGU // SGU_HEADS)
    w_heads = [jnp.where(blk_r >= blk_c, sguw_ref[hd], 0.0).astype(BF16) for hd in range(SGU_HEADS)]
    s_chunks = []
    for c in range(ts // SGU_CHUNK):
        vc = v[c * SGU_CHUNK:(c + 1) * SGU_CHUNK, :]
        sc = jnp.zeros((SGU_CHUNK, D_SGU), F32)
        for hd in range(SGU_HEADS):
            sc = jnp.where(head_of_lane == hd, _dot(w_heads[hd], vc), sc)
        s_chunks.append(sc + sgub_ref[...])
    yb = u * jnp.concatenate(s_chunks, axis=0)

    cc = z[:, b_end:]
    pb0[HIST:HIST + ts, :] = cc
    pb1[0:ts + 24, :] = pb0[8:ts + 32, :] + pb0[7:ts + 31, :]
    pb2[0:ts + 16, :] = pb1[8:ts + 24, :] + pb1[6:ts + 22, :]
    pb3[0:ts + 8, :] = pb2[8:ts + 16, :] + pb2[4:ts + 12, :]
    s16 = pb3[8:ts + 8, :] + pb3[0:ts, :]
    s8 = pb3[8:ts + 8, :]
    s4 = pb2[16:ts + 16, :]
    s2 = pb1[24:ts + 24, :]
    grp = lax.broadcasted_iota(jnp.int32, (ts, D_POOL), 1) // POOL_GROUP_DIM
    wsum = jnp.where(grp == 0, s2, jnp.where(grp == 1, s4, jnp.where(grp == 2, s8, s16)))
    win = jnp.where(grp == 0, 2, jnp.where(grp == 1, 4, jnp.where(grp == 2, 8, 16)))
    pos = s * ts + lax.broadcasted_iota(jnp.int32, (ts, D_POOL), 0)
    cnt = jnp.minimum(pos + 1, win).astype(F32)
    p = (wsum / cnt - cc).astype(BF16)
    yc = (_dot(p, poolw_ref[...]) + poolb_ref[...]) * pscale_ref[...]
    pb0[0:HIST, :] = pb0[ts:ts + HIST, :]

    mixed = jnp.concatenate([ya.astype(BF16), yb.astype(BF16), yc.astype(BF16)], axis=-1)
    o_ref[0] = x + _dot(mixed, wout_ref[...])


def _const_spec(shape):
    zeros = (0,) * len(shape)
    return pl.BlockSpec(shape, lambda *_: zeros)


def _mixer(x, g, w_in, conv_w, conv_b, cln_g, cln_b, sln_g, sln_b, sgu_w, sgu_bias, pool_wbd, pool_b,
           pool_scale, w_out):
    B, S, D = x.shape
    ts = MIX_TS
    row = lambda a: a.reshape(1, -1)
    args = (x, row(g), w_in, conv_w, row(conv_b), row(cln_g), row(cln_b), row(sln_g), row(sln_b),
            sgu_w, sgu_bias, pool_wbd, row(pool_b), row(pool_scale), w_out)
    in_specs = [pl.BlockSpec((1, ts, D), lambda b, s: (b, s, 0))]
    in_specs += [_const_spec(a.shape) for a in args[1:]]
    return pl.pallas_call(
        _mixer_kernel,
        grid=(B, S // ts),
        in_specs=in_specs,
        out_specs=pl.BlockSpec((1, ts, D), lambda b, s: (b, s, 0)),
        out_shape=jax.ShapeDtypeStruct((B, S, D), F32),
        scratch_shapes=[pltpu.VMEM((ts + HIST, D_CONV), F32),
                        pltpu.VMEM((SUBLANES - 1, ts + HIST - SUBLANES, D_CONV), F32)]
                       + [pltpu.VMEM((ts + HIST, D_POOL), F32)] * 4,
        compiler_params=pltpu.CompilerParams(
            dimension_semantics=("arbitrary", "arbitrary"), vmem_limit_bytes=VMEM_LIMIT),
        name="mixer",
    )(*args)


def _kv_kernel(mem_ref, g_ref, wk_ref, wv_ref, k_ref, v_ref):
    m = _rms(mem_ref[0], g_ref[...]).astype(BF16)
    k_ref[0] = _dot(m, wk_ref[...]).astype(BF16)
    v_ref[0] = _dot(m, wv_ref[...]).astype(BF16)


def _kv(mem, g, wk, wv):
    B, M, D = mem.shape
    return pl.pallas_call(
        _kv_kernel,
        grid=(B,),
        in_specs=[pl.BlockSpec((1, M, D), lambda b: (b, 0, 0)), _const_spec((1, D)),
                  _const_spec((D, D)), _const_spec((D, D))],
        out_specs=[pl.BlockSpec((1, M, D), lambda b: (b, 0, 0))] * 2,
        out_shape=[jax.ShapeDtypeStruct((B, M, D), BF16)] * 2,
        compiler_params=pltpu.CompilerParams(
            dimension_semantics=("arbitrary",), vmem_limit_bytes=VMEM_LIMIT),
        name="mem_kv",
    )(mem, g.reshape(1, -1), wk, wv)


def _xattn_kernel(x_ref, g_ref, wq_ref, k_ref, v_ref, wo_ref, o_ref):
    x = x_ref[0]
    h = _rms(x, g_ref[...]).astype(BF16)
    q = _dot(h, wq_ref[...]).astype(BF16)
    heads = []
    for hd in range(X_HEADS):
        sl = slice(hd * X_HEAD_DIM, (hd + 1) * X_HEAD_DIM)
        sc = lax.dot_general(q[:, sl], k_ref[0, :, sl], (((1,), (1,)), ((), ())),
                             preferred_element_type=F32) * (X_HEAD_DIM ** -0.5)
        e = jnp.exp(sc - jnp.max(sc, axis=-1, keepdims=True))
        heads.append(_dot(e.astype(BF16), v_ref[0, :, sl]) / jnp.sum(e, axis=-1, keepdims=True))
    o = jnp.concatenate(heads, axis=-1).astype(BF16)
    o_ref[0] = x + _dot(o, wo_ref[...])


def _xattn(x, g, wq, k, v, wo):
    B, S, D = x.shape
    M = k.shape[1]
    ts = ATT_TS
    return pl.pallas_call(
        _xattn_kernel,
        grid=(B, S // ts),
        in_specs=[pl.BlockSpec((1, ts, D), lambda b, s: (b, s, 0)), _const_spec((1, D)), _const_spec((D, D)),
                  pl.BlockSpec((1, M, D), lambda b, s: (b, 0, 0)), pl.BlockSpec((1, M, D), lambda b, s: (b, 0, 0)),
                  _const_spec((D, D))],
        out_specs=pl.BlockSpec((1, ts, D), lambda b, s: (b, s, 0)),
        out_shape=jax.ShapeDtypeStruct((B, S, D), F32),
        compiler_params=pltpu.CompilerParams(
            dimension_semantics=("arbitrary", "arbitrary"), vmem_limit_bytes=VMEM_LIMIT),
        name="xattn",
    )(x, g.reshape(1, -1), wq, k, v, wo)


def _store_tiles(ref, val):
    rows, width = val.shape
    for j in range(width // LANES):
        ref[:, j, :, :] = val[:, j * LANES:(j + 1) * LANES].reshape(rows // SUBLANES, SUBLANES, LANES)


def _load_tiles(ref):
    nb, k, _, _ = ref.shape
    return jnp.concatenate([ref[:, j, :, :].reshape(nb * SUBLANES, LANES) for j in range(k)], axis=1)


def _pack_bf16_pairs(v):
    bits = lax.bitcast_convert_type(v.astype(BF16).astype(F32), U32)
    half = bits.shape[1] // 2
    return (bits[:, :half] & jnp.uint32(HI16)) | (bits[:, half:] >> 16)


def _unpack_bf16_pairs(w):
    hi = lax.bitcast_convert_type(w & jnp.uint32(HI16), F32)
    lo = lax.bitcast_convert_type(w << 16, F32)
    return jnp.concatenate([hi, lo], axis=1)


def _tiled_shape(rows, width, dtype):
    return jax.ShapeDtypeStruct((rows // SUBLANES, width // LANES, SUBLANES, LANES), dtype)


def _tiled_spec(rows, width, row_block, lead=None):
    block = (rows // SUBLANES, width // LANES, SUBLANES, LANES)
    if lead is None:
        return pl.BlockSpec(block, lambda *a: (row_block(*a), 0, 0, 0))
    return pl.BlockSpec((1,) + block, lambda *a: (lead, row_block(*a), 0, 0, 0))


def _piece_index(rows, k):
    rows = rows[..., None]
    return ((rows // SUBLANES) * k + jnp.arange(k, dtype=I32)) * SUBLANES + rows % SUBLANES


def _tile_order(piece_of_row):
    *lead, T, k = piece_of_row.shape
    x = piece_of_row.reshape(*lead, T // SUBLANES, SUBLANES, k)
    return jnp.swapaxes(x, -1, -2).reshape(*lead, T * k)


META_I1, META_I2, META_G1, META_G2, META_R1, META_R2 = range(6)


def _meta_col(meta, col):
    lane = lax.broadcasted_iota(I32, meta.shape, 1)
    return jnp.sum(jnp.where(lane == col, meta, 0.0), axis=-1, keepdims=True)


def _router_kernel(x_ref, g_ref, rw_ref, tril_ref, meta_ref, metat_ref, hp_ref, cnt_ref, carry):
    @pl.when(pl.program_id(0) == 0)
    def _():
        carry[...] = jnp.zeros_like(carry)

    h = _rms(x_ref[...], g_ref[...])
    h_hi = h.astype(BF16)
    h_lo = (h - h_hi.astype(F32)).astype(BF16)
    logits = _dot(jnp.concatenate([h_hi, h_hi, h_lo], axis=1), rw_ref[...])
    lane = lax.broadcasted_iota(I32, logits.shape, 1)
    neg = jnp.float32(-jnp.inf)
    lg = jnp.where(lane < N_EXPERTS, logits, neg)
    m1 = jnp.max(lg, axis=-1, keepdims=True)
    i1 = jnp.min(jnp.where(lg == m1, lane, LANES), axis=-1, keepdims=True)
    lg2 = jnp.where(lane == i1, neg, lg)
    m2 = jnp.max(lg2, axis=-1, keepdims=True)
    i2 = jnp.min(jnp.where(lg2 == m2, lane, LANES), axis=-1, keepdims=True)
    d = jnp.exp(m2 - m1)
    g1 = 1.0 / (1.0 + d)
    g2 = d / (1.0 + d)

    sel1 = lane == i1
    sel2 = lane == i2
    onehot = jnp.where(jnp.logical_or(sel1, sel2), 1.0, 0.0)
    before = _dot(tril_ref[...], onehot.astype(BF16)) + carry[...]
    r1 = jnp.sum(jnp.where(sel1, before, 0.0), axis=-1, keepdims=True)
    r2 = jnp.sum(jnp.where(sel2, before, 0.0), axis=-1, keepdims=True)
    carry[...] += jnp.sum(onehot, axis=0, keepdims=True)
    cnt_ref[...] = carry[...]

    meta = jnp.zeros_like(logits)
    for c, val in ((META_I1, i1.astype(F32)), (META_I2, i2.astype(F32)), (META_G1, g1), (META_G2, g2),
                   (META_R1, r1), (META_R2, r2)):
        meta = jnp.where(lane == c, val, meta)
    meta_ref[...] = meta

    _store_tiles(hp_ref, _pack_bf16_pairs(h))
    metat_ref[...] = meta.T[:SUBLANES, :]


def _router(x2, g, router_w):
    T, D = x2.shape
    tm = FFN_TM
    rw = jnp.zeros((D, LANES), F32).at[:, :N_EXPERTS].set(router_w)
    rw_hi = rw.astype(BF16)
    rw_lo = (rw - rw_hi.astype(F32)).astype(BF16)
    rw_split = jnp.concatenate([rw_hi, rw_lo, rw_hi], axis=0)
    tril = jnp.tril(jnp.ones((tm, tm), BF16), -1)
    return pl.pallas_call(
        _router_kernel,
        grid=(T // tm,),
        in_specs=[pl.BlockSpec((tm, D), lambda i: (i, 0)), _const_spec((1, D)), _const_spec((3 * D, LANES)),
                  _const_spec((tm, tm))],
        out_specs=[pl.BlockSpec((tm, LANES), lambda i: (i, 0)), pl.BlockSpec((SUBLANES, tm), lambda i: (0, i)),
                   _tiled_spec(tm, D // 2, lambda i: i), _const_spec((1, LANES))],
        out_shape=[jax.ShapeDtypeStruct((T, LANES), F32), jax.ShapeDtypeStruct((SUBLANES, T), F32),
                   _tiled_shape(T, D // 2, U32), jax.ShapeDtypeStruct((1, LANES), F32)],
        scratch_shapes=[pltpu.VMEM((1, LANES), F32)],
        compiler_params=pltpu.CompilerParams(
            dimension_semantics=("arbitrary",), vmem_limit_bytes=VMEM_LIMIT),
        name="router",
    )(x2, g.reshape(1, -1), rw_split, tril)


def _sc_mesh():
    return plsc.VectorSubcoreMesh(core_axis_name="core", subcore_axis_name="subcore")


def _sc_dispatch(tiled, dest1, dest2, n_out):
    nb, k, _, _ = tiled.shape
    pieces = tiled.reshape(nb * k * SUBLANES, LANES)
    d1, d2 = (_tile_order(_piece_index(d, k)) for d in (dest1, dest2))
    out = _sc_scatter_pieces(pieces, d1, d2, n_out * k)
    return out.reshape(n_out // SUBLANES, k, SUBLANES, LANES)


def _sc_gather(tiled, idx):
    nb, k, _, _ = tiled.shape
    G, n = idx.shape
    pieces = tiled.reshape(nb * k * SUBLANES, LANES)
    out = _sc_gather_pieces(pieces, _tile_order(_piece_index(idx, k)).reshape(-1))
    return out.reshape(G, n // SUBLANES, k, SUBLANES, LANES)


def _sc_scatter_pieces(rows, dest1, dest2, n_out):
    T, W = rows.shape
    win = SC_WIN

    @pl.kernel(out_type=jax.ShapeDtypeStruct((n_out, W), rows.dtype), mesh=_sc_mesh(), scratch_types=[])
    def scatter_kernel(x_hbm, i1_hbm, i2_hbm, o_hbm):
        def body(x_vmem, i1_vmem, i2_vmem):
            pltpu.sync_copy(x_vmem, o_hbm.at[i1_vmem.at[0]])
            pltpu.sync_copy(x_vmem, o_hbm.at[i2_vmem.at[0]])

        pltpu.emit_pipeline(
            body,
            grid=(T // win,),
            in_specs=[pl.BlockSpec((win, W), lambda i: (i, 0)),
                      pl.BlockSpec((1, win), lambda i: (0, i)),
                      pl.BlockSpec((1, win), lambda i: (0, i))],
            out_specs=[],
            core_axis_name=("core", "subcore"),
            dimension_semantics=(pltpu.PARALLEL,),
        )(x_hbm, i1_hbm, i2_hbm)

    return scatter_kernel(rows, dest1.reshape(1, T), dest2.reshape(1, T))


def _sc_gather_pieces(rows, idx):
    n = idx.shape[0]
    W = rows.shape[1]
    win = SC_WIN

    @pl.kernel(out_type=jax.ShapeDtypeStruct((n, W), rows.dtype), mesh=_sc_mesh(), scratch_types=[])
    def gather_kernel(x_hbm, i_hbm, o_hbm):
        def body(i_vmem, o_vmem):
            pltpu.sync_copy(x_hbm.at[i_vmem.at[0]], o_vmem)

        pltpu.emit_pipeline(
            body,
            grid=(n // win,),
            in_specs=[pl.BlockSpec((1, win), lambda i: (0, i))],
            out_specs=[pl.BlockSpec((win, W), lambda i: (i, 0))],
            core_axis_name=("core", "subcore"),
            dimension_semantics=(pltpu.PARALLEL,),
        )(i_hbm, o_hbm)

    return gather_kernel(rows, idx.reshape(1, n))


def _swiglu(h, wg, wu, wd):
    gate = _dot(h, wg)
    up = _dot(h, wu)
    return _dot((gate * jax.nn.sigmoid(gate) * up).astype(BF16), wd)


def _swiglu_chunked(h, wg_ref, wu_ref, wd_ref, acc=None):
    F = wg_ref.shape[1]
    n_chunks = -(-F // FFN_FC)
    cols = -(-F // (n_chunks * MXU_WIDTH)) * MXU_WIDTH
    for lo in range(0, F, cols):
        sl = slice(lo, min(lo + cols, F))
        y = _swiglu(h, wg_ref[:, sl], wu_ref[:, sl], wd_ref[sl, :])
        acc = y if acc is None else acc + y
    return acc


def _ffn_kernel(x_ref, g_ref, wg_ref, wu_ref, wd_ref, o_ref):
    x = x_ref[...]
    h = _rms(x, g_ref[...]).astype(BF16)
    o_ref[...] = _swiglu_chunked(h, wg_ref, wu_ref, wd_ref, acc=x)


def _ffn(x2, g, wg, wu, wd):
    T, D = x2.shape
    F = wg.shape[1]
    tm = FFN_TM
    return pl.pallas_call(
        _ffn_kernel,
        grid=(T // tm,),
        in_specs=[pl.BlockSpec((tm, D), lambda i: (i, 0)), _const_spec((1, D)),
                  _const_spec((D, F)), _const_spec((D, F)), _const_spec((F, D))],
        out_specs=pl.BlockSpec((tm, D), lambda i: (i, 0)),
        out_shape=jax.ShapeDtypeStruct((T, D), F32),
        compiler_params=pltpu.CompilerParams(
            dimension_semantics=("arbitrary",), vmem_limit_bytes=VMEM_LIMIT),
        name="swiglu",
    )(x2, g.reshape(1, -1), wg, wu, wd)


def _moe_kernel(tidx_ref, texp_ref, nvalid_ref, xs_ref, wg_ref, wu_ref, wd_ref, y_ref):
    del tidx_ref, texp_ref

    @pl.when(pl.program_id(0) < nvalid_ref[0])
    def _():
        h = _unpack_bf16_pairs(_load_tiles(xs_ref)).astype(BF16)
        _store_tiles(y_ref, _pack_bf16_pairs(_swiglu_chunked(h, wg_ref.at[0], wu_ref.at[0], wd_ref.at[0])))


def _moe_experts(xs, tile_idx, tile_expert, n_valid, wg, wu, wd):
    E, D, F = wg.shape
    tm = MOE_TM
    P = xs.shape[0] * SUBLANES
    half = xs.shape[1] * LANES
    row_block = lambda g, ti, te, nv: ti[g]
    w_map = lambda g, ti, te, nv: (te[g], 0, 0)
    return pl.pallas_call(
        _moe_kernel,
        grid_spec=pltpu.PrefetchScalarGridSpec(
            num_scalar_prefetch=3,
            grid=(P // tm,),
            in_specs=[_tiled_spec(tm, half, row_block),
                      pl.BlockSpec((1, D, F), w_map), pl.BlockSpec((1, D, F), w_map),
                      pl.BlockSpec((1, F, D), w_map)],
            out_specs=_tiled_spec(tm, D // 2, row_block)),
        out_shape=_tiled_shape(P, D // 2, U32),
        compiler_params=pltpu.CompilerParams(
            dimension_semantics=("arbitrary",), vmem_limit_bytes=VMEM_LIMIT),
        name="moe_experts",
    )(tile_idx, tile_expert, n_valid, xs, wg, wu, wd)


def _combine_kernel(x_ref, y1_ref, y2_ref, meta_ref, fg_ref, o_ref, *, final_norm):
    meta = meta_ref[...]
    out = x_ref[...] + _meta_col(meta, META_G1) * _unpack_bf16_pairs(_load_tiles(y1_ref.at[0]))
    out = out + _meta_col(meta, META_G2) * _unpack_bf16_pairs(_load_tiles(y2_ref.at[0]))
    if final_norm:
        out = _rms(out, fg_ref[...])
    o_ref[...] = out


def _combine(x2, y12, meta, final_g, *, final_norm):
    T, D = x2.shape
    tm = FFN_TM
    return pl.pallas_call(
        functools.partial(_combine_kernel, final_norm=final_norm),
        grid=(T // tm,),
        in_specs=[pl.BlockSpec((tm, D), lambda i: (i, 0)),
                  _tiled_spec(tm, D // 2, lambda i: i, lead=0), _tiled_spec(tm, D // 2, lambda i: i, lead=1),
                  pl.BlockSpec((tm, LANES), lambda i: (i, 0)), _const_spec((1, D))],
        out_specs=pl.BlockSpec((tm, D), lambda i: (i, 0)),
        out_shape=jax.ShapeDtypeStruct((T, D), F32),
        compiler_params=pltpu.CompilerParams(
            dimension_semantics=("arbitrary",), vmem_limit_bytes=VMEM_LIMIT),
        name="moe_combine",
    )(x2, y12, y12, meta, final_g.reshape(1, -1))


def _moe_layer(x2, g, router_w, wg, wu, wd, final_g, *, final_norm):
    T, D = x2.shape
    tm = MOE_TM
    n_tiles = (T * TOP_K) // tm + N_EXPERTS
    meta, meta_t, hp, counts = _router(x2, g, router_w)

    cnt = counts[0, :N_EXPERTS].astype(I32)
    tiles_e = (cnt + tm - 1) // tm
    tile_end = jnp.cumsum(tiles_e)
    row_start = (tile_end - tiles_e) * tm
    experts = jnp.arange(N_EXPERTS, dtype=I32)[:, None]

    def dest(i_row, r_row):
        start = jnp.sum(jnp.where(meta_t[i_row].astype(I32) == experts, row_start[:, None], 0), axis=0)
        return start + meta_t[r_row].astype(I32)

    dest1 = dest(META_I1, META_R1)
    dest2 = dest(META_I2, META_R2)
    n_valid = tile_end[-1:]
    tile_idx = jnp.minimum(jnp.arange(n_tiles, dtype=I32), n_valid - 1)
    tile_expert = jnp.minimum(jnp.sum(tile_end[:, None] <= tile_idx, axis=0), N_EXPERTS - 1).astype(I32)

    xs = _sc_dispatch(hp, dest1, dest2, n_tiles * tm)
    ys = _moe_experts(xs, tile_idx, tile_expert, n_valid.astype(I32), wg, wu, wd)
    y12 = _sc_gather(ys, jnp.stack([dest1, dest2]))
    return _combine(x2, y12, meta, final_g, final_norm=final_norm)


def kernel(x, mem, mix_norm_g, w_in, conv_w, conv_b, conv_ln_g, conv_ln_b, sgu_ln_g, sgu_ln_b, sgu_w, sgu_b,
           pool_w, pool_b, pool_scale, w_out, xattn_norm_g, mem_norm_g, xattn_wq, xattn_wk, xattn_wv, xattn_wo,
           ffn_norm_g, ffn_wg, ffn_wu, ffn_wd, router_w, moe_wg, moe_wu, moe_wd, final_norm_g):
    B, S, D = x.shape
    bf = lambda a: a.astype(BF16)
    for l in range(DEPTH):
        sgu_bias = jnp.repeat(sgu_b[l].T, D_SGU // SGU_HEADS, axis=1)
        pool_wbd = jax.scipy.linalg.block_diag(*[pool_w[l, gi] for gi in range(len(POOL_WINDOWS))])
        x = _mixer(x, mix_norm_g[l], bf(w_in[l]), conv_w[l], conv_b[l], conv_ln_g[l], conv_ln_b[l],
                   sgu_ln_g[l], sgu_ln_b[l], sgu_w[l], sgu_bias, bf(pool_wbd), pool_b[l].reshape(-1),
                   pool_scale[l], bf(w_out[l]))
        k, v = _kv(mem, mem_norm_g[l], bf(xattn_wk[l]), bf(xattn_wv[l]))
        x = _xattn(x, xattn_norm_g[l], bf(xattn_wq[l]), k, v, bf(xattn_wo[l]))
        x2 = x.reshape(B * S, D)
        j = l // 2
        if l % 2 == 0:
            assert l != DEPTH - 1, "the final RMSNorm is fused into the routed layer's combine kernel"
            x2 = _ffn(x2, ffn_norm_g[l], bf(ffn_wg[j]), bf(ffn_wu[j]), bf(ffn_wd[j]))
        else:
            x2 = _moe_layer(x2, ffn_norm_g[l], router_w[j], bf(moe_wg[j]), bf(moe_wu[j]), bf(moe_wd[j]),
                            final_norm_g, final_norm=l == DEPTH - 1)
        x = x2.reshape(B, S, D)
    return x
```

```python
import functools

import jax
import jax.numpy as jnp
from jax import lax
from jax.experimental import pallas as pl
from jax.experimental.pallas import tpu as pltpu
from jax.experimental.pallas import tpu_sc as plsc

F32 = jnp.float32
BF16 = jnp.bfloat16
U32 = jnp.uint32
I32 = jnp.int32

D_MODEL = 1024
DEPTH = 2
CHUNK = 64
D_CONV = 384
CONV_WIDTH = 31
D_SGU = 384
SGU_HEADS = 4
SGU_CHUNK = 128
D_POOL = 256
POOL_WINDOWS = (2, 4, 8, 16)
POOL_GROUP_DIM = D_POOL // len(POOL_WINDOWS)
D_MIX = D_CONV + D_SGU + D_POOL
D_IN = 2 * D_CONV + 2 * D_SGU + D_POOL
X_HEADS = 4
X_HEAD_DIM = D_MODEL // X_HEADS
N_EXPERTS = 8
TOP_K = 2
EPS = 1e-6

LANES = 128
SUBLANES = 8
HIST = 32
MIX_TS = 512
CONV_RB = 64
ATT_TS = 512
FFN_TM = 512
FFN_FC = 1408
MXU_WIDTH = 256
MOE_TM = 512
SC_WIN = 128
HI16 = 0xFFFF0000
VMEM_LIMIT = 56 * 1024 * 1024


def _rms(x, g):
    return x * lax.rsqrt(jnp.mean(x * x, axis=-1, keepdims=True) + EPS) * g


def _layer_norm(x, g, b):
    mu = jnp.mean(x, axis=-1, keepdims=True)
    xc = x - mu
    var = jnp.mean(xc * xc, axis=-1, keepdims=True)
    return xc * lax.rsqrt(var + EPS) * g + b


def _dot(a, b):
    return jnp.dot(a, b, preferred_element_type=F32)


def _mixer_kernel(x_ref, g_ref, win_ref, convw_ref, convb_ref, clng_ref, clnb_ref,
                  slng_ref, slnb_ref, sguw_ref, sgub_ref, poolw_ref, poolb_ref, pscale_ref,
                  wout_ref, o_ref, cbuf, cshift, pb0, pb1, pb2, pb3):
    ts = MIX_TS
    s = pl.program_id(1)

    @pl.when(s == 0)
    def _():
        cbuf[0:HIST, :] = jnp.zeros((HIST, D_CONV), F32)
        pb0[0:HIST, :] = jnp.zeros((HIST, D_POOL), F32)

    x = x_ref[0]
    h = _rms(x, g_ref[...]).astype(BF16)
    z = _dot(h, win_ref[...])

    a_end = 2 * D_CONV
    cbuf[HIST:HIST + ts, :] = z[:, :D_CONV] * jax.nn.sigmoid(z[:, D_CONV:a_end])
    span = ts + HIST - SUBLANES
    for r in range(1, SUBLANES):
        cshift[r - 1, 0:span, :] = cbuf[r:r + span, :]
    ya_blocks = []
    for rb in range(ts // CONV_RB):
        acc = jnp.zeros((CONV_RB, D_CONV), F32)
        for k in range(CONV_WIDTH):
            start = rb * CONV_RB + HIST - (CONV_WIDTH - 1) + k
            r, base = start % SUBLANES, start - start % SUBLANES
            src = cbuf[base:base + CONV_RB, :] if r == 0 else cshift[r - 1, base:base + CONV_RB, :]
            acc = acc + convw_ref[k:k + 1, :] * src
        ya_blocks.append(acc)
    ya = jnp.concatenate(ya_blocks, axis=0) + convb_ref[...]
    ya = _layer_norm(ya, clng_ref[...], clnb_ref[...])
    ya = ya * jax.nn.sigmoid(ya)
    cbuf[0:HIST, :] = cbuf[ts:ts + HIST, :]

    b_end = a_end + 2 * D_SGU
    zb = jax.nn.gelu(z[:, a_end:b_end])
    u = zb[:, :D_SGU]
    v = _layer_norm(zb[:, D_SGU:], slng_ref[...], slnb_ref[...]).astype(BF16)
    blk_r = lax.broadcasted_iota(jnp.int32, (SGU_CHUNK, SGU_CHUNK), 0) // CHUNK
    blk_c = lax.broadcasted_iota(jnp.int32, (SGU_CHUNK, SGU_CHUNK), 1) // CHUNK
    head_of_lane = lax.broadcasted_iota(jnp.int32, (SGU_CHUNK, D_SGU), 1) // (D_SGU // SGU_HEADS)
    w_heads = [jnp.where(blk_r >= blk_c, sguw_ref[hd], 0.0).astype(BF16) for hd in range(SGU_HEADS)]
    s_chunks = []
    for c in range(ts // SGU_CHUNK):
        vc = v[c * SGU_CHUNK:(c + 1) * SGU_CHUNK, :]
        sc = jnp.zeros((SGU_CHUNK, D_SGU), F32)
        for hd in range(SGU_HEADS):
            sc = jnp.where(head_of_lane == hd, _dot(w_heads[hd], vc), sc)
        s_chunks.append(sc + sgub_ref[...])
    yb = u * jnp.concatenate(s_chunks, axis=0)

    cc = z[:, b_end:]
    pb0[HIST:HIST + ts, :] = cc
    pb1[0:ts + 24, :] = pb0[8:ts + 32, :] + pb0[7:ts + 31, :]
    pb2[0:ts + 16, :] = pb1[8:ts + 24, :] + pb1[6:ts + 22, :]
    pb3[0:ts + 8, :] = pb2[8:ts + 16, :] + pb2[4:ts + 12, :]
    s16 = pb3[8:ts + 8, :] + pb3[0:ts, :]
    s8 = pb3[8:ts + 8, :]
    s4 = pb2[16:ts + 16, :]
    s2 = pb1[24:ts + 24, :]
    grp = lax.broadcasted_iota(jnp.int32, (ts, D_POOL), 1) // POOL_GROUP_DIM
    wsum = jnp.where(grp == 0, s2, jnp.where(grp == 1, s4, jnp.where(grp == 2, s8, s16)))
    win = jnp.where(grp == 0, 2, jnp.where(grp == 1, 4, jnp.where(grp == 2, 8, 16)))
    pos = s * ts + lax.broadcasted_iota(jnp.int32, (ts, D_POOL), 0)
    cnt = jnp.minimum(pos + 1, win).astype(F32)
    p = (wsum / cnt - cc).astype(BF16)
    yc = (_dot(p, poolw_ref[...]) + poolb_ref[...]) * pscale_ref[...]
    pb0[0:HIST, :] = pb0[ts:ts + HIST, :]

    mixed = jnp.concatenate([ya.astype(BF16), yb.astype(BF16), yc.astype(BF16)], axis=-1)
    o_ref[0] = x + _dot(mixed, wout_ref[...])


def _const_spec(shape):
    zeros = (0,) * len(shape)
    return pl.BlockSpec(shape, lambda *_: zeros)


def _mixer(x, g, w_in, conv_w, conv_b, cln_g, cln_b, sln_g, sln_b, sgu_w, sgu_bias, pool_wbd, pool_b,
           pool_scale, w_out):
    B, S, D = x.shape
    ts = MIX_TS
    row = lambda a: a.reshape(1, -1)
    args = (x, row(g), w_in, conv_w, row(conv_b), row(cln_g), row(cln_b), row(sln_g), row(sln_b),
            sgu_w, sgu_bias, pool_wbd, row(pool_b), row(pool_scale), w_out)
    in_specs = [pl.BlockSpec((1, ts, D), lambda b, s: (b, s, 0))]
    in_specs += [_const_spec(a.shape) for a in args[1:]]
    return pl.pallas_call(
        _mixer_kernel,
        grid=(B, S // ts),
        in_specs=in_specs,
        out_specs=pl.BlockSpec((1, ts, D), lambda b, s: (b, s, 0)),
        out_shape=jax.ShapeDtypeStruct((B, S, D), F32),
        scratch_shapes=[pltpu.VMEM((ts + HIST, D_CONV), F32),
                        pltpu.VMEM((SUBLANES - 1, ts + HIST - SUBLANES, D_CONV), F32)]
                       + [pltpu.VMEM((ts + HIST, D_POOL), F32)] * 4,
        compiler_params=pltpu.CompilerParams(
            dimension_semantics=("arbitrary", "arbitrary"), vmem_limit_bytes=VMEM_LIMIT),
        name="mixer",
    )(*args)


def _kv_kernel(mem_ref, g_ref, wk_ref, wv_ref, k_ref, v_ref):
    m = _rms(mem_ref[0], g_ref[...]).astype(BF16)
    k_ref[0] = _dot(m, wk_ref[...]).astype(BF16)
    v_ref[0] = _dot(m, wv_ref[...]).astype(BF16)


def _kv(mem, g, wk, wv):
    B, M, D = mem.shape
    return pl.pallas_call(
        _kv_kernel,
        grid=(B,),
        in_specs=[pl.BlockSpec((1, M, D), lambda b: (b, 0, 0)), _const_spec((1, D)),
                  _const_spec((D, D)), _const_spec((D, D))],
        out_specs=[pl.BlockSpec((1, M, D), lambda b: (b, 0, 0))] * 2,
        out_shape=[jax.ShapeDtypeStruct((B, M, D), BF16)] * 2,
        compiler_params=pltpu.CompilerParams(
            dimension_semantics=("arbitrary",), vmem_limit_bytes=VMEM_LIMIT),
        name="mem_kv",
    )(mem, g.reshape(1, -1), wk, wv)


def _xattn_kernel(x_ref, g_ref, wq_ref, k_ref, v_ref, wo_ref, o_ref):
    x = x_ref[0]
    h = _rms(x, g_ref[...]).astype(BF16)
    q = _dot(h, wq_ref[...]).astype(BF16)
    heads = []
    for hd in range(X_HEADS):
        sl = slice(hd * X_HEAD_DIM, (hd + 1) * X_HEAD_DIM)
        sc = lax.dot_general(q[:, sl], k_ref[0, :, sl], (((1,), (1,)), ((), ())),
                             preferred_element_type=F32) * (X_HEAD_DIM ** -0.5)
        e = jnp.exp(sc - jnp.max(sc, axis=-1, keepdims=True))
        heads.append(_dot(e.astype(BF16), v_ref[0, :, sl]) / jnp.sum(e, axis=-1, keepdims=True))
    o = jnp.concatenate(heads, axis=-1).astype(BF16)
    o_ref[0] = x + _dot(o, wo_ref[...])


def _xattn(x, g, wq, k, v, wo):
    B, S, D = x.shape
    M = k.shape[1]
    ts = ATT_TS
    return pl.pallas_call(
        _xattn_kernel,
        grid=(B, S // ts),
        in_specs=[pl.BlockSpec((1, ts, D), lambda b, s: (b, s, 0)), _const_spec((1, D)), _const_spec((D, D)),
                  pl.BlockSpec((1, M, D), lambda b, s: (b, 0, 0)), pl.BlockSpec((1, M, D), lambda b, s: (b, 0, 0)),
                  _const_spec((D, D))],
        out_specs=pl.BlockSpec((1, ts, D), lambda b, s: (b, s, 0)),
        out_shape=jax.ShapeDtypeStruct((B, S, D), F32),
        compiler_params=pltpu.CompilerParams(
            dimension_semantics=("arbitrary", "arbitrary"), vmem_limit_bytes=VMEM_LIMIT),
        name="xattn",
    )(x, g.reshape(1, -1), wq, k, v, wo)


def _store_tiles(ref, val):
    for j in range(ref.shape[0]):
        ref[j] = val[:, j * LANES:(j + 1) * LANES]


def _load_tiles(ref):
    return jnp.concatenate([ref[j] for j in range(ref.shape[0])], axis=1)


def _pack_bf16_pairs(v):
    bits = lax.bitcast_convert_type(v.astype(BF16).astype(F32), U32)
    half = bits.shape[1] // 2
    return (bits[:, :half] & jnp.uint32(HI16)) | (bits[:, half:] >> 16)


def _unpack_bf16_pairs(w):
    hi = lax.bitcast_convert_type(w & jnp.uint32(HI16), F32)
    lo = lax.bitcast_convert_type(w << 16, F32)
    return jnp.concatenate([hi, lo], axis=1)


def _tiled_shape(rows, width, dtype):
    return jax.ShapeDtypeStruct((width // LANES, rows, LANES), dtype)


def _tiled_spec(rows, width, row_block, lead=None):
    block = (width // LANES, rows, LANES)
    if lead is None:
        return pl.BlockSpec(block, lambda *a: (0, row_block(*a), 0))
    return pl.BlockSpec((1,) + block, lambda *a: (lead, 0, row_block(*a), 0))


def _piece_index(rows, k, n_rows):
    return jnp.arange(k, dtype=I32)[:, None] * n_rows + rows[..., None, :]


META_I1, META_I2, META_G1, META_G2, META_R1, META_R2 = range(6)


def _meta_col(meta, col):
    lane = lax.broadcasted_iota(I32, meta.shape, 1)
    return jnp.sum(jnp.where(lane == col, meta, 0.0), axis=-1, keepdims=True)


def _router_kernel(x_ref, g_ref, rw_ref, tril_ref, meta_ref, metat_ref, hp_ref, cnt_ref, carry):
    @pl.when(pl.program_id(0) == 0)
    def _():
        carry[...] = jnp.zeros_like(carry)

    h = _rms(x_ref[...], g_ref[...])
    h_hi = h.astype(BF16)
    h_lo = (h - h_hi.astype(F32)).astype(BF16)
    logits = _dot(jnp.concatenate([h_hi, h_hi, h_lo], axis=1), rw_ref[...])
    lane = lax.broadcasted_iota(I32, logits.shape, 1)
    neg = jnp.float32(-jnp.inf)
    lg = jnp.where(lane < N_EXPERTS, logits, neg)
    m1 = jnp.max(lg, axis=-1, keepdims=True)
    i1 = jnp.min(jnp.where(lg == m1, lane, LANES), axis=-1, keepdims=True)
    lg2 = jnp.where(lane == i1, neg, lg)
    m2 = jnp.max(lg2, axis=-1, keepdims=True)
    i2 = jnp.min(jnp.where(lg2 == m2, lane, LANES), axis=-1, keepdims=True)
    d = jnp.exp(m2 - m1)
    g1 = 1.0 / (1.0 + d)
    g2 = d / (1.0 + d)

    sel1 = lane == i1
    sel2 = lane == i2
    onehot = jnp.where(jnp.logical_or(sel1, sel2), 1.0, 0.0)
    before = _dot(tril_ref[...], onehot.astype(BF16)) + carry[...]
    r1 = jnp.sum(jnp.where(sel1, before, 0.0), axis=-1, keepdims=True)
    r2 = jnp.sum(jnp.where(sel2, before, 0.0), axis=-1, keepdims=True)
    carry[...] += jnp.sum(onehot, axis=0, keepdims=True)
    cnt_ref[...] = carry[...]

    meta = jnp.zeros_like(logits)
    for c, val in ((META_I1, i1.astype(F32)), (META_I2, i2.astype(F32)), (META_G1, g1), (META_G2, g2),
                   (META_R1, r1), (META_R2, r2)):
        meta = jnp.where(lane == c, val, meta)
    meta_ref[...] = meta

    _store_tiles(hp_ref, _pack_bf16_pairs(h))
    metat_ref[...] = meta.T[:SUBLANES, :]


def _router(x2, g, router_w):
    T, D = x2.shape
    tm = FFN_TM
    rw = jnp.zeros((D, LANES), F32).at[:, :N_EXPERTS].set(router_w)
    rw_hi = rw.astype(BF16)
    rw_lo = (rw - rw_hi.astype(F32)).astype(BF16)
    rw_split = jnp.concatenate([rw_hi, rw_lo, rw_hi], axis=0)
    tril = jnp.tril(jnp.ones((tm, tm), BF16), -1)
    return pl.pallas_call(
        _router_kernel,
        grid=(T // tm,),
        in_specs=[pl.BlockSpec((tm, D), lambda i: (i, 0)), _const_spec((1, D)), _const_spec((3 * D, LANES)),
                  _const_spec((tm, tm))],
        out_specs=[pl.BlockSpec((tm, LANES), lambda i: (i, 0)), pl.BlockSpec((SUBLANES, tm), lambda i: (0, i)),
                   _tiled_spec(tm, D // 2, lambda i: i), _const_spec((1, LANES))],
        out_shape=[jax.ShapeDtypeStruct((T, LANES), F32), jax.ShapeDtypeStruct((SUBLANES, T), F32),
                   _tiled_shape(T, D // 2, U32), jax.ShapeDtypeStruct((1, LANES), F32)],
        scratch_shapes=[pltpu.VMEM((1, LANES), F32)],
        compiler_params=pltpu.CompilerParams(
            dimension_semantics=("arbitrary",), vmem_limit_bytes=VMEM_LIMIT),
        name="router",
    )(x2, g.reshape(1, -1), rw_split, tril)


def _sc_mesh():
    return plsc.VectorSubcoreMesh(core_axis_name="core", subcore_axis_name="subcore")


def _sc_dispatch(tiled, dest1, dest2, n_out):
    k, n, _ = tiled.shape
    d1, d2 = (_piece_index(d, k, n_out).reshape(-1) for d in (dest1, dest2))
    return _sc_scatter_pieces(tiled.reshape(k * n, LANES), d1, d2, k * n_out).reshape(k, n_out, LANES)


def _sc_gather(tiled, idx):
    k, n_rows, _ = tiled.shape
    G, n = idx.shape
    out = _sc_gather_pieces(tiled.reshape(k * n_rows, LANES), _piece_index(idx, k, n_rows).reshape(-1))
    return out.reshape(G, k, n, LANES)


def _sc_scatter_pieces(rows, dest1, dest2, n_out):
    T, W = rows.shape
    win = SC_WIN

    @pl.kernel(out_type=jax.ShapeDtypeStruct((n_out, W), rows.dtype), mesh=_sc_mesh(), scratch_types=[])
    def scatter_kernel(x_hbm, i1_hbm, i2_hbm, o_hbm):
        def body(x_vmem, i1_vmem, i2_vmem):
            pltpu.sync_copy(x_vmem, o_hbm.at[i1_vmem.at[0]])
            pltpu.sync_copy(x_vmem, o_hbm.at[i2_vmem.at[0]])

        pltpu.emit_pipeline(
            body,
            grid=(T // win,),
            in_specs=[pl.BlockSpec((win, W), lambda i: (i, 0)),
                      pl.BlockSpec((1, win), lambda i: (0, i)),
                      pl.BlockSpec((1, win), lambda i: (0, i))],
            out_specs=[],
            core_axis_name=("core", "subcore"),
            dimension_semantics=(pltpu.PARALLEL,),
        )(x_hbm, i1_hbm, i2_hbm)

    return scatter_kernel(rows, dest1.reshape(1, T), dest2.reshape(1, T))


def _sc_gather_pieces(rows, idx):
    n = idx.shape[0]
    W = rows.shape[1]
    win = SC_WIN

    @pl.kernel(out_type=jax.ShapeDtypeStruct((n, W), rows.dtype), mesh=_sc_mesh(), scratch_types=[])
    def gather_kernel(x_hbm, i_hbm, o_hbm):
        def body(i_vmem, o_vmem):
            pltpu.sync_copy(x_hbm.at[i_vmem.at[0]], o_vmem)

        pltpu.emit_pipeline(
            body,
            grid=(n // win,),
            in_specs=[pl.BlockSpec((1, win), lambda i: (0, i))],
            out_specs=[pl.BlockSpec((win, W), lambda i: (i, 0))],
            core_axis_name=("core", "subcore"),
            dimension_semantics=(pltpu.PARALLEL,),
        )(i_hbm, o_hbm)

    return gather_kernel(rows, idx.reshape(1, n))


def _swiglu(h, wg, wu, wd):
    gate = _dot(h, wg)
    up = _dot(h, wu)
    return _dot((gate * jax.nn.sigmoid(gate) * up).astype(BF16), wd)


def _swiglu_chunked(h, wg_ref, wu_ref, wd_ref, acc=None):
    F = wg_ref.shape[1]
    n_chunks = -(-F // FFN_FC)
    cols = -(-F // (n_chunks * MXU_WIDTH)) * MXU_WIDTH
    for lo in range(0, F, cols):
        sl = slice(lo, min(lo + cols, F))
        y = _swiglu(h, wg_ref[:, sl], wu_ref[:, sl], wd_ref[sl, :])
        acc = y if acc is None else acc + y
    return acc


def _ffn_kernel(x_ref, g_ref, wg_ref, wu_ref, wd_ref, o_ref):
    x = x_ref[...]
    h = _rms(x, g_ref[...]).astype(BF16)
    o_ref[...] = _swiglu_chunked(h, wg_ref, wu_ref, wd_ref, acc=x)


def _ffn(x2, g, wg, wu, wd):
    T, D = x2.shape
    F = wg.shape[1]
    tm = FFN_TM
    return pl.pallas_call(
        _ffn_kernel,
        grid=(T // tm,),
        in_specs=[pl.BlockSpec((tm, D), lambda i: (i, 0)), _const_spec((1, D)),
                  _const_spec((D, F)), _const_spec((D, F)), _const_spec((F, D))],
        out_specs=pl.BlockSpec((tm, D), lambda i: (i, 0)),
        out_shape=jax.ShapeDtypeStruct((T, D), F32),
        compiler_params=pltpu.CompilerParams(
            dimension_semantics=("arbitrary",), vmem_limit_bytes=VMEM_LIMIT),
        name="swiglu",
    )(x2, g.reshape(1, -1), wg, wu, wd)


def _moe_kernel(tidx_ref, texp_ref, nvalid_ref, xs_ref, wg_ref, wu_ref, wd_ref, y_ref):
    del tidx_ref, texp_ref

    @pl.when(pl.program_id(0) < nvalid_ref[0])
    def _():
        h = _unpack_bf16_pairs(_load_tiles(xs_ref)).astype(BF16)
        _store_tiles(y_ref, _pack_bf16_pairs(_swiglu_chunked(h, wg_ref.at[0], wu_ref.at[0], wd_ref.at[0])))


def _moe_experts(xs, tile_idx, tile_expert, n_valid, wg, wu, wd):
    E, D, F = wg.shape
    tm = MOE_TM
    P = xs.shape[1]
    half = xs.shape[0] * LANES
    row_block = lambda g, ti, te, nv: ti[g]
    w_map = lambda g, ti, te, nv: (te[g], 0, 0)
    return pl.pallas_call(
        _moe_kernel,
        grid_spec=pltpu.PrefetchScalarGridSpec(
            num_scalar_prefetch=3,
            grid=(P // tm,),
            in_specs=[_tiled_spec(tm, half, row_block),
                      pl.BlockSpec((1, D, F), w_map), pl.BlockSpec((1, D, F), w_map),
                      pl.BlockSpec((1, F, D), w_map)],
            out_specs=_tiled_spec(tm, D // 2, row_block)),
        out_shape=_tiled_shape(P, D // 2, U32),
        compiler_params=pltpu.CompilerParams(
            dimension_semantics=("arbitrary",), vmem_limit_bytes=VMEM_LIMIT),
        name="moe_experts",
    )(tile_idx, tile_expert, n_valid, xs, wg, wu, wd)


def _combine_kernel(x_ref, y1_ref, y2_ref, meta_ref, fg_ref, o_ref, *, final_norm):
    meta = meta_ref[...]
    out = x_ref[...] + _meta_col(meta, META_G1) * _unpack_bf16_pairs(_load_tiles(y1_ref.at[0]))
    out = out + _meta_col(meta, META_G2) * _unpack_bf16_pairs(_load_tiles(y2_ref.at[0]))
    if final_norm:
        out = _rms(out, fg_ref[...])
    o_ref[...] = out


def _combine(x2, y12, meta, final_g, *, final_norm):
    T, D = x2.shape
    tm = FFN_TM
    return pl.pallas_call(
        functools.partial(_combine_kernel, final_norm=final_norm),
        grid=(T // tm,),
        in_specs=[pl.BlockSpec((tm, D), lambda i: (i, 0)),
                  _tiled_spec(tm, D // 2, lambda i: i, lead=0), _tiled_spec(tm, D // 2, lambda i: i, lead=1),
                  pl.BlockSpec((tm, LANES), lambda i: (i, 0)), _const_spec((1, D))],
        out_specs=pl.BlockSpec((tm, D), lambda i: (i, 0)),
        out_shape=jax.ShapeDtypeStruct((T, D), F32),
        compiler_params=pltpu.CompilerParams(
            dimension_semantics=("arbitrary",), vmem_limit_bytes=VMEM_LIMIT),
        name="moe_combine",
    )(x2, y12, y12, meta, final_g.reshape(1, -1))


def _moe_layer(x2, g, router_w, wg, wu, wd, final_g, *, final_norm):
    T, D = x2.shape
    tm = MOE_TM
    n_tiles = (T * TOP_K) // tm + N_EXPERTS
    meta, meta_t, hp, counts = _router(x2, g, router_w)

    cnt = counts[0, :N_EXPERTS].astype(I32)
    tiles_e = (cnt + tm - 1) // tm
    tile_end = jnp.cumsum(tiles_e)
    row_start = (tile_end - tiles_e) * tm
    experts = jnp.arange(N_EXPERTS, dtype=I32)[:, None]

    def dest(i_row, r_row):
        start = jnp.sum(jnp.where(meta_t[i_row].astype(I32) == experts, row_start[:, None], 0), axis=0)
        return start + meta_t[r_row].astype(I32)

    dest1 = dest(META_I1, META_R1)
    dest2 = dest(META_I2, META_R2)
    n_valid = tile_end[-1:]
    tile_idx = jnp.minimum(jnp.arange(n_tiles, dtype=I32), n_valid - 1)
    tile_expert = jnp.minimum(jnp.sum(tile_end[:, None] <= tile_idx, axis=0), N_EXPERTS - 1).astype(I32)

    xs = _sc_dispatch(hp, dest1, dest2, n_tiles * tm)
    ys = _moe_experts(xs, tile_idx, tile_expert, n_valid.astype(I32), wg, wu, wd)
    y12 = _sc_gather(ys, jnp.stack([dest1, dest2]))
    return _combine(x2, y12, meta, final_g, final_norm=final_norm)


def kernel(x, mem, mix_norm_g, w_in, conv_w, conv_b, conv_ln_g, conv_ln_b, sgu_ln_g, sgu_ln_b, sgu_w, sgu_b,
           pool_w, pool_b, pool_scale, w_out, xattn_norm_g, mem_norm_g, xattn_wq, xattn_wk, xattn_wv, xattn_wo,
           ffn_norm_g, ffn_wg, ffn_wu, ffn_wd, router_w, moe_wg, moe_wu, moe_wd, final_norm_g):
    B, S, D = x.shape
    bf = lambda a: a.astype(BF16)
    for l in range(DEPTH):
        sgu_bias = jnp.repeat(sgu_b[l].T, D_SGU // SGU_HEADS, axis=1)
        pool_wbd = jax.scipy.linalg.block_diag(*[pool_w[l, gi] for gi in range(len(POOL_WINDOWS))])
        x = _mixer(x, mix_norm_g[l], bf(w_in[l]), conv_w[l], conv_b[l], conv_ln_g[l], conv_ln_b[l],
                   sgu_ln_g[l], sgu_ln_b[l], sgu_w[l], sgu_bias, bf(pool_wbd), pool_b[l].reshape(-1),
                   pool_scale[l], bf(w_out[l]))
        k, v = _kv(mem, mem_norm_g[l], bf(xattn_wk[l]), bf(xattn_wv[l]))
        x = _xattn(x, xattn_norm_g[l], bf(xattn_wq[l]), k, v, bf(xattn_wo[l]))
        x2 = x.reshape(B * S, D)
        j = l // 2
        if l % 2 == 0:
            assert l != DEPTH - 1, "the final RMSNorm is fused into the routed layer's combine kernel"
            x2 = _ffn(x2, ffn_norm_g[l], bf(ffn_wg[j]), bf(ffn_wu[j]), bf(ffn_wd[j]))
        else:
            x2 = _moe_layer(x2, ffn_norm_g[l], router_w[j], bf(moe_wg[j]), bf(moe_wu[j]), bf(moe_wd[j]),
                            final_norm_g, final_norm=l == DEPTH - 1)
        x = x2.reshape(B, S, D)
    return x
```

```python
import functools

import jax
import jax.numpy as jnp
from jax import lax
from jax.experimental import pallas as pl
from jax.experimental.pallas import tpu as pltpu
from jax.experimental.pallas import tpu_sc as plsc

F32 = jnp.float32
BF16 = jnp.bfloat16
U32 = jnp.uint32
I32 = jnp.int32

D_MODEL = 1024
DEPTH = 2
CHUNK = 64
D_CONV = 384
CONV_WIDTH = 31
D_SGU = 384
SGU_HEADS = 4
SGU_CHUNK = 128
D_POOL = 256
POOL_WINDOWS = (2, 4, 8, 16)
POOL_GROUP_DIM = D_POOL // len(POOL_WINDOWS)
D_MIX = D_CONV + D_SGU + D_POOL
D_IN = 2 * D_CONV + 2 * D_SGU + D_POOL
X_HEADS = 4
X_HEAD_DIM = D_MODEL // X_HEADS
N_EXPERTS = 8
TOP_K = 2
EPS = 1e-6

LANES = 128
SUBLANES = 8
HIST = 32
MIX_TS = 512
CONV_RB = 64
FFN_TM = 512
FFN_FC = 1408
MXU_WIDTH = 256
MOE_TM = 512
SC_WIN = 128
HI16 = 0xFFFF0000
VMEM_LIMIT = 56 * 1024 * 1024


def _rms(x, g):
    return x * lax.rsqrt(jnp.mean(x * x, axis=-1, keepdims=True) + EPS) * g


def _layer_norm(x, g, b):
    mu = jnp.mean(x, axis=-1, keepdims=True)
    xc = x - mu
    var = jnp.mean(xc * xc, axis=-1, keepdims=True)
    return xc * lax.rsqrt(var + EPS) * g + b


def _dot(a, b):
    return jnp.dot(a, b, preferred_element_type=F32)


def _mixer_tile(x, s, g_ref, win_ref, convw_ref, convb_ref, clng_ref, clnb_ref,
                slng_ref, slnb_ref, sguw_ref, sgub_ref, poolw_ref, poolb_ref, pscale_ref,
                wout_ref, cbuf, cshift, pb0, pb1, pb2, pb3):
    ts = MIX_TS
    h = _rms(x, g_ref[...]).astype(BF16)
    z = _dot(h, win_ref[...])

    a_end = 2 * D_CONV
    cbuf[HIST:HIST + ts, :] = z[:, :D_CONV] * jax.nn.sigmoid(z[:, D_CONV:a_end])
    span = ts + HIST - SUBLANES
    for r in range(1, SUBLANES):
        cshift[r - 1, 0:span, :] = cbuf[r:r + span, :]
    ya_blocks = []
    for rb in range(ts // CONV_RB):
        acc = jnp.zeros((CONV_RB, D_CONV), F32)
        for k in range(CONV_WIDTH):
            start = rb * CONV_RB + HIST - (CONV_WIDTH - 1) + k
            r, base = start % SUBLANES, start - start % SUBLANES
            src = cbuf[base:base + CONV_RB, :] if r == 0 else cshift[r - 1, base:base + CONV_RB, :]
            acc = acc + convw_ref[k:k + 1, :] * src
        ya_blocks.append(acc)
    ya = jnp.concatenate(ya_blocks, axis=0) + convb_ref[...]
    ya = _layer_norm(ya, clng_ref[...], clnb_ref[...])
    ya = ya * jax.nn.sigmoid(ya)
    cbuf[0:HIST, :] = cbuf[ts:ts + HIST, :]

    b_end = a_end + 2 * D_SGU
    zb = jax.nn.gelu(z[:, a_end:b_end])
    u = zb[:, :D_SGU]
    v = _layer_norm(zb[:, D_SGU:], slng_ref[...], slnb_ref[...]).astype(BF16)
    blk_r = lax.broadcasted_iota(jnp.int32, (SGU_CHUNK, SGU_CHUNK), 0) // CHUNK
    blk_c = lax.broadcasted_iota(jnp.int32, (SGU_CHUNK, SGU_CHUNK), 1) // CHUNK
    head_of_lane = lax.broadcasted_iota(jnp.int32, (SGU_CHUNK, D_SGU), 1) // (D_SGU // SGU_HEADS)
    w_heads = [jnp.where(blk_r >= blk_c, sguw_ref[hd], 0.0).astype(BF16) for hd in range(SGU_HEADS)]
    s_chunks = []
    for c in range(ts // SGU_CHUNK):
        vc = v[c * SGU_CHUNK:(c + 1) * SGU_CHUNK, :]
        sc = jnp.zeros((SGU_CHUNK, D_SGU), F32)
        for hd in range(SGU_HEADS):
            sc = jnp.where(head_of_lane == hd, _dot(w_heads[hd], vc), sc)
        s_chunks.append(sc + sgub_ref[...])
    yb = u * jnp.concatenate(s_chunks, axis=0)

    cc = z[:, b_end:]
    pb0[HIST:HIST + ts, :] = cc
    pb1[0:ts + 24, :] = pb0[8:ts + 32, :] + pb0[7:ts + 31, :]
    pb2[0:ts + 16, :] = pb1[8:ts + 24, :] + pb1[6:ts + 22, :]
    pb3[0:ts + 8, :] = pb2[8:ts + 16, :] + pb2[4:ts + 12, :]
    s16 = pb3[8:ts + 8, :] + pb3[0:ts, :]
    s8 = pb3[8:ts + 8, :]
    s4 = pb2[16:ts + 16, :]
    s2 = pb1[24:ts + 24, :]
    grp = lax.broadcasted_iota(jnp.int32, (ts, D_POOL), 1) // POOL_GROUP_DIM
    wsum = jnp.where(grp == 0, s2, jnp.where(grp == 1, s4, jnp.where(grp == 2, s8, s16)))
    win = jnp.where(grp == 0, 2, jnp.where(grp == 1, 4, jnp.where(grp == 2, 8, 16)))
    pos = s * ts + lax.broadcasted_iota(jnp.int32, (ts, D_POOL), 0)
    cnt = jnp.minimum(pos + 1, win).astype(F32)
    p = (wsum / cnt - cc).astype(BF16)
    yc = (_dot(p, poolw_ref[...]) + poolb_ref[...]) * pscale_ref[...]
    pb0[0:HIST, :] = pb0[ts:ts + HIST, :]

    mixed = jnp.concatenate([ya.astype(BF16), yb.astype(BF16), yc.astype(BF16)], axis=-1)
    return x + _dot(mixed, wout_ref[...])


def _const_spec(shape):
    zeros = (0,) * len(shape)
    return pl.BlockSpec(shape, lambda *_: zeros)


N_MIX_PARAMS = 14
N_ATT_PARAMS = 5


def _mix_attn_kernel(x_ref, *refs, n_tiles, tiles_per_row):
    mix_refs = refs[:N_MIX_PARAMS]
    att_refs = refs[N_MIX_PARAMS:N_MIX_PARAMS + N_ATT_PARAMS]
    o_ref, mid, cbuf, cshift, pb0, pb1, pb2, pb3 = refs[N_MIX_PARAMS + N_ATT_PARAMS:]
    i = pl.program_id(0)
    s = jnp.minimum(i, n_tiles - 1) % tiles_per_row

    @pl.when(i == 0)
    def _():
        mid[...] = jnp.zeros_like(mid)

    @pl.when(s == 0)
    def _():
        cbuf[0:HIST, :] = jnp.zeros((HIST, D_CONV), F32)
        pb0[0:HIST, :] = jnp.zeros((HIST, D_POOL), F32)

    o_ref[0] = _xattn_tile(mid[...], *att_refs)
    mid[...] = _mixer_tile(x_ref[0], s, *mix_refs, cbuf, cshift, pb0, pb1, pb2, pb3)


def _mix_attn(x, mix_params, k, v, att_params):
    B, S, D = x.shape
    M = k.shape[1]
    ts = MIX_TS
    tiles_per_row = S // ts
    n_tiles = B * tiles_per_row
    g_att, wq, wo = att_params
    row = lambda a: a.reshape(1, -1) if a.ndim == 1 else a
    mix_args = [row(a) for a in mix_params]
    att_args = [row(g_att), wq, k, v, wo]
    assert len(mix_args) == N_MIX_PARAMS and len(att_args) == N_ATT_PARAMS

    def mix_tile(i):
        t = jnp.minimum(i, n_tiles - 1)
        return t // tiles_per_row, t % tiles_per_row, 0

    def att_tile(i):
        t = jnp.maximum(i - 1, 0)
        return t // tiles_per_row, t % tiles_per_row, 0

    single = lambda a: pl.BlockSpec(a.shape, lambda i, nd=a.ndim: (0,) * nd, pipeline_mode=pl.Buffered(1))
    kv_spec = pl.BlockSpec((1, M, D), lambda i: (att_tile(i)[0], 0, 0))
    in_specs = [pl.BlockSpec((1, ts, D), mix_tile)] + [single(a) for a in mix_args]
    in_specs += [single(att_args[0]), single(wq), kv_spec, kv_spec, single(wo)]
    return pl.pallas_call(
        functools.partial(_mix_attn_kernel, n_tiles=n_tiles, tiles_per_row=tiles_per_row),
        grid=(n_tiles + 1,),
        in_specs=in_specs,
        out_specs=pl.BlockSpec((1, ts, D), att_tile),
        out_shape=jax.ShapeDtypeStruct((B, S, D), F32),
        scratch_shapes=[pltpu.VMEM((ts, D), F32),
                        pltpu.VMEM((ts + HIST, D_CONV), F32),
                        pltpu.VMEM((SUBLANES - 1, ts + HIST - SUBLANES, D_CONV), F32)]
                       + [pltpu.VMEM((ts + HIST, D_POOL), F32)] * 4,
        compiler_params=pltpu.CompilerParams(
            dimension_semantics=("arbitrary",), vmem_limit_bytes=VMEM_LIMIT),
        name="mix_attn",
    )(x, *mix_args, *att_args)


def _kv_kernel(mem_ref, g_ref, wk_ref, wv_ref, k_ref, v_ref):
    m = _rms(mem_ref[0], g_ref[...]).astype(BF16)
    k_ref[0] = _dot(m, wk_ref[...]).astype(BF16)
    v_ref[0] = _dot(m, wv_ref[...]).astype(BF16)


def _kv(mem, g, wk, wv):
    B, M, D = mem.shape
    return pl.pallas_call(
        _kv_kernel,
        grid=(B,),
        in_specs=[pl.BlockSpec((1, M, D), lambda b: (b, 0, 0)), _const_spec((1, D)),
                  _const_spec((D, D)), _const_spec((D, D))],
        out_specs=[pl.BlockSpec((1, M, D), lambda b: (b, 0, 0))] * 2,
        out_shape=[jax.ShapeDtypeStruct((B, M, D), BF16)] * 2,
        compiler_params=pltpu.CompilerParams(
            dimension_semantics=("arbitrary",), vmem_limit_bytes=VMEM_LIMIT),
        name="mem_kv",
    )(mem, g.reshape(1, -1), wk, wv)


def _xattn_tile(x, g_ref, wq_ref, k_ref, v_ref, wo_ref):
    h = _rms(x, g_ref[...]).astype(BF16)
    q = _dot(h, wq_ref[...]).astype(BF16)
    heads = []
    for hd in range(X_HEADS):
        sl = slice(hd * X_HEAD_DIM, (hd + 1) * X_HEAD_DIM)
        sc = lax.dot_general(q[:, sl], k_ref[0, :, sl], (((1,), (1,)), ((), ())),
                             preferred_element_type=F32) * (X_HEAD_DIM ** -0.5)
        e = jnp.exp(sc - jnp.max(sc, axis=-1, keepdims=True))
        heads.append(_dot(e.astype(BF16), v_ref[0, :, sl]) / jnp.sum(e, axis=-1, keepdims=True))
    o = jnp.concatenate(heads, axis=-1).astype(BF16)
    return x + _dot(o, wo_ref[...])


def _store_tiles(ref, val):
    for j in range(ref.shape[0]):
        ref[j] = val[:, j * LANES:(j + 1) * LANES]


def _load_tiles(ref):
    return jnp.concatenate([ref[j] for j in range(ref.shape[0])], axis=1)


def _pack_bf16_pairs(v):
    bits = lax.bitcast_convert_type(v.astype(BF16).astype(F32), U32)
    half = bits.shape[1] // 2
    return (bits[:, :half] & jnp.uint32(HI16)) | (bits[:, half:] >> 16)


def _unpack_bf16_pairs(w):
    hi = lax.bitcast_convert_type(w & jnp.uint32(HI16), F32)
    lo = lax.bitcast_convert_type(w << 16, F32)
    return jnp.concatenate([hi, lo], axis=1)


def _tiled_shape(rows, width, dtype):
    return jax.ShapeDtypeStruct((width // LANES, rows, LANES), dtype)


def _tiled_spec(rows, width, row_block, lead=None):
    block = (width // LANES, rows, LANES)
    if lead is None:
        return pl.BlockSpec(block, lambda *a: (0, row_block(*a), 0))
    return pl.BlockSpec((1,) + block, lambda *a: (lead, 0, row_block(*a), 0))


def _piece_index(rows, k, n_rows):
    return jnp.arange(k, dtype=I32)[:, None] * n_rows + rows[..., None, :]


META_I1, META_I2, META_G1, META_G2, META_R1, META_R2 = range(6)


def _meta_col(meta, col):
    lane = lax.broadcasted_iota(I32, meta.shape, 1)
    return jnp.sum(jnp.where(lane == col, meta, 0.0), axis=-1, keepdims=True)


def _router_kernel(x_ref, g_ref, rw_ref, tril_ref, meta_ref, metat_ref, hp_ref, cnt_ref, carry):
    @pl.when(pl.program_id(0) == 0)
    def _():
        carry[...] = jnp.zeros_like(carry)

    h = _rms(x_ref[...], g_ref[...])
    h_hi = h.astype(BF16)
    h_lo = (h - h_hi.astype(F32)).astype(BF16)
    logits = _dot(jnp.concatenate([h_hi, h_hi, h_lo], axis=1), rw_ref[...])
    lane = lax.broadcasted_iota(I32, logits.shape, 1)
    neg = jnp.float32(-jnp.inf)
    lg = jnp.where(lane < N_EXPERTS, logits, neg)
    m1 = jnp.max(lg, axis=-1, keepdims=True)
    i1 = jnp.min(jnp.where(lg == m1, lane, LANES), axis=-1, keepdims=True)
    lg2 = jnp.where(lane == i1, neg, lg)
    m2 = jnp.max(lg2, axis=-1, keepdims=True)
    i2 = jnp.min(jnp.where(lg2 == m2, lane, LANES), axis=-1, keepdims=True)
    d = jnp.exp(m2 - m1)
    g1 = 1.0 / (1.0 + d)
    g2 = d / (1.0 + d)

    sel1 = lane == i1
    sel2 = lane == i2
    onehot = jnp.where(jnp.logical_or(sel1, sel2), 1.0, 0.0)
    before = _dot(tril_ref[...], onehot.astype(BF16)) + carry[...]
    r1 = jnp.sum(jnp.where(sel1, before, 0.0), axis=-1, keepdims=True)
    r2 = jnp.sum(jnp.where(sel2, before, 0.0), axis=-1, keepdims=True)
    carry[...] += jnp.sum(onehot, axis=0, keepdims=True)
    cnt_ref[...] = carry[...]

    meta = jnp.zeros_like(logits)
    for c, val in ((META_I1, i1.astype(F32)), (META_I2, i2.astype(F32)), (META_G1, g1), (META_G2, g2),
                   (META_R1, r1), (META_R2, r2)):
        meta = jnp.where(lane == c, val, meta)
    meta_ref[...] = meta

    _store_tiles(hp_ref, _pack_bf16_pairs(h))
    metat_ref[...] = meta.T[:SUBLANES, :]


def _router(x2, g, router_w):
    T, D = x2.shape
    tm = FFN_TM
    rw = jnp.zeros((D, LANES), F32).at[:, :N_EXPERTS].set(router_w)
    rw_hi = rw.astype(BF16)
    rw_lo = (rw - rw_hi.astype(F32)).astype(BF16)
    rw_split = jnp.concatenate([rw_hi, rw_lo, rw_hi], axis=0)
    tril = jnp.tril(jnp.ones((tm, tm), BF16), -1)
    return pl.pallas_call(
        _router_kernel,
        grid=(T // tm,),
        in_specs=[pl.BlockSpec((tm, D), lambda i: (i, 0)), _const_spec((1, D)), _const_spec((3 * D, LANES)),
                  _const_spec((tm, tm))],
        out_specs=[pl.BlockSpec((tm, LANES), lambda i: (i, 0)), pl.BlockSpec((SUBLANES, tm), lambda i: (0, i)),
                   _tiled_spec(tm, D // 2, lambda i: i), _const_spec((1, LANES))],
        out_shape=[jax.ShapeDtypeStruct((T, LANES), F32), jax.ShapeDtypeStruct((SUBLANES, T), F32),
                   _tiled_shape(T, D // 2, U32), jax.ShapeDtypeStruct((1, LANES), F32)],
        scratch_shapes=[pltpu.VMEM((1, LANES), F32)],
        compiler_params=pltpu.CompilerParams(
            dimension_semantics=("arbitrary",), vmem_limit_bytes=VMEM_LIMIT),
        name="router",
    )(x2, g.reshape(1, -1), rw_split, tril)


def _sc_mesh():
    return plsc.VectorSubcoreMesh(core_axis_name="core", subcore_axis_name="subcore")


def _sc_dispatch(tiled, dest1, dest2, n_out):
    k, n, _ = tiled.shape
    d1, d2 = (_piece_index(d, k, n_out).reshape(-1) for d in (dest1, dest2))
    return _sc_scatter_pieces(tiled.reshape(k * n, LANES), d1, d2, k * n_out).reshape(k, n_out, LANES)


def _sc_gather(tiled, idx):
    k, n_rows, _ = tiled.shape
    G, n = idx.shape
    out = _sc_gather_pieces(tiled.reshape(k * n_rows, LANES), _piece_index(idx, k, n_rows).reshape(-1))
    return out.reshape(G, k, n, LANES)


def _sc_scatter_pieces(rows, dest1, dest2, n_out):
    T, W = rows.shape
    win = SC_WIN

    @pl.kernel(out_type=jax.ShapeDtypeStruct((n_out, W), rows.dtype), mesh=_sc_mesh(), scratch_types=[])
    def scatter_kernel(x_hbm, i1_hbm, i2_hbm, o_hbm):
        def body(x_vmem, i1_vmem, i2_vmem):
            pltpu.sync_copy(x_vmem, o_hbm.at[i1_vmem.at[0]])
            pltpu.sync_copy(x_vmem, o_hbm.at[i2_vmem.at[0]])

        pltpu.emit_pipeline(
            body,
            grid=(T // win,),
            in_specs=[pl.BlockSpec((win, W), lambda i: (i, 0)),
                      pl.BlockSpec((1, win), lambda i: (0, i)),
                      pl.BlockSpec((1, win), lambda i: (0, i))],
            out_specs=[],
            core_axis_name=("core", "subcore"),
            dimension_semantics=(pltpu.PARALLEL,),
        )(x_hbm, i1_hbm, i2_hbm)

    return scatter_kernel(rows, dest1.reshape(1, T), dest2.reshape(1, T))


def _sc_gather_pieces(rows, idx):
    n = idx.shape[0]
    W = rows.shape[1]
    win = SC_WIN

    @pl.kernel(out_type=jax.ShapeDtypeStruct((n, W), rows.dtype), mesh=_sc_mesh(), scratch_types=[])
    def gather_kernel(x_hbm, i_hbm, o_hbm):
        def body(i_vmem, o_vmem):
            pltpu.sync_copy(x_hbm.at[i_vmem.at[0]], o_vmem)

        pltpu.emit_pipeline(
            body,
            grid=(n // win,),
            in_specs=[pl.BlockSpec((1, win), lambda i: (0, i))],
            out_specs=[pl.BlockSpec((win, W), lambda i: (i, 0))],
            core_axis_name=("core", "subcore"),
            dimension_semantics=(pltpu.PARALLEL,),
        )(i_hbm, o_hbm)

    return gather_kernel(rows, idx.reshape(1, n))


def _swiglu(h, wg, wu, wd):
    gate = _dot(h, wg)
    up = _dot(h, wu)
    return _dot((gate * jax.nn.sigmoid(gate) * up).astype(BF16), wd)


def _swiglu_chunked(h, wg_ref, wu_ref, wd_ref, acc=None):
    F = wg_ref.shape[1]
    n_chunks = -(-F // FFN_FC)
    cols = -(-F // (n_chunks * MXU_WIDTH)) * MXU_WIDTH
    for lo in range(0, F, cols):
        sl = slice(lo, min(lo + cols, F))
        y = _swiglu(h, wg_ref[:, sl], wu_ref[:, sl], wd_ref[sl, :])
        acc = y if acc is None else acc + y
    return acc


def _ffn_kernel(x_ref, g_ref, wg_ref, wu_ref, wd_ref, o_ref):
    x = x_ref[...]
    h = _rms(x, g_ref[...]).astype(BF16)
    o_ref[...] = _swiglu_chunked(h, wg_ref, wu_ref, wd_ref, acc=x)


def _ffn(x2, g, wg, wu, wd):
    T, D = x2.shape
    F = wg.shape[1]
    tm = FFN_TM
    return pl.pallas_call(
        _ffn_kernel,
        grid=(T // tm,),
        in_specs=[pl.BlockSpec((tm, D), lambda i: (i, 0)), _const_spec((1, D)),
                  _const_spec((D, F)), _const_spec((D, F)), _const_spec((F, D))],
        out_specs=pl.BlockSpec((tm, D), lambda i: (i, 0)),
        out_shape=jax.ShapeDtypeStruct((T, D), F32),
        compiler_params=pltpu.CompilerParams(
            dimension_semantics=("arbitrary",), vmem_limit_bytes=VMEM_LIMIT),
        name="swiglu",
    )(x2, g.reshape(1, -1), wg, wu, wd)


def _moe_kernel(tidx_ref, texp_ref, nvalid_ref, xs_ref, wg_ref, wu_ref, wd_ref, y_ref):
    del tidx_ref, texp_ref

    @pl.when(pl.program_id(0) < nvalid_ref[0])
    def _():
        h = _unpack_bf16_pairs(_load_tiles(xs_ref)).astype(BF16)
        _store_tiles(y_ref, _pack_bf16_pairs(_swiglu_chunked(h, wg_ref.at[0], wu_ref.at[0], wd_ref.at[0])))


def _moe_experts(xs, tile_idx, tile_expert, n_valid, wg, wu, wd):
    E, D, F = wg.shape
    tm = MOE_TM
    P = xs.shape[1]
    half = xs.shape[0] * LANES
    row_block = lambda g, ti, te, nv: ti[g]
    w_map = lambda g, ti, te, nv: (te[g], 0, 0)
    return pl.pallas_call(
        _moe_kernel,
        grid_spec=pltpu.PrefetchScalarGridSpec(
            num_scalar_prefetch=3,
            grid=(P // tm,),
            in_specs=[_tiled_spec(tm, half, row_block),
                      pl.BlockSpec((1, D, F), w_map), pl.BlockSpec((1, D, F), w_map),
                      pl.BlockSpec((1, F, D), w_map)],
            out_specs=_tiled_spec(tm, D // 2, row_block)),
        out_shape=_tiled_shape(P, D // 2, U32),
        compiler_params=pltpu.CompilerParams(
            dimension_semantics=("arbitrary",), vmem_limit_bytes=VMEM_LIMIT),
        name="moe_experts",
    )(tile_idx, tile_expert, n_valid, xs, wg, wu, wd)


def _combine_kernel(x_ref, y1_ref, y2_ref, meta_ref, fg_ref, o_ref, *, final_norm):
    meta = meta_ref[...]
    out = x_ref[...] + _meta_col(meta, META_G1) * _unpack_bf16_pairs(_load_tiles(y1_ref.at[0]))
    out = out + _meta_col(meta, META_G2) * _unpack_bf16_pairs(_load_tiles(y2_ref.at[0]))
    if final_norm:
        out = _rms(out, fg_ref[...])
    o_ref[...] = out


def _combine(x2, y12, meta, final_g, *, final_norm):
    T, D = x2.shape
    tm = FFN_TM
    return pl.pallas_call(
        functools.partial(_combine_kernel, final_norm=final_norm),
        grid=(T // tm,),
        in_specs=[pl.BlockSpec((tm, D), lambda i: (i, 0)),
                  _tiled_spec(tm, D // 2, lambda i: i, lead=0), _tiled_spec(tm, D // 2, lambda i: i, lead=1),
                  pl.BlockSpec((tm, LANES), lambda i: (i, 0)), _const_spec((1, D))],
        out_specs=pl.BlockSpec((tm, D), lambda i: (i, 0)),
        out_shape=jax.ShapeDtypeStruct((T, D), F32),
        compiler_params=pltpu.CompilerParams(
            dimension_semantics=("arbitrary",), vmem_limit_bytes=VMEM_LIMIT),
        name="moe_combine",
    )(x2, y12, y12, meta, final_g.reshape(1, -1))


def _moe_layer(x2, g, router_w, wg, wu, wd, final_g, *, final_norm):
    T, D = x2.shape
    tm = MOE_TM
    n_tiles = (T * TOP_K) // tm + N_EXPERTS
    meta, meta_t, hp, counts = _router(x2, g, router_w)

    cnt = counts[0, :N_EXPERTS].astype(I32)
    tiles_e = (cnt + tm - 1) // tm
    tile_end = jnp.cumsum(tiles_e)
    row_start = (tile_end - tiles_e) * tm
    experts = jnp.arange(N_EXPERTS, dtype=I32)[:, None]

    def dest(i_row, r_row):
        start = jnp.sum(jnp.where(meta_t[i_row].astype(I32) == experts, row_start[:, None], 0), axis=0)
        return start + meta_t[r_row].astype(I32)

    dest1 = dest(META_I1, META_R1)
    dest2 = dest(META_I2, META_R2)
    n_valid = tile_end[-1:]
    tile_idx = jnp.minimum(jnp.arange(n_tiles, dtype=I32), n_valid - 1)
    tile_expert = jnp.minimum(jnp.sum(tile_end[:, None] <= tile_idx, axis=0), N_EXPERTS - 1).astype(I32)

    xs = _sc_dispatch(hp, dest1, dest2, n_tiles * tm)
    ys = _moe_experts(xs, tile_idx, tile_expert, n_valid.astype(I32), wg, wu, wd)
    y12 = _sc_gather(ys, jnp.stack([dest1, dest2]))
    return _combine(x2, y12, meta, final_g, final_norm=final_norm)


def kernel(x, mem, mix_norm_g, w_in, conv_w, conv_b, conv_ln_g, conv_ln_b, sgu_ln_g, sgu_ln_b, sgu_w, sgu_b,
           pool_w, pool_b, pool_scale, w_out, xattn_norm_g, mem_norm_g, xattn_wq, xattn_wk, xattn_wv, xattn_wo,
           ffn_norm_g, ffn_wg, ffn_wu, ffn_wd, router_w, moe_wg, moe_wu, moe_wd, final_norm_g):
    B, S, D = x.shape
    bf = lambda a: a.astype(BF16)
    for l in range(DEPTH):
        sgu_bias = jnp.repeat(sgu_b[l].T, D_SGU // SGU_HEADS, axis=1)
        pool_wbd = jax.scipy.linalg.block_diag(*[pool_w[l, gi] for gi in range(len(POOL_WINDOWS))])
        mix_params = (mix_norm_g[l], bf(w_in[l]), conv_w[l], conv_b[l], conv_ln_g[l], conv_ln_b[l],
                      sgu_ln_g[l], sgu_ln_b[l], sgu_w[l], sgu_bias, bf(pool_wbd), pool_b[l].reshape(-1),
                      pool_scale[l], bf(w_out[l]))
        k, v = _kv(mem, mem_norm_g[l], bf(xattn_wk[l]), bf(xattn_wv[l]))
        x = _mix_attn(x, mix_params, k, v, (xattn_norm_g[l], bf(xattn_wq[l]), bf(xattn_wo[l])))
        x2 = x.reshape(B * S, D)
        j = l // 2
        if l % 2 == 0:
            assert l != DEPTH - 1, "the final RMSNorm is fused into the routed layer's combine kernel"
            x2 = _ffn(x2, ffn_norm_g[l], bf(ffn_wg[j]), bf(ffn_wu[j]), bf(ffn_wd[j]))
        else:
            x2 = _moe_layer(x2, ffn_norm_g[l], router_w[j], bf(moe_wg[j]), bf(moe_wu[j]), bf(moe_wd[j]),
                            final_norm_g, final_norm=l == DEPTH - 1)
        x = x2.reshape(B, S, D)
    return x
```

```python
import functools

import jax
import jax.numpy as jnp
from jax import lax
from jax.experimental import pallas as pl
from jax.experimental.pallas import tpu as pltpu
from jax.experimental.pallas import tpu_sc as plsc

F32 = jnp.float32
BF16 = jnp.bfloat16
U32 = jnp.uint32
I32 = jnp.int32

D_MODEL = 1024
DEPTH = 2
CHUNK = 64
D_CONV = 384
CONV_WIDTH = 31
D_SGU = 384
SGU_HEADS = 4
SGU_CHUNK = 128
D_POOL = 256
POOL_WINDOWS = (2, 4, 8, 16)
POOL_GROUP_DIM = D_POOL // len(POOL_WINDOWS)
D_MIX = D_CONV + D_SGU + D_POOL
D_IN = 2 * D_CONV + 2 * D_SGU + D_POOL
X_HEADS = 4
X_HEAD_DIM = D_MODEL // X_HEADS
N_EXPERTS = 8
TOP_K = 2
EPS = 1e-6

LANES = 128
SUBLANES = 8
HIST = 32
MIX_TS = 512
CONV_RB = 64
FFN_TM = 512
FFN_FC = 1408
MXU_WIDTH = 256
MOE_TM = 512
SC_WIN = 128
HI16 = 0xFFFF0000
VMEM_LIMIT = 56 * 1024 * 1024


def _rms(x, g):
    return x * lax.rsqrt(jnp.mean(x * x, axis=-1, keepdims=True) + EPS) * g


def _layer_norm(x, g, b):
    mu = jnp.mean(x, axis=-1, keepdims=True)
    xc = x - mu
    var = jnp.mean(xc * xc, axis=-1, keepdims=True)
    return xc * lax.rsqrt(var + EPS) * g + b


def _dot(a, b):
    return jnp.dot(a, b, preferred_element_type=F32)


def _mixer_tile(x, s, g_ref, win_ref, convw_ref, convb_ref, clng_ref, clnb_ref,
                slng_ref, slnb_ref, sguw_ref, sgub_ref, poolw_ref, poolb_ref, pscale_ref,
                wout_ref, cbuf, cshift, pb0, pb1, pb2, pb3, overlap):
    ts = MIX_TS
    h = _rms(x, g_ref[...]).astype(BF16)
    z = _dot(h, win_ref[...])

    a_end = 2 * D_CONV
    cbuf[HIST:HIST + ts, :] = z[:, :D_CONV] * jax.nn.sigmoid(z[:, D_CONV:a_end])
    span = ts + HIST - SUBLANES
    for r in range(1, SUBLANES):
        cshift[r - 1, 0:span, :] = cbuf[r:r + span, :]
    ya_blocks = []
    for rb in range(ts // CONV_RB):
        acc = jnp.zeros((CONV_RB, D_CONV), F32)
        for k in range(CONV_WIDTH):
            start = rb * CONV_RB + HIST - (CONV_WIDTH - 1) + k
            r, base = start % SUBLANES, start - start % SUBLANES
            src = cbuf[base:base + CONV_RB, :] if r == 0 else cshift[r - 1, base:base + CONV_RB, :]
            acc = acc + convw_ref[k:k + 1, :] * src
        ya_blocks.append(acc)
        next(overlap, None)
    for _ in overlap:
        pass
    ya = jnp.concatenate(ya_blocks, axis=0) + convb_ref[...]
    ya = _layer_norm(ya, clng_ref[...], clnb_ref[...])
    ya = ya * jax.nn.sigmoid(ya)
    cbuf[0:HIST, :] = cbuf[ts:ts + HIST, :]

    b_end = a_end + 2 * D_SGU
    zb = jax.nn.gelu(z[:, a_end:b_end])
    u = zb[:, :D_SGU]
    v = _layer_norm(zb[:, D_SGU:], slng_ref[...], slnb_ref[...]).astype(BF16)
    blk_r = lax.broadcasted_iota(jnp.int32, (SGU_CHUNK, SGU_CHUNK), 0) // CHUNK
    blk_c = lax.broadcasted_iota(jnp.int32, (SGU_CHUNK, SGU_CHUNK), 1) // CHUNK
    head_of_lane = lax.broadcasted_iota(jnp.int32, (SGU_CHUNK, D_SGU), 1) // (D_SGU // SGU_HEADS)
    w_heads = [jnp.where(blk_r >= blk_c, sguw_ref[hd], 0.0).astype(BF16) for hd in range(SGU_HEADS)]
    s_chunks = []
    for c in range(ts // SGU_CHUNK):
        vc = v[c * SGU_CHUNK:(c + 1) * SGU_CHUNK, :]
        sc = jnp.zeros((SGU_CHUNK, D_SGU), F32)
        for hd in range(SGU_HEADS):
            sc = jnp.where(head_of_lane == hd, _dot(w_heads[hd], vc), sc)
        s_chunks.append(sc + sgub_ref[...])
    yb = u * jnp.concatenate(s_chunks, axis=0)

    cc = z[:, b_end:]
    pb0[HIST:HIST + ts, :] = cc
    pb1[0:ts + 24, :] = pb0[8:ts + 32, :] + pb0[7:ts + 31, :]
    pb2[0:ts + 16, :] = pb1[8:ts + 24, :] + pb1[6:ts + 22, :]
    pb3[0:ts + 8, :] = pb2[8:ts + 16, :] + pb2[4:ts + 12, :]
    s16 = pb3[8:ts + 8, :] + pb3[0:ts, :]
    s8 = pb3[8:ts + 8, :]
    s4 = pb2[16:ts + 16, :]
    s2 = pb1[24:ts + 24, :]
    grp = lax.broadcasted_iota(jnp.int32, (ts, D_POOL), 1) // POOL_GROUP_DIM
    wsum = jnp.where(grp == 0, s2, jnp.where(grp == 1, s4, jnp.where(grp == 2, s8, s16)))
    win = jnp.where(grp == 0, 2, jnp.where(grp == 1, 4, jnp.where(grp == 2, 8, 16)))
    pos = s * ts + lax.broadcasted_iota(jnp.int32, (ts, D_POOL), 0)
    cnt = jnp.minimum(pos + 1, win).astype(F32)
    p = (wsum / cnt - cc).astype(BF16)
    yc = (_dot(p, poolw_ref[...]) + poolb_ref[...]) * pscale_ref[...]
    pb0[0:HIST, :] = pb0[ts:ts + HIST, :]

    mixed = jnp.concatenate([ya.astype(BF16), yb.astype(BF16), yc.astype(BF16)], axis=-1)
    return x + _dot(mixed, wout_ref[...])


def _const_spec(shape):
    zeros = (0,) * len(shape)
    return pl.BlockSpec(shape, lambda *_: zeros)


N_MIX_PARAMS = 14
N_ATT_PARAMS = 5


def _mix_attn_kernel(x_ref, *refs, n_tiles, tiles_per_row):
    mix_refs = refs[:N_MIX_PARAMS]
    att_refs = refs[N_MIX_PARAMS:N_MIX_PARAMS + N_ATT_PARAMS]
    o_ref, mid, cbuf, cshift, pb0, pb1, pb2, pb3 = refs[N_MIX_PARAMS + N_ATT_PARAMS:]
    i = pl.program_id(0)
    s = jnp.minimum(i, n_tiles - 1) % tiles_per_row

    @pl.when(i == 0)
    def _():
        mid[...] = jnp.zeros_like(mid)

    @pl.when(s == 0)
    def _():
        cbuf[0:HIST, :] = jnp.zeros((HIST, D_CONV), F32)
        pb0[0:HIST, :] = jnp.zeros((HIST, D_POOL), F32)

    attention = _xattn_pieces(mid[...], o_ref, *att_refs)
    mid[...] = _mixer_tile(x_ref[0], s, *mix_refs, cbuf, cshift, pb0, pb1, pb2, pb3, overlap=attention)


def _mix_attn(x, mix_params, k, v, att_params):
    B, S, D = x.shape
    M = k.shape[1]
    ts = MIX_TS
    tiles_per_row = S // ts
    n_tiles = B * tiles_per_row
    g_att, wq, wo = att_params
    row = lambda a: a.reshape(1, -1) if a.ndim == 1 else a
    mix_args = [row(a) for a in mix_params]
    att_args = [row(g_att), wq, k, v, wo]
    assert len(mix_args) == N_MIX_PARAMS and len(att_args) == N_ATT_PARAMS

    def mix_tile(i):
        t = jnp.minimum(i, n_tiles - 1)
        return t // tiles_per_row, t % tiles_per_row, 0

    def att_tile(i):
        t = jnp.maximum(i - 1, 0)
        return t // tiles_per_row, t % tiles_per_row, 0

    single = lambda a: pl.BlockSpec(a.shape, lambda i, nd=a.ndim: (0,) * nd, pipeline_mode=pl.Buffered(1))
    kv_spec = pl.BlockSpec((1, M, D), lambda i: (att_tile(i)[0], 0, 0))
    in_specs = [pl.BlockSpec((1, ts, D), mix_tile)] + [single(a) for a in mix_args]
    in_specs += [single(att_args[0]), single(wq), kv_spec, kv_spec, single(wo)]
    return pl.pallas_call(
        functools.partial(_mix_attn_kernel, n_tiles=n_tiles, tiles_per_row=tiles_per_row),
        grid=(n_tiles + 1,),
        in_specs=in_specs,
        out_specs=pl.BlockSpec((1, ts, D), att_tile),
        out_shape=jax.ShapeDtypeStruct((B, S, D), F32),
        scratch_shapes=[pltpu.VMEM((ts, D), F32),
                        pltpu.VMEM((ts + HIST, D_CONV), F32),
                        pltpu.VMEM((SUBLANES - 1, ts + HIST - SUBLANES, D_CONV), F32)]
                       + [pltpu.VMEM((ts + HIST, D_POOL), F32)] * 4,
        compiler_params=pltpu.CompilerParams(
            dimension_semantics=("arbitrary",), vmem_limit_bytes=VMEM_LIMIT),
        name="mix_attn",
    )(x, *mix_args, *att_args)


def _kv_kernel(mem_ref, g_ref, wk_ref, wv_ref, k_ref, v_ref):
    m = _rms(mem_ref[0], g_ref[...]).astype(BF16)
    k_ref[0] = _dot(m, wk_ref[...]).astype(BF16)
    v_ref[0] = _dot(m, wv_ref[...]).astype(BF16)


def _kv(mem, g, wk, wv):
    B, M, D = mem.shape
    return pl.pallas_call(
        _kv_kernel,
        grid=(B,),
        in_specs=[pl.BlockSpec((1, M, D), lambda b: (b, 0, 0)), _const_spec((1, D)),
                  _const_spec((D, D)), _const_spec((D, D))],
        out_specs=[pl.BlockSpec((1, M, D), lambda b: (b, 0, 0))] * 2,
        out_shape=[jax.ShapeDtypeStruct((B, M, D), BF16)] * 2,
        compiler_params=pltpu.CompilerParams(
            dimension_semantics=("arbitrary",), vmem_limit_bytes=VMEM_LIMIT),
        name="mem_kv",
    )(mem, g.reshape(1, -1), wk, wv)


def _xattn_pieces(x, o_ref, g_ref, wq_ref, k_ref, v_ref, wo_ref):
    h = _rms(x, g_ref[...]).astype(BF16)
    heads = []
    for hd in range(X_HEADS):
        sl = slice(hd * X_HEAD_DIM, (hd + 1) * X_HEAD_DIM)
        q = _dot(h, wq_ref[:, sl]).astype(BF16)
        sc = lax.dot_general(q, k_ref[0, :, sl], (((1,), (1,)), ((), ())),
                             preferred_element_type=F32) * (X_HEAD_DIM ** -0.5)
        e = jnp.exp(sc - jnp.max(sc, axis=-1, keepdims=True))
        heads.append(_dot(e.astype(BF16), v_ref[0, :, sl]) / jnp.sum(e, axis=-1, keepdims=True))
        yield
    o = jnp.concatenate(heads, axis=-1).astype(BF16)
    for c in range(X_HEADS):
        sl = slice(c * X_HEAD_DIM, (c + 1) * X_HEAD_DIM)
        o_ref[0, :, sl] = x[:, sl] + _dot(o, wo_ref[:, sl])
        yield


def _store_tiles(ref, val):
    for j in range(ref.shape[0]):
        ref[j] = val[:, j * LANES:(j + 1) * LANES]


def _load_tiles(ref):
    return jnp.concatenate([ref[j] for j in range(ref.shape[0])], axis=1)


def _pack_bf16_pairs(v):
    bits = lax.bitcast_convert_type(v.astype(BF16).astype(F32), U32)
    half = bits.shape[1] // 2
    return (bits[:, :half] & jnp.uint32(HI16)) | (bits[:, half:] >> 16)


def _unpack_bf16_pairs(w):
    hi = lax.bitcast_convert_type(w & jnp.uint32(HI16), F32)
    lo = lax.bitcast_convert_type(w << 16, F32)
    return jnp.concatenate([hi, lo], axis=1)


def _tiled_shape(rows, width, dtype):
    return jax.ShapeDtypeStruct((width // LANES, rows, LANES), dtype)


def _tiled_spec(rows, width, row_block, lead=None):
    block = (width // LANES, rows, LANES)
    if lead is None:
        return pl.BlockSpec(block, lambda *a: (0, row_block(*a), 0))
    return pl.BlockSpec((1,) + block, lambda *a: (lead, 0, row_block(*a), 0))


def _piece_index(rows, k, n_rows):
    return jnp.arange(k, dtype=I32)[:, None] * n_rows + rows[..., None, :]


META_I1, META_I2, META_G1, META_G2, META_R1, META_R2 = range(6)


def _meta_col(meta, col):
    lane = lax.broadcasted_iota(I32, meta.shape, 1)
    return jnp.sum(jnp.where(lane == col, meta, 0.0), axis=-1, keepdims=True)


def _router_kernel(x_ref, g_ref, rw_ref, tril_ref, meta_ref, metat_ref, hp_ref, cnt_ref, carry):
    @pl.when(pl.program_id(0) == 0)
    def _():
        carry[...] = jnp.zeros_like(carry)

    h = _rms(x_ref[...], g_ref[...])
    h_hi = h.astype(BF16)
    h_lo = (h - h_hi.astype(F32)).astype(BF16)
    logits = _dot(jnp.concatenate([h_hi, h_hi, h_lo], axis=1), rw_ref[...])
    lane = lax.broadcasted_iota(I32, logits.shape, 1)
    neg = jnp.float32(-jnp.inf)
    lg = jnp.where(lane < N_EXPERTS, logits, neg)
    m1 = jnp.max(lg, axis=-1, keepdims=True)
    i1 = jnp.min(jnp.where(lg == m1, lane, LANES), axis=-1, keepdims=True)
    lg2 = jnp.where(lane == i1, neg, lg)
    m2 = jnp.max(lg2, axis=-1, keepdims=True)
    i2 = jnp.min(jnp.where(lg2 == m2, lane, LANES), axis=-1, keepdims=True)
    d = jnp.exp(m2 - m1)
    g1 = 1.0 / (1.0 + d)
    g2 = d / (1.0 + d)

    sel1 = lane == i1
    sel2 = lane == i2
    onehot = jnp.where(jnp.logical_or(sel1, sel2), 1.0, 0.0)
    before = _dot(tril_ref[...], onehot.astype(BF16)) + carry[...]
    r1 = jnp.sum(jnp.where(sel1, before, 0.0), axis=-1, keepdims=True)
    r2 = jnp.sum(jnp.where(sel2, before, 0.0), axis=-1, keepdims=True)
    carry[...] += jnp.sum(onehot, axis=0, keepdims=True)
    cnt_ref[...] = carry[...]

    meta = jnp.zeros_like(logits)
    for c, val in ((META_I1, i1.astype(F32)), (META_I2, i2.astype(F32)), (META_G1, g1), (META_G2, g2),
                   (META_R1, r1), (META_R2, r2)):
        meta = jnp.where(lane == c, val, meta)
    meta_ref[...] = meta

    _store_tiles(hp_ref, _pack_bf16_pairs(h))
    metat_ref[...] = meta.T[:SUBLANES, :]


def _router(x2, g, router_w):
    T, D = x2.shape
    tm = FFN_TM
    rw = jnp.zeros((D, LANES), F32).at[:, :N_EXPERTS].set(router_w)
    rw_hi = rw.astype(BF16)
    rw_lo = (rw - rw_hi.astype(F32)).astype(BF16)
    rw_split = jnp.concatenate([rw_hi, rw_lo, rw_hi], axis=0)
    tril = jnp.tril(jnp.ones((tm, tm), BF16), -1)
    return pl.pallas_call(
        _router_kernel,
        grid=(T // tm,),
        in_specs=[pl.BlockSpec((tm, D), lambda i: (i, 0)), _const_spec((1, D)), _const_spec((3 * D, LANES)),
                  _const_spec((tm, tm))],
        out_specs=[pl.BlockSpec((tm, LANES), lambda i: (i, 0)), pl.BlockSpec((SUBLANES, tm), lambda i: (0, i)),
                   _tiled_spec(tm, D // 2, lambda i: i), _const_spec((1, LANES))],
        out_shape=[jax.ShapeDtypeStruct((T, LANES), F32), jax.ShapeDtypeStruct((SUBLANES, T), F32),
                   _tiled_shape(T, D // 2, U32), jax.ShapeDtypeStruct((1, LANES), F32)],
        scratch_shapes=[pltpu.VMEM((1, LANES), F32)],
        compiler_params=pltpu.CompilerParams(
            dimension_semantics=("arbitrary",), vmem_limit_bytes=VMEM_LIMIT),
        name="router",
    )(x2, g.reshape(1, -1), rw_split, tril)


def _sc_mesh():
    return plsc.VectorSubcoreMesh(core_axis_name="core", subcore_axis_name="subcore")


def _sc_dispatch(tiled, dest1, dest2, n_out):
    k, n, _ = tiled.shape
    d1, d2 = (_piece_index(d, k, n_out).reshape(-1) for d in (dest1, dest2))
    return _sc_scatter_pieces(tiled.reshape(k * n, LANES), d1, d2, k * n_out).reshape(k, n_out, LANES)


def _sc_gather(tiled, idx):
    k, n_rows, _ = tiled.shape
    G, n = idx.shape
    out = _sc_gather_pieces(tiled.reshape(k * n_rows, LANES), _piece_index(idx, k, n_rows).reshape(-1))
    return out.reshape(G, k, n, LANES)


def _sc_scatter_pieces(rows, dest1, dest2, n_out):
    T, W = rows.shape
    win = SC_WIN

    @pl.kernel(out_type=jax.ShapeDtypeStruct((n_out, W), rows.dtype), mesh=_sc_mesh(), scratch_types=[])
    def scatter_kernel(x_hbm, i1_hbm, i2_hbm, o_hbm):
        def body(x_vmem, i1_vmem, i2_vmem):
            pltpu.sync_copy(x_vmem, o_hbm.at[i1_vmem.at[0]])
            pltpu.sync_copy(x_vmem, o_hbm.at[i2_vmem.at[0]])

        pltpu.emit_pipeline(
            body,
            grid=(T // win,),
            in_specs=[pl.BlockSpec((win, W), lambda i: (i, 0)),
                      pl.BlockSpec((1, win), lambda i: (0, i)),
                      pl.BlockSpec((1, win), lambda i: (0, i))],
            out_specs=[],
            core_axis_name=("core", "subcore"),
            dimension_semantics=(pltpu.PARALLEL,),
        )(x_hbm, i1_hbm, i2_hbm)

    return scatter_kernel(rows, dest1.reshape(1, T), dest2.reshape(1, T))


def _sc_gather_pieces(rows, idx):
    n = idx.shape[0]
    W = rows.shape[1]
    win = SC_WIN

    @pl.kernel(out_type=jax.ShapeDtypeStruct((n, W), rows.dtype), mesh=_sc_mesh(), scratch_types=[])
    def gather_kernel(x_hbm, i_hbm, o_hbm):
        def body(i_vmem, o_vmem):
            pltpu.sync_copy(x_hbm.at[i_vmem.at[0]], o_vmem)

        pltpu.emit_pipeline(
            body,
            grid=(n // win,),
            in_specs=[pl.BlockSpec((1, win), lambda i: (0, i))],
            out_specs=[pl.BlockSpec((win, W), lambda i: (i, 0))],
            core_axis_name=("core", "subcore"),
            dimension_semantics=(pltpu.PARALLEL,),
        )(i_hbm, o_hbm)

    return gather_kernel(rows, idx.reshape(1, n))


def _swiglu(h, wg, wu, wd):
    gate = _dot(h, wg)
    up = _dot(h, wu)
    return _dot((gate * jax.nn.sigmoid(gate) * up).astype(BF16), wd)


def _swiglu_chunked(h, wg_ref, wu_ref, wd_ref, acc=None):
    F = wg_ref.shape[1]
    n_chunks = -(-F // FFN_FC)
    cols = -(-F // (n_chunks * MXU_WIDTH)) * MXU_WIDTH
    for lo in range(0, F, cols):
        sl = slice(lo, min(lo + cols, F))
        y = _swiglu(h, wg_ref[:, sl], wu_ref[:, sl], wd_ref[sl, :])
        acc = y if acc is None else acc + y
    return acc


def _ffn_kernel(x_ref, g_ref, wg_ref, wu_ref, wd_ref, o_ref):
    x = x_ref[...]
    h = _rms(x, g_ref[...]).astype(BF16)
    o_ref[...] = _swiglu_chunked(h, wg_ref, wu_ref, wd_ref, acc=x)


def _ffn(x2, g, wg, wu, wd):
    T, D = x2.shape
    F = wg.shape[1]
    tm = FFN_TM
    return pl.pallas_call(
        _ffn_kernel,
        grid=(T // tm,),
        in_specs=[pl.BlockSpec((tm, D), lambda i: (i, 0)), _const_spec((1, D)),
                  _const_spec((D, F)), _const_spec((D, F)), _const_spec((F, D))],
        out_specs=pl.BlockSpec((tm, D), lambda i: (i, 0)),
        out_shape=jax.ShapeDtypeStruct((T, D), F32),
        compiler_params=pltpu.CompilerParams(
            dimension_semantics=("arbitrary",), vmem_limit_bytes=VMEM_LIMIT),
        name="swiglu",
    )(x2, g.reshape(1, -1), wg, wu, wd)


def _moe_kernel(tidx_ref, texp_ref, nvalid_ref, xs_ref, wg_ref, wu_ref, wd_ref, y_ref):
    del tidx_ref, texp_ref

    @pl.when(pl.program_id(0) < nvalid_ref[0])
    def _():
        h = _unpack_bf16_pairs(_load_tiles(xs_ref)).astype(BF16)
        _store_tiles(y_ref, _pack_bf16_pairs(_swiglu_chunked(h, wg_ref.at[0], wu_ref.at[0], wd_ref.at[0])))


def _moe_experts(xs, tile_idx, tile_expert, n_valid, wg, wu, wd):
    E, D, F = wg.shape
    tm = MOE_TM
    P = xs.shape[1]
    half = xs.shape[0] * LANES
    row_block = lambda g, ti, te, nv: ti[g]
    w_map = lambda g, ti, te, nv: (te[g], 0, 0)
    return pl.pallas_call(
        _moe_kernel,
        grid_spec=pltpu.PrefetchScalarGridSpec(
            num_scalar_prefetch=3,
            grid=(P // tm,),
            in_specs=[_tiled_spec(tm, half, row_block),
                      pl.BlockSpec((1, D, F), w_map), pl.BlockSpec((1, D, F), w_map),
                      pl.BlockSpec((1, F, D), w_map)],
            out_specs=_tiled_spec(tm, D // 2, row_block)),
        out_shape=_tiled_shape(P, D // 2, U32),
        compiler_params=pltpu.CompilerParams(
            dimension_semantics=("arbitrary",), vmem_limit_bytes=VMEM_LIMIT),
        name="moe_experts",
    )(tile_idx, tile_expert, n_valid, xs, wg, wu, wd)


def _combine_kernel(x_ref, y1_ref, y2_ref, meta_ref, fg_ref, o_ref, *, final_norm):
    meta = meta_ref[...]
    out = x_ref[...] + _meta_col(meta, META_G1) * _unpack_bf16_pairs(_load_tiles(y1_ref.at[0]))
    out = out + _meta_col(meta, META_G2) * _unpack_bf16_pairs(_load_tiles(y2_ref.at[0]))
    if final_norm:
        out = _rms(out, fg_ref[...])
    o_ref[...] = out


def _combine(x2, y12, meta, final_g, *, final_norm):
    T, D = x2.shape
    tm = FFN_TM
    return pl.pallas_call(
        functools.partial(_combine_kernel, final_norm=final_norm),
        grid=(T // tm,),
        in_specs=[pl.BlockSpec((tm, D), lambda i: (i, 0)),
                  _tiled_spec(tm, D // 2, lambda i: i, lead=0), _tiled_spec(tm, D // 2, lambda i: i, lead=1),
                  pl.BlockSpec((tm, LANES), lambda i: (i, 0)), _const_spec((1, D))],
        out_specs=pl.BlockSpec((tm, D), lambda i: (i, 0)),
        out_shape=jax.ShapeDtypeStruct((T, D), F32),
        compiler_params=pltpu.CompilerParams(
            dimension_semantics=("arbitrary",), vmem_limit_bytes=VMEM_LIMIT),
        name="moe_combine",
    )(x2, y12, y12, meta, final_g.reshape(1, -1))


def _moe_layer(x2, g, router_w, wg, wu, wd, final_g, *, final_norm):
    T, D = x2.shape
    tm = MOE_TM
    n_tiles = (T * TOP_K) // tm + N_EXPERTS
    meta, meta_t, hp, counts = _router(x2, g, router_w)

    cnt = counts[0, :N_EXPERTS].astype(I32)
    tiles_e = (cnt + tm - 1) // tm
    tile_end = jnp.cumsum(tiles_e)
    row_start = (tile_end - tiles_e) * tm
    experts = jnp.arange(N_EXPERTS, dtype=I32)[:, None]

    def dest(i_row, r_row):
        start = jnp.sum(jnp.where(meta_t[i_row].astype(I32) == experts, row_start[:, None], 0), axis=0)
        return start + meta_t[r_row].astype(I32)

    dest1 = dest(META_I1, META_R1)
    dest2 = dest(META_I2, META_R2)
    n_valid = tile_end[-1:]
    tile_idx = jnp.minimum(jnp.arange(n_tiles, dtype=I32), n_valid - 1)
    tile_expert = jnp.minimum(jnp.sum(tile_end[:, None] <= tile_idx, axis=0), N_EXPERTS - 1).astype(I32)

    xs = _sc_dispatch(hp, dest1, dest2, n_tiles * tm)
    ys = _moe_experts(xs, tile_idx, tile_expert, n_valid.astype(I32), wg, wu, wd)
    y12 = _sc_gather(ys, jnp.stack([dest1, dest2]))
    return _combine(x2, y12, meta, final_g, final_norm=final_norm)


def kernel(x, mem, mix_norm_g, w_in, conv_w, conv_b, conv_ln_g, conv_ln_b, sgu_ln_g, sgu_ln_b, sgu_w, sgu_b,
           pool_w, pool_b, pool_scale, w_out, xattn_norm_g, mem_norm_g, xattn_wq, xattn_wk, xattn_wv, xattn_wo,
           ffn_norm_g, ffn_wg, ffn_wu, ffn_wd, router_w, moe_wg, moe_wu, moe_wd, final_norm_g):
    B, S, D = x.shape
    bf = lambda a: a.astype(BF16)
    for l in range(DEPTH):
        sgu_bias = jnp.repeat(sgu_b[l].T, D_SGU // SGU_HEADS, axis=1)
        pool_wbd = jax.scipy.linalg.block_diag(*[pool_w[l, gi] for gi in range(len(POOL_WINDOWS))])
        mix_params = (mix_norm_g[l], bf(w_in[l]), conv_w[l], conv_b[l], conv_ln_g[l], conv_ln_b[l],
                      sgu_ln_g[l], sgu_ln_b[l], sgu_w[l], sgu_bias, bf(pool_wbd), pool_b[l].reshape(-1),
                      pool_scale[l], bf(w_out[l]))
        k, v = _kv(mem, mem_norm_g[l], bf(xattn_wk[l]), bf(xattn_wv[l]))
        x = _mix_attn(x, mix_params, k, v, (xattn_norm_g[l], bf(xattn_wq[l]), bf(xattn_wo[l])))
        x2 = x.reshape(B * S, D)
        j = l // 2
        if l % 2 == 0:
            assert l != DEPTH - 1, "the final RMSNorm is fused into the routed layer's combine kernel"
            x2 = _ffn(x2, ffn_norm_g[l], bf(ffn_wg[j]), bf(ffn_wu[j]), bf(ffn_wd[j]))
        else:
            x2 = _moe_layer(x2, ffn_norm_g[l], router_w[j], bf(moe_wg[j]), bf(moe_wu[j]), bf(moe_wd[j]),
                            final_norm_g, final_norm=l == DEPTH - 1)
        x = x2.reshape(B, S, D)
    return x
```

```python
import functools

import jax
import jax.numpy as jnp
from jax import lax
from jax.experimental import pallas as pl
from jax.experimental.pallas import tpu as pltpu
from jax.experimental.pallas import tpu_sc as plsc

F32 = jnp.float32
BF16 = jnp.bfloat16
U32 = jnp.uint32
I32 = jnp.int32

D_MODEL = 1024
DEPTH = 2
CHUNK = 64
D_CONV = 384
CONV_WIDTH = 31
D_SGU = 384
SGU_HEADS = 4
SGU_CHUNK = 128
D_POOL = 256
POOL_WINDOWS = (2, 4, 8, 16)
POOL_GROUP_DIM = D_POOL // len(POOL_WINDOWS)
D_MIX = D_CONV + D_SGU + D_POOL
D_IN = 2 * D_CONV + 2 * D_SGU + D_POOL
X_HEADS = 4
X_HEAD_DIM = D_MODEL // X_HEADS
N_EXPERTS = 8
TOP_K = 2
EPS = 1e-6

LANES = 128
SUBLANES = 8
HIST = 32
MIX_TS = 512
CONV_RB = 64
FFN_TM = 512
FFN_FC = 1408
MXU_WIDTH = 256
MOE_TM = 512
SC_WIN = 128
HI16 = 0xFFFF0000
VMEM_LIMIT = 56 * 1024 * 1024


def _rms(x, g):
    return x * lax.rsqrt(jnp.mean(x * x, axis=-1, keepdims=True) + EPS) * g


def _layer_norm(x, g, b):
    mu = jnp.mean(x, axis=-1, keepdims=True)
    xc = x - mu
    var = jnp.mean(xc * xc, axis=-1, keepdims=True)
    return xc * lax.rsqrt(var + EPS) * g + b


def _dot(a, b):
    return jnp.dot(a, b, preferred_element_type=F32)


def _round_robin(*generators):
    live = list(generators)
    while live:
        for gen in list(live):
            try:
                next(gen)
                yield
            except StopIteration:
                live.remove(gen)


def _mixer_tile(x, s, g_ref, win_ref, convw_ref, convb_ref, clng_ref, clnb_ref,
                slng_ref, slnb_ref, sguw_ref, sgub_ref, poolw_ref, poolb_ref, pscale_ref,
                wout_ref, cbuf, cshift, pb0, pb1, pb2, pb3, overlap):
    ts = MIX_TS
    a_end = 2 * D_CONV
    b_end = a_end + 2 * D_SGU
    h = _rms(x, g_ref[...]).astype(BF16)
    z_a = _dot(h, win_ref[:, :a_end])
    z_rest = []

    def project_rest():
        for lo in range(a_end, D_IN, MXU_WIDTH):
            z_rest.append(_dot(h, win_ref[:, lo:lo + MXU_WIDTH]))
            yield

    pieces = _round_robin(project_rest(), overlap)

    cbuf[HIST:HIST + ts, :] = z_a[:, :D_CONV] * jax.nn.sigmoid(z_a[:, D_CONV:])
    span = ts + HIST - SUBLANES
    for r in range(1, SUBLANES):
        cshift[r - 1, 0:span, :] = cbuf[r:r + span, :]
    ya_blocks = []
    n_blocks = ts // CONV_RB
    for rb in range(n_blocks):
        acc = jnp.zeros((CONV_RB, D_CONV), F32)
        for k in range(CONV_WIDTH):
            start = rb * CONV_RB + HIST - (CONV_WIDTH - 1) + k
            r, base = start % SUBLANES, start - start % SUBLANES
            src = cbuf[base:base + CONV_RB, :] if r == 0 else cshift[r - 1, base:base + CONV_RB, :]
            acc = acc + convw_ref[k:k + 1, :] * src
        ya_blocks.append(acc)
        for _ in range(2 if rb < n_blocks // 2 else 1):
            next(pieces, None)
    for _ in pieces:
        pass
    ya = jnp.concatenate(ya_blocks, axis=0) + convb_ref[...]
    ya = _layer_norm(ya, clng_ref[...], clnb_ref[...])
    ya = ya * jax.nn.sigmoid(ya)
    cbuf[0:HIST, :] = cbuf[ts:ts + HIST, :]
    out = x + _dot(ya.astype(BF16), wout_ref[0:D_CONV, :])

    z = jnp.concatenate(z_rest, axis=1)
    zb = jax.nn.gelu(z[:, :b_end - a_end])
    u = zb[:, :D_SGU]
    v = _layer_norm(zb[:, D_SGU:], slng_ref[...], slnb_ref[...]).astype(BF16)
    blk_r = lax.broadcasted_iota(jnp.int32, (SGU_CHUNK, SGU_CHUNK), 0) // CHUNK
    blk_c = lax.broadcasted_iota(jnp.int32, (SGU_CHUNK, SGU_CHUNK), 1) // CHUNK
    head_of_lane = lax.broadcasted_iota(jnp.int32, (SGU_CHUNK, D_SGU), 1) // (D_SGU // SGU_HEADS)
    w_heads = [jnp.where(blk_r >= blk_c, sguw_ref[hd], 0.0).astype(BF16) for hd in range(SGU_HEADS)]
    s_chunks = []
    for c in range(ts // SGU_CHUNK):
        vc = v[c * SGU_CHUNK:(c + 1) * SGU_CHUNK, :]
        sc = jnp.zeros((SGU_CHUNK, D_SGU), F32)
        for hd in range(SGU_HEADS):
            sc = jnp.where(head_of_lane == hd, _dot(w_heads[hd], vc), sc)
        s_chunks.append(sc + sgub_ref[...])
    yb = u * jnp.concatenate(s_chunks, axis=0)
    out = out + _dot(yb.astype(BF16), wout_ref[D_CONV:D_CONV + D_SGU, :])

    cc = z[:, b_end - a_end:]
    pb0[HIST:HIST + ts, :] = cc
    pb1[0:ts + 24, :] = pb0[8:ts + 32, :] + pb0[7:ts + 31, :]
    pb2[0:ts + 16, :] = pb1[8:ts + 24, :] + pb1[6:ts + 22, :]
    pb3[0:ts + 8, :] = pb2[8:ts + 16, :] + pb2[4:ts + 12, :]
    s16 = pb3[8:ts + 8, :] + pb3[0:ts, :]
    s8 = pb3[8:ts + 8, :]
    s4 = pb2[16:ts + 16, :]
    s2 = pb1[24:ts + 24, :]
    grp = lax.broadcasted_iota(jnp.int32, (ts, D_POOL), 1) // POOL_GROUP_DIM
    wsum = jnp.where(grp == 0, s2, jnp.where(grp == 1, s4, jnp.where(grp == 2, s8, s16)))
    win = jnp.where(grp == 0, 2, jnp.where(grp == 1, 4, jnp.where(grp == 2, 8, 16)))
    pos = s * ts + lax.broadcasted_iota(jnp.int32, (ts, D_POOL), 0)
    cnt = jnp.minimum(pos + 1, win).astype(F32)
    p = (wsum / cnt - cc).astype(BF16)
    yc = (_dot(p, poolw_ref[...]) + poolb_ref[...]) * pscale_ref[...]
    pb0[0:HIST, :] = pb0[ts:ts + HIST, :]

    return out + _dot(yc.astype(BF16), wout_ref[D_CONV + D_SGU:D_MIX, :])


def _const_spec(shape):
    zeros = (0,) * len(shape)
    return pl.BlockSpec(shape, lambda *_: zeros)


N_MIX_PARAMS = 14
N_ATT_PARAMS = 5


def _mix_attn_kernel(x_ref, *refs, n_tiles, tiles_per_row):
    mix_refs = refs[:N_MIX_PARAMS]
    att_refs = refs[N_MIX_PARAMS:N_MIX_PARAMS + N_ATT_PARAMS]
    o_ref, mid, cbuf, cshift, pb0, pb1, pb2, pb3 = refs[N_MIX_PARAMS + N_ATT_PARAMS:]
    i = pl.program_id(0)
    s = jnp.minimum(i, n_tiles - 1) % tiles_per_row

    @pl.when(i == 0)
    def _():
        mid[...] = jnp.zeros_like(mid)

    @pl.when(s == 0)
    def _():
        cbuf[0:HIST, :] = jnp.zeros((HIST, D_CONV), F32)
        pb0[0:HIST, :] = jnp.zeros((HIST, D_POOL), F32)

    attention = _xattn_pieces(mid[...], o_ref, *att_refs)
    mid[...] = _mixer_tile(x_ref[0], s, *mix_refs, cbuf, cshift, pb0, pb1, pb2, pb3, overlap=attention)


def _mix_attn(x, mix_params, k, v, att_params):
    B, S, D = x.shape
    M = k.shape[1]
    ts = MIX_TS
    tiles_per_row = S // ts
    n_tiles = B * tiles_per_row
    g_att, wq, wo = att_params
    row = lambda a: a.reshape(1, -1) if a.ndim == 1 else a
    mix_args = [row(a) for a in mix_params]
    att_args = [row(g_att), wq, k, v, wo]
    assert len(mix_args) == N_MIX_PARAMS and len(att_args) == N_ATT_PARAMS

    def mix_tile(i):
        t = jnp.minimum(i, n_tiles - 1)
        return t // tiles_per_row, t % tiles_per_row, 0

    def att_tile(i):
        t = jnp.maximum(i - 1, 0)
        return t // tiles_per_row, t % tiles_per_row, 0

    single = lambda a: pl.BlockSpec(a.shape, lambda i, nd=a.ndim: (0,) * nd, pipeline_mode=pl.Buffered(1))
    kv_spec = pl.BlockSpec((1, M, D), lambda i: (att_tile(i)[0], 0, 0))
    in_specs = [pl.BlockSpec((1, ts, D), mix_tile)] + [single(a) for a in mix_args]
    in_specs += [single(att_args[0]), single(wq), kv_spec, kv_spec, single(wo)]
    return pl.pallas_call(
        functools.partial(_mix_attn_kernel, n_tiles=n_tiles, tiles_per_row=tiles_per_row),
        grid=(n_tiles + 1,),
        in_specs=in_specs,
        out_specs=pl.BlockSpec((1, ts, D), att_tile),
        out_shape=jax.ShapeDtypeStruct((B, S, D), F32),
        scratch_shapes=[pltpu.VMEM((ts, D), F32),
                        pltpu.VMEM((ts + HIST, D_CONV), F32),
                        pltpu.VMEM((SUBLANES - 1, ts + HIST - SUBLANES, D_CONV), F32)]
                       + [pltpu.VMEM((ts + HIST, D_POOL), F32)] * 4,
        compiler_params=pltpu.CompilerParams(
            dimension_semantics=("arbitrary",), vmem_limit_bytes=VMEM_LIMIT),
        name="mix_attn",
    )(x, *mix_args, *att_args)


def _kv_kernel(mem_ref, g_ref, wk_ref, wv_ref, k_ref, v_ref):
    m = _rms(mem_ref[0], g_ref[...]).astype(BF16)
    k_ref[0] = _dot(m, wk_ref[...]).astype(BF16)
    v_ref[0] = _dot(m, wv_ref[...]).astype(BF16)


def _kv(mem, g, wk, wv):
    B, M, D = mem.shape
    return pl.pallas_call(
        _kv_kernel,
        grid=(B,),
        in_specs=[pl.BlockSpec((1, M, D), lambda b: (b, 0, 0)), _const_spec((1, D)),
                  _const_spec((D, D)), _const_spec((D, D))],
        out_specs=[pl.BlockSpec((1, M, D), lambda b: (b, 0, 0))] * 2,
        out_shape=[jax.ShapeDtypeStruct((B, M, D), BF16)] * 2,
        compiler_params=pltpu.CompilerParams(
            dimension_semantics=("arbitrary",), vmem_limit_bytes=VMEM_LIMIT),
        name="mem_kv",
    )(mem, g.reshape(1, -1), wk, wv)


def _xattn_pieces(x, o_ref, g_ref, wq_ref, k_ref, v_ref, wo_ref):
    h = _rms(x, g_ref[...]).astype(BF16)
    heads = []
    for hd in range(X_HEADS):
        sl = slice(hd * X_HEAD_DIM, (hd + 1) * X_HEAD_DIM)
        q = _dot(h, wq_ref[:, sl]).astype(BF16)
        sc = lax.dot_general(q, k_ref[0, :, sl], (((1,), (1,)), ((), ())),
                             preferred_element_type=F32) * (X_HEAD_DIM ** -0.5)
        e = jnp.exp(sc - jnp.max(sc, axis=-1, keepdims=True))
        heads.append(_dot(e.astype(BF16), v_ref[0, :, sl]) / jnp.sum(e, axis=-1, keepdims=True))
        yield
    o = jnp.concatenate(heads, axis=-1).astype(BF16)
    for c in range(X_HEADS):
        sl = slice(c * X_HEAD_DIM, (c + 1) * X_HEAD_DIM)
        o_ref[0, :, sl] = x[:, sl] + _dot(o, wo_ref[:, sl])
        yield


def _store_tiles(ref, val):
    for j in range(ref.shape[0]):
        ref[j] = val[:, j * LANES:(j + 1) * LANES]


def _load_tiles(ref):
    return jnp.concatenate([ref[j] for j in range(ref.shape[0])], axis=1)


def _pack_bf16_pairs(v):
    bits = lax.bitcast_convert_type(v.astype(BF16).astype(F32), U32)
    half = bits.shape[1] // 2
    return (bits[:, :half] & jnp.uint32(HI16)) | (bits[:, half:] >> 16)


def _unpack_bf16_pairs(w):
    hi = lax.bitcast_convert_type(w & jnp.uint32(HI16), F32)
    lo = lax.bitcast_convert_type(w << 16, F32)
    return jnp.concatenate([hi, lo], axis=1)


def _tiled_shape(rows, width, dtype):
    return jax.ShapeDtypeStruct((width // LANES, rows, LANES), dtype)


def _tiled_spec(rows, width, row_block, lead=None):
    block = (width // LANES, rows, LANES)
    if lead is None:
        return pl.BlockSpec(block, lambda *a: (0, row_block(*a), 0))
    return pl.BlockSpec((1,) + block, lambda *a: (lead, 0, row_block(*a), 0))


def _piece_index(rows, k, n_rows):
    return jnp.arange(k, dtype=I32)[:, None] * n_rows + rows[..., None, :]


META_I1, META_I2, META_G1, META_G2, META_R1, META_R2 = range(6)


def _meta_col(meta, col):
    lane = lax.broadcasted_iota(I32, meta.shape, 1)
    return jnp.sum(jnp.where(lane == col, meta, 0.0), axis=-1, keepdims=True)


def _router_kernel(x_ref, g_ref, rw_ref, tril_ref, meta_ref, metat_ref, hp_ref, cnt_ref, carry):
    @pl.when(pl.program_id(0) == 0)
    def _():
        carry[...] = jnp.zeros_like(carry)

    h = _rms(x_ref[...], g_ref[...])
    h_hi = h.astype(BF16)
    h_lo = (h - h_hi.astype(F32)).astype(BF16)
    logits = _dot(jnp.concatenate([h_hi, h_hi, h_lo], axis=1), rw_ref[...])
    lane = lax.broadcasted_iota(I32, logits.shape, 1)
    neg = jnp.float32(-jnp.inf)
    lg = jnp.where(lane < N_EXPERTS, logits, neg)
    m1 = jnp.max(lg, axis=-1, keepdims=True)
    i1 = jnp.min(jnp.where(lg == m1, lane, LANES), axis=-1, keepdims=True)
    lg2 = jnp.where(lane == i1, neg, lg)
    m2 = jnp.max(lg2, axis=-1, keepdims=True)
    i2 = jnp.min(jnp.where(lg2 == m2, lane, LANES), axis=-1, keepdims=True)
    d = jnp.exp(m2 - m1)
    g1 = 1.0 / (1.0 + d)
    g2 = d / (1.0 + d)

    sel1 = lane == i1
    sel2 = lane == i2
    onehot = jnp.where(jnp.logical_or(sel1, sel2), 1.0, 0.0)
    before = _dot(tril_ref[...], onehot.astype(BF16)) + carry[...]
    r1 = jnp.sum(jnp.where(sel1, before, 0.0), axis=-1, keepdims=True)
    r2 = jnp.sum(jnp.where(sel2, before, 0.0), axis=-1, keepdims=True)
    carry[...] += jnp.sum(onehot, axis=0, keepdims=True)
    cnt_ref[...] = carry[...]

    meta = jnp.zeros_like(logits)
    for c, val in ((META_I1, i1.astype(F32)), (META_I2, i2.astype(F32)), (META_G1, g1), (META_G2, g2),
                   (META_R1, r1), (META_R2, r2)):
        meta = jnp.where(lane == c, val, meta)
    meta_ref[...] = meta

    _store_tiles(hp_ref, _pack_bf16_pairs(h))
    metat_ref[...] = meta.T[:SUBLANES, :]


def _router(x2, g, router_w):
    T, D = x2.shape
    tm = FFN_TM
    rw = jnp.zeros((D, LANES), F32).at[:, :N_EXPERTS].set(router_w)
    rw_hi = rw.astype(BF16)
    rw_lo = (rw - rw_hi.astype(F32)).astype(BF16)
    rw_split = jnp.concatenate([rw_hi, rw_lo, rw_hi], axis=0)
    tril = jnp.tril(jnp.ones((tm, tm), BF16), -1)
    return pl.pallas_call(
        _router_kernel,
        grid=(T // tm,),
        in_specs=[pl.BlockSpec((tm, D), lambda i: (i, 0)), _const_spec((1, D)), _const_spec((3 * D, LANES)),
                  _const_spec((tm, tm))],
        out_specs=[pl.BlockSpec((tm, LANES), lambda i: (i, 0)), pl.BlockSpec((SUBLANES, tm), lambda i: (0, i)),
                   _tiled_spec(tm, D // 2, lambda i: i), _const_spec((1, LANES))],
        out_shape=[jax.ShapeDtypeStruct((T, LANES), F32), jax.ShapeDtypeStruct((SUBLANES, T), F32),
                   _tiled_shape(T, D // 2, U32), jax.ShapeDtypeStruct((1, LANES), F32)],
        scratch_shapes=[pltpu.VMEM((1, LANES), F32)],
        compiler_params=pltpu.CompilerParams(
            dimension_semantics=("arbitrary",), vmem_limit_bytes=VMEM_LIMIT),
        name="router",
    )(x2, g.reshape(1, -1), rw_split, tril)


def _sc_mesh():
    return plsc.VectorSubcoreMesh(core_axis_name="core", subcore_axis_name="subcore")


def _sc_dispatch(tiled, dest1, dest2, n_out):
    k, n, _ = tiled.shape
    d1, d2 = (_piece_index(d, k, n_out).reshape(-1) for d in (dest1, dest2))
    return _sc_scatter_pieces(tiled.reshape(k * n, LANES), d1, d2, k * n_out).reshape(k, n_out, LANES)


def _sc_gather(tiled, idx):
    k, n_rows, _ = tiled.shape
    G, n = idx.shape
    out = _sc_gather_pieces(tiled.reshape(k * n_rows, LANES), _piece_index(idx, k, n_rows).reshape(-1))
    return out.reshape(G, k, n, LANES)


def _sc_scatter_pieces(rows, dest1, dest2, n_out):
    T, W = rows.shape
    win = SC_WIN

    @pl.kernel(out_type=jax.ShapeDtypeStruct((n_out, W), rows.dtype), mesh=_sc_mesh(), scratch_types=[])
    def scatter_kernel(x_hbm, i1_hbm, i2_hbm, o_hbm):
        def body(x_vmem, i1_vmem, i2_vmem):
            pltpu.sync_copy(x_vmem, o_hbm.at[i1_vmem.at[0]])
            pltpu.sync_copy(x_vmem, o_hbm.at[i2_vmem.at[0]])

        pltpu.emit_pipeline(
            body,
            grid=(T // win,),
            in_specs=[pl.BlockSpec((win, W), lambda i: (i, 0)),
                      pl.BlockSpec((1, win), lambda i: (0, i)),
                      pl.BlockSpec((1, win), lambda i: (0, i))],
            out_specs=[],
            core_axis_name=("core", "subcore"),
            dimension_semantics=(pltpu.PARALLEL,),
        )(x_hbm, i1_hbm, i2_hbm)

    return scatter_kernel(rows, dest1.reshape(1, T), dest2.reshape(1, T))


def _sc_gather_pieces(rows, idx):
    n = idx.shape[0]
    W = rows.shape[1]
    win = SC_WIN

    @pl.kernel(out_type=jax.ShapeDtypeStruct((n, W), rows.dtype), mesh=_sc_mesh(), scratch_types=[])
    def gather_kernel(x_hbm, i_hbm, o_hbm):
        def body(i_vmem, o_vmem):
            pltpu.sync_copy(x_hbm.at[i_vmem.at[0]], o_vmem)

        pltpu.emit_pipeline(
            body,
            grid=(n // win,),
            in_specs=[pl.BlockSpec((1, win), lambda i: (0, i))],
            out_specs=[pl.BlockSpec((win, W), lambda i: (i, 0))],
            core_axis_name=("core", "subcore"),
            dimension_semantics=(pltpu.PARALLEL,),
        )(i_hbm, o_hbm)

    return gather_kernel(rows, idx.reshape(1, n))


def _swiglu(h, wg, wu, wd):
    gate = _dot(h, wg)
    up = _dot(h, wu)
    return _dot((gate * jax.nn.sigmoid(gate) * up).astype(BF16), wd)


def _swiglu_chunked(h, wg_ref, wu_ref, wd_ref, acc=None):
    F = wg_ref.shape[1]
    n_chunks = -(-F // FFN_FC)
    cols = -(-F // (n_chunks * MXU_WIDTH)) * MXU_WIDTH
    for lo in range(0, F, cols):
        sl = slice(lo, min(lo + cols, F))
        y = _swiglu(h, wg_ref[:, sl], wu_ref[:, sl], wd_ref[sl, :])
        acc = y if acc is None else acc + y
    return acc


def _ffn_kernel(x_ref, g_ref, wg_ref, wu_ref, wd_ref, *refs):
    n_cast = len(refs) // 2
    o_ref = refs[n_cast]
    x = x_ref[...]
    h = _rms(x, g_ref[...]).astype(BF16)
    o_ref[...] = _swiglu_chunked(h, wg_ref, wu_ref, wd_ref, acc=x)
    for src_ref, dst_ref in zip(refs[:n_cast], refs[n_cast + 1:]):
        dst_ref[...] = src_ref[...].astype(BF16)


def _ffn(x2, g, wg, wu, wd, cast_weights=()):
    T, D = x2.shape
    F = wg.shape[1]
    tm = FFN_TM
    n_steps = T // tm
    single = lambda shape: pl.BlockSpec(shape, lambda i: (0,) * len(shape), pipeline_mode=pl.Buffered(1))
    cast_specs = []
    for w in cast_weights:
        E, rows, cols = w.shape
        per_expert = n_steps // E
        assert n_steps % E == 0 and rows % (per_expert * 2 * SUBLANES) == 0
        cast_specs.append(pl.BlockSpec((1, rows // per_expert, cols),
                                       lambda i, per_expert=per_expert: (i // per_expert, i % per_expert, 0)))
    out, *cast = pl.pallas_call(
        _ffn_kernel,
        grid=(n_steps,),
        in_specs=[pl.BlockSpec((tm, D), lambda i: (i, 0)), single((1, D)),
                  single((D, F)), single((D, F)), single((F, D))] + cast_specs,
        out_specs=[pl.BlockSpec((tm, D), lambda i: (i, 0))] + cast_specs,
        out_shape=[jax.ShapeDtypeStruct((T, D), F32)] + [jax.ShapeDtypeStruct(w.shape, BF16) for w in cast_weights],
        compiler_params=pltpu.CompilerParams(
            dimension_semantics=("arbitrary",), vmem_limit_bytes=VMEM_LIMIT),
        name="swiglu",
    )(x2, g.reshape(1, -1), wg, wu, wd, *cast_weights)
    return out, cast


def _moe_kernel(tidx_ref, texp_ref, nvalid_ref, xs_ref, wg_ref, wu_ref, wd_ref, y_ref):
    del tidx_ref, texp_ref

    @pl.when(pl.program_id(0) < nvalid_ref[0])
    def _():
        h = _unpack_bf16_pairs(_load_tiles(xs_ref)).astype(BF16)
        _store_tiles(y_ref, _pack_bf16_pairs(_swiglu_chunked(h, wg_ref.at[0], wu_ref.at[0], wd_ref.at[0])))


def _moe_experts(xs, tile_idx, tile_expert, n_valid, wg, wu, wd):
    E, D, F = wg.shape
    tm = MOE_TM
    P = xs.shape[1]
    half = xs.shape[0] * LANES
    row_block = lambda g, ti, te, nv: ti[g]
    w_map = lambda g, ti, te, nv: (te[g], 0, 0)
    return pl.pallas_call(
        _moe_kernel,
        grid_spec=pltpu.PrefetchScalarGridSpec(
            num_scalar_prefetch=3,
            grid=(P // tm,),
            in_specs=[_tiled_spec(tm, half, row_block),
                      pl.BlockSpec((1, D, F), w_map), pl.BlockSpec((1, D, F), w_map),
                      pl.BlockSpec((1, F, D), w_map)],
            out_specs=_tiled_spec(tm, D // 2, row_block)),
        out_shape=_tiled_shape(P, D // 2, U32),
        compiler_params=pltpu.CompilerParams(
            dimension_semantics=("arbitrary",), vmem_limit_bytes=VMEM_LIMIT),
        name="moe_experts",
    )(tile_idx, tile_expert, n_valid, xs, wg, wu, wd)


def _combine_kernel(x_ref, y1_ref, y2_ref, meta_ref, fg_ref, o_ref, *, final_norm):
    meta = meta_ref[...]
    out = x_ref[...] + _meta_col(meta, META_G1) * _unpack_bf16_pairs(_load_tiles(y1_ref.at[0]))
    out = out + _meta_col(meta, META_G2) * _unpack_bf16_pairs(_load_tiles(y2_ref.at[0]))
    if final_norm:
        out = _rms(out, fg_ref[...])
    o_ref[...] = out


def _combine(x2, y12, meta, final_g, *, final_norm):
    T, D = x2.shape
    tm = FFN_TM
    return pl.pallas_call(
        functools.partial(_combine_kernel, final_norm=final_norm),
        grid=(T // tm,),
        in_specs=[pl.BlockSpec((tm, D), lambda i: (i, 0)),
                  _tiled_spec(tm, D // 2, lambda i: i, lead=0), _tiled_spec(tm, D // 2, lambda i: i, lead=1),
                  pl.BlockSpec((tm, LANES), lambda i: (i, 0)), _const_spec((1, D))],
        out_specs=pl.BlockSpec((tm, D), lambda i: (i, 0)),
        out_shape=jax.ShapeDtypeStruct((T, D), F32),
        compiler_params=pltpu.CompilerParams(
            dimension_semantics=("arbitrary",), vmem_limit_bytes=VMEM_LIMIT),
        name="moe_combine",
    )(x2, y12, y12, meta, final_g.reshape(1, -1))


def _moe_layer(x2, g, router_w, wg, wu, wd, final_g, *, final_norm):
    T, D = x2.shape
    tm = MOE_TM
    n_tiles = (T * TOP_K) // tm + N_EXPERTS
    meta, meta_t, hp, counts = _router(x2, g, router_w)

    cnt = counts[0, :N_EXPERTS].astype(I32)
    tiles_e = (cnt + tm - 1) // tm
    tile_end = jnp.cumsum(tiles_e)
    row_start = (tile_end - tiles_e) * tm
    experts = jnp.arange(N_EXPERTS, dtype=I32)[:, None]

    def dest(i_row, r_row):
        start = jnp.sum(jnp.where(meta_t[i_row].astype(I32) == experts, row_start[:, None], 0), axis=0)
        return start + meta_t[r_row].astype(I32)

    dest1 = dest(META_I1, META_R1)
    dest2 = dest(META_I2, META_R2)
    n_valid = tile_end[-1:]
    tile_idx = jnp.minimum(jnp.arange(n_tiles, dtype=I32), n_valid - 1)
    tile_expert = jnp.minimum(jnp.sum(tile_end[:, None] <= tile_idx, axis=0), N_EXPERTS - 1).astype(I32)

    xs = _sc_dispatch(hp, dest1, dest2, n_tiles * tm)
    ys = _moe_experts(xs, tile_idx, tile_expert, n_valid.astype(I32), wg, wu, wd)
    y12 = _sc_gather(ys, jnp.stack([dest1, dest2]))
    return _combine(x2, y12, meta, final_g, final_norm=final_norm)


def kernel(x, mem, mix_norm_g, w_in, conv_w, conv_b, conv_ln_g, conv_ln_b, sgu_ln_g, sgu_ln_b, sgu_w, sgu_b,
           pool_w, pool_b, pool_scale, w_out, xattn_norm_g, mem_norm_g, xattn_wq, xattn_wk, xattn_wv, xattn_wo,
           ffn_norm_g, ffn_wg, ffn_wu, ffn_wd, router_w, moe_wg, moe_wu, moe_wd, final_norm_g):
    B, S, D = x.shape
    bf = lambda a: a.astype(BF16)
    for l in range(DEPTH):
        sgu_bias = jnp.repeat(sgu_b[l].T, D_SGU // SGU_HEADS, axis=1)
        pool_wbd = jax.scipy.linalg.block_diag(*[pool_w[l, gi] for gi in range(len(POOL_WINDOWS))])
        mix_params = (mix_norm_g[l], bf(w_in[l]), conv_w[l], conv_b[l], conv_ln_g[l], conv_ln_b[l],
                      sgu_ln_g[l], sgu_ln_b[l], sgu_w[l], sgu_bias, bf(pool_wbd), pool_b[l].reshape(-1),
                      pool_scale[l], bf(w_out[l]))
        k, v = _kv(mem, mem_norm_g[l], bf(xattn_wk[l]), bf(xattn_wv[l]))
        x = _mix_attn(x, mix_params, k, v, (xattn_norm_g[l], bf(xattn_wq[l]), bf(xattn_wo[l])))
        x2 = x.reshape(B * S, D)
        j = l // 2
        if l % 2 == 0:
            assert l != DEPTH - 1, "the final RMSNorm is fused into the routed layer's combine kernel"
            x2, moe_bf16 = _ffn(x2, ffn_norm_g[l], bf(ffn_wg[j]), bf(ffn_wu[j]), bf(ffn_wd[j]),
                                cast_weights=(moe_wg[j], moe_wu[j], moe_wd[j]))
        else:
            x2 = _moe_layer(x2, ffn_norm_g[l], router_w[j], *moe_bf16, final_norm_g, final_norm=l == DEPTH - 1)
        x = x2.reshape(B, S, D)
    return x
```

```python
import functools

import jax
import jax.numpy as jnp
from jax import lax
from jax.experimental import pallas as pl
from jax.experimental.pallas import tpu as pltpu
from jax.experimental.pallas import tpu_sc as plsc

F32 = jnp.float32
BF16 = jnp.bfloat16
U32 = jnp.uint32
I32 = jnp.int32

D_MODEL = 1024
DEPTH = 2
CHUNK = 64
D_CONV = 384
CONV_WIDTH = 31
D_SGU = 384
SGU_HEADS = 4
SGU_CHUNK = 128
D_POOL = 256
POOL_WINDOWS = (2, 4, 8, 16)
POOL_GROUP_DIM = D_POOL // len(POOL_WINDOWS)
D_MIX = D_CONV + D_SGU + D_POOL
D_IN = 2 * D_CONV + 2 * D_SGU + D_POOL
X_HEADS = 4
X_HEAD_DIM = D_MODEL // X_HEADS
N_EXPERTS = 8
TOP_K = 2
EPS = 1e-6

LANES = 128
SUBLANES = 8
HIST = 32
MIX_TS = 512
CONV_RB = 64
FFN_TM = 512
FFN_FC = 1408
MXU_WIDTH = 256
MOE_TM = 512
MOE_ROW_GROUPS = 2
SC_WIN = 128
HI16 = 0xFFFF0000
VMEM_LIMIT = 56 * 1024 * 1024


def _rms(x, g):
    return x * lax.rsqrt(jnp.mean(x * x, axis=-1, keepdims=True) + EPS) * g


def _layer_norm(x, g, b):
    mu = jnp.mean(x, axis=-1, keepdims=True)
    xc = x - mu
    var = jnp.mean(xc * xc, axis=-1, keepdims=True)
    return xc * lax.rsqrt(var + EPS) * g + b


def _dot(a, b):
    return jnp.dot(a, b, preferred_element_type=F32)


def _round_robin(*generators):
    live = list(generators)
    while live:
        for gen in list(live):
            try:
                next(gen)
                yield
            except StopIteration:
                live.remove(gen)


def _mixer_tile(x, s, g_ref, win_ref, convw_ref, convb_ref, clng_ref, clnb_ref,
                slng_ref, slnb_ref, sguw_ref, sgub_ref, poolw_ref, poolb_ref, pscale_ref,
                wout_ref, cbuf, cshift, pb0, pb1, pb2, pb3, overlap):
    ts = MIX_TS
    a_end = 2 * D_CONV
    b_end = a_end + 2 * D_SGU
    h = _rms(x, g_ref[...]).astype(BF16)
    z_a = _dot(h, win_ref[:, :a_end])
    z_rest = []

    def project_rest():
        for lo in range(a_end, D_IN, MXU_WIDTH):
            z_rest.append(_dot(h, win_ref[:, lo:lo + MXU_WIDTH]))
            yield

    pieces = _round_robin(project_rest(), overlap)

    cbuf[HIST:HIST + ts, :] = z_a[:, :D_CONV] * jax.nn.sigmoid(z_a[:, D_CONV:])
    span = ts + HIST - SUBLANES
    for r in range(1, SUBLANES):
        cshift[r - 1, 0:span, :] = cbuf[r:r + span, :]
    ya_blocks = []
    n_blocks = ts // CONV_RB
    for rb in range(n_blocks):
        acc = jnp.zeros((CONV_RB, D_CONV), F32)
        for k in range(CONV_WIDTH):
            start = rb * CONV_RB + HIST - (CONV_WIDTH - 1) + k
            r, base = start % SUBLANES, start - start % SUBLANES
            src = cbuf[base:base + CONV_RB, :] if r == 0 else cshift[r - 1, base:base + CONV_RB, :]
            acc = acc + convw_ref[k:k + 1, :] * src
        ya_blocks.append(acc)
        for _ in range(2 if rb < n_blocks // 2 else 1):
            next(pieces, None)
    for _ in pieces:
        pass
    ya = jnp.concatenate(ya_blocks, axis=0) + convb_ref[...]
    ya = _layer_norm(ya, clng_ref[...], clnb_ref[...])
    ya = ya * jax.nn.sigmoid(ya)
    cbuf[0:HIST, :] = cbuf[ts:ts + HIST, :]
    out = x + _dot(ya.astype(BF16), wout_ref[0:D_CONV, :])

    z = jnp.concatenate(z_rest, axis=1)
    zb = jax.nn.gelu(z[:, :b_end - a_end])
    u = zb[:, :D_SGU]
    v = _layer_norm(zb[:, D_SGU:], slng_ref[...], slnb_ref[...]).astype(BF16)
    blk_r = lax.broadcasted_iota(jnp.int32, (SGU_CHUNK, SGU_CHUNK), 0) // CHUNK
    blk_c = lax.broadcasted_iota(jnp.int32, (SGU_CHUNK, SGU_CHUNK), 1) // CHUNK
    head_of_lane = lax.broadcasted_iota(jnp.int32, (SGU_CHUNK, D_SGU), 1) // (D_SGU // SGU_HEADS)
    w_heads = [jnp.where(blk_r >= blk_c, sguw_ref[hd], 0.0).astype(BF16) for hd in range(SGU_HEADS)]
    s_chunks = []
    for c in range(ts // SGU_CHUNK):
        vc = v[c * SGU_CHUNK:(c + 1) * SGU_CHUNK, :]
        sc = jnp.zeros((SGU_CHUNK, D_SGU), F32)
        for hd in range(SGU_HEADS):
            sc = jnp.where(head_of_lane == hd, _dot(w_heads[hd], vc), sc)
        s_chunks.append(sc + sgub_ref[...])
    yb = u * jnp.concatenate(s_chunks, axis=0)
    out = out + _dot(yb.astype(BF16), wout_ref[D_CONV:D_CONV + D_SGU, :])

    cc = z[:, b_end - a_end:]
    pb0[HIST:HIST + ts, :] = cc
    pb1[0:ts + 24, :] = pb0[8:ts + 32, :] + pb0[7:ts + 31, :]
    pb2[0:ts + 16, :] = pb1[8:ts + 24, :] + pb1[6:ts + 22, :]
    pb3[0:ts + 8, :] = pb2[8:ts + 16, :] + pb2[4:ts + 12, :]
    s16 = pb3[8:ts + 8, :] + pb3[0:ts, :]
    s8 = pb3[8:ts + 8, :]
    s4 = pb2[16:ts + 16, :]
    s2 = pb1[24:ts + 24, :]
    grp = lax.broadcasted_iota(jnp.int32, (ts, D_POOL), 1) // POOL_GROUP_DIM
    wsum = jnp.where(grp == 0, s2, jnp.where(grp == 1, s4, jnp.where(grp == 2, s8, s16)))
    win = jnp.where(grp == 0, 2, jnp.where(grp == 1, 4, jnp.where(grp == 2, 8, 16)))
    pos = s * ts + lax.broadcasted_iota(jnp.int32, (ts, D_POOL), 0)
    cnt = jnp.minimum(pos + 1, win).astype(F32)
    p = (wsum / cnt - cc).astype(BF16)
    yc = (_dot(p, poolw_ref[...]) + poolb_ref[...]) * pscale_ref[...]
    pb0[0:HIST, :] = pb0[ts:ts + HIST, :]

    return out + _dot(yc.astype(BF16), wout_ref[D_CONV + D_SGU:D_MIX, :])


def _const_spec(shape):
    zeros = (0,) * len(shape)
    return pl.BlockSpec(shape, lambda *_: zeros)


N_MIX_PARAMS = 14
N_ATT_PARAMS = 5


def _mix_attn_kernel(x_ref, *refs, n_tiles, tiles_per_row):
    mix_refs = refs[:N_MIX_PARAMS]
    att_refs = refs[N_MIX_PARAMS:N_MIX_PARAMS + N_ATT_PARAMS]
    o_ref, mid, cbuf, cshift, pb0, pb1, pb2, pb3 = refs[N_MIX_PARAMS + N_ATT_PARAMS:]
    i = pl.program_id(0)
    s = jnp.minimum(i, n_tiles - 1) % tiles_per_row

    @pl.when(i == 0)
    def _():
        mid[...] = jnp.zeros_like(mid)

    @pl.when(s == 0)
    def _():
        cbuf[0:HIST, :] = jnp.zeros((HIST, D_CONV), F32)
        pb0[0:HIST, :] = jnp.zeros((HIST, D_POOL), F32)

    attention = _xattn_pieces(mid[...], o_ref, *att_refs)
    mid[...] = _mixer_tile(x_ref[0], s, *mix_refs, cbuf, cshift, pb0, pb1, pb2, pb3, overlap=attention)


def _mix_attn(x, mix_params, k, v, att_params):
    B, S, D = x.shape
    M = k.shape[1]
    ts = MIX_TS
    tiles_per_row = S // ts
    n_tiles = B * tiles_per_row
    g_att, wq, wo = att_params
    row = lambda a: a.reshape(1, -1) if a.ndim == 1 else a
    mix_args = [row(a) for a in mix_params]
    att_args = [row(g_att), wq, k, v, wo]
    assert len(mix_args) == N_MIX_PARAMS and len(att_args) == N_ATT_PARAMS

    def mix_tile(i):
        t = jnp.minimum(i, n_tiles - 1)
        return t // tiles_per_row, t % tiles_per_row, 0

    def att_tile(i):
        t = jnp.maximum(i - 1, 0)
        return t // tiles_per_row, t % tiles_per_row, 0

    single = lambda a: pl.BlockSpec(a.shape, lambda i, nd=a.ndim: (0,) * nd, pipeline_mode=pl.Buffered(1))
    kv_spec = pl.BlockSpec((1, M, D), lambda i: (att_tile(i)[0], 0, 0))
    in_specs = [pl.BlockSpec((1, ts, D), mix_tile)] + [single(a) for a in mix_args]
    in_specs += [single(att_args[0]), single(wq), kv_spec, kv_spec, single(wo)]
    return pl.pallas_call(
        functools.partial(_mix_attn_kernel, n_tiles=n_tiles, tiles_per_row=tiles_per_row),
        grid=(n_tiles + 1,),
        in_specs=in_specs,
        out_specs=pl.BlockSpec((1, ts, D), att_tile),
        out_shape=jax.ShapeDtypeStruct((B, S, D), F32),
        scratch_shapes=[pltpu.VMEM((ts, D), F32),
                        pltpu.VMEM((ts + HIST, D_CONV), F32),
                        pltpu.VMEM((SUBLANES - 1, ts + HIST - SUBLANES, D_CONV), F32)]
                       + [pltpu.VMEM((ts + HIST, D_POOL), F32)] * 4,
        compiler_params=pltpu.CompilerParams(
            dimension_semantics=("arbitrary",), vmem_limit_bytes=VMEM_LIMIT),
        name="mix_attn",
    )(x, *mix_args, *att_args)


def _kv_kernel(mem_ref, g_ref, wk_ref, wv_ref, k_ref, v_ref):
    m = _rms(mem_ref[0], g_ref[...]).astype(BF16)
    k_ref[0] = _dot(m, wk_ref[...]).astype(BF16)
    v_ref[0] = _dot(m, wv_ref[...]).astype(BF16)


def _kv(mem, g, wk, wv):
    B, M, D = mem.shape
    return pl.pallas_call(
        _kv_kernel,
        grid=(B,),
        in_specs=[pl.BlockSpec((1, M, D), lambda b: (b, 0, 0)), _const_spec((1, D)),
                  _const_spec((D, D)), _const_spec((D, D))],
        out_specs=[pl.BlockSpec((1, M, D), lambda b: (b, 0, 0))] * 2,
        out_shape=[jax.ShapeDtypeStruct((B, M, D), BF16)] * 2,
        compiler_params=pltpu.CompilerParams(
            dimension_semantics=("arbitrary",), vmem_limit_bytes=VMEM_LIMIT),
        name="mem_kv",
    )(mem, g.reshape(1, -1), wk, wv)


def _xattn_pieces(x, o_ref, g_ref, wq_ref, k_ref, v_ref, wo_ref):
    h = _rms(x, g_ref[...]).astype(BF16)
    heads = []
    for hd in range(X_HEADS):
        sl = slice(hd * X_HEAD_DIM, (hd + 1) * X_HEAD_DIM)
        q = _dot(h, wq_ref[:, sl]).astype(BF16)
        sc = lax.dot_general(q, k_ref[0, :, sl], (((1,), (1,)), ((), ())),
                             preferred_element_type=F32) * (X_HEAD_DIM ** -0.5)
        e = jnp.exp(sc - jnp.max(sc, axis=-1, keepdims=True))
        heads.append(_dot(e.astype(BF16), v_ref[0, :, sl]) / jnp.sum(e, axis=-1, keepdims=True))
        yield
    o = jnp.concatenate(heads, axis=-1).astype(BF16)
    for c in range(X_HEADS):
        sl = slice(c * X_HEAD_DIM, (c + 1) * X_HEAD_DIM)
        o_ref[0, :, sl] = x[:, sl] + _dot(o, wo_ref[:, sl])
        yield


def _store_tiles(ref, val):
    for j in range(ref.shape[0]):
        ref[j] = val[:, j * LANES:(j + 1) * LANES]


def _load_tiles(ref):
    return jnp.concatenate([ref[j] for j in range(ref.shape[0])], axis=1)


def _pack_bf16_pairs(v):
    bits = lax.bitcast_convert_type(v.astype(BF16).astype(F32), U32)
    half = bits.shape[1] // 2
    return (bits[:, :half] & jnp.uint32(HI16)) | (bits[:, half:] >> 16)


def _unpack_bf16_pairs(w):
    hi = lax.bitcast_convert_type(w & jnp.uint32(HI16), F32)
    lo = lax.bitcast_convert_type(w << 16, F32)
    return jnp.concatenate([hi, lo], axis=1)


def _tiled_shape(rows, width, dtype):
    return jax.ShapeDtypeStruct((width // LANES, rows, LANES), dtype)


def _tiled_spec(rows, width, row_block, lead=None):
    block = (width // LANES, rows, LANES)
    if lead is None:
        return pl.BlockSpec(block, lambda *a: (0, row_block(*a), 0))
    return pl.BlockSpec((1,) + block, lambda *a: (lead, 0, row_block(*a), 0))


def _piece_index(rows, k, n_rows):
    return jnp.arange(k, dtype=I32)[:, None] * n_rows + rows[..., None, :]


META_I1, META_I2, META_G1, META_G2, META_R1, META_R2 = range(6)


def _meta_col(meta, col):
    lane = lax.broadcasted_iota(I32, meta.shape, 1)
    return jnp.sum(jnp.where(lane == col, meta, 0.0), axis=-1, keepdims=True)


def _router_kernel(x_ref, g_ref, rw_ref, tril_ref, meta_ref, metat_ref, hp_ref, cnt_ref, carry):
    @pl.when(pl.program_id(0) == 0)
    def _():
        carry[...] = jnp.zeros_like(carry)

    h = _rms(x_ref[...], g_ref[...])
    h_hi = h.astype(BF16)
    h_lo = (h - h_hi.astype(F32)).astype(BF16)
    logits = _dot(jnp.concatenate([h_hi, h_hi, h_lo], axis=1), rw_ref[...])
    lane = lax.broadcasted_iota(I32, logits.shape, 1)
    neg = jnp.float32(-jnp.inf)
    lg = jnp.where(lane < N_EXPERTS, logits, neg)
    m1 = jnp.max(lg, axis=-1, keepdims=True)
    i1 = jnp.min(jnp.where(lg == m1, lane, LANES), axis=-1, keepdims=True)
    lg2 = jnp.where(lane == i1, neg, lg)
    m2 = jnp.max(lg2, axis=-1, keepdims=True)
    i2 = jnp.min(jnp.where(lg2 == m2, lane, LANES), axis=-1, keepdims=True)
    d = jnp.exp(m2 - m1)
    g1 = 1.0 / (1.0 + d)
    g2 = d / (1.0 + d)

    sel1 = lane == i1
    sel2 = lane == i2
    onehot = jnp.where(jnp.logical_or(sel1, sel2), 1.0, 0.0)
    before = _dot(tril_ref[...], onehot.astype(BF16)) + carry[...]
    r1 = jnp.sum(jnp.where(sel1, before, 0.0), axis=-1, keepdims=True)
    r2 = jnp.sum(jnp.where(sel2, before, 0.0), axis=-1, keepdims=True)
    carry[...] += jnp.sum(onehot, axis=0, keepdims=True)
    cnt_ref[...] = carry[...]

    meta = jnp.zeros_like(logits)
    for c, val in ((META_I1, i1.astype(F32)), (META_I2, i2.astype(F32)), (META_G1, g1), (META_G2, g2),
                   (META_R1, r1), (META_R2, r2)):
        meta = jnp.where(lane == c, val, meta)
    meta_ref[...] = meta

    _store_tiles(hp_ref, _pack_bf16_pairs(h))
    metat_ref[...] = meta.T[:SUBLANES, :]


def _router(x2, g, router_w, first_row, n_rows):
    D = x2.shape[1]
    tm = FFN_TM
    first_block = first_row // tm
    rw = jnp.zeros((D, LANES), F32).at[:, :N_EXPERTS].set(router_w)
    rw_hi = rw.astype(BF16)
    rw_lo = (rw - rw_hi.astype(F32)).astype(BF16)
    rw_split = jnp.concatenate([rw_hi, rw_lo, rw_hi], axis=0)
    tril = jnp.tril(jnp.ones((tm, tm), BF16), -1)
    return pl.pallas_call(
        _router_kernel,
        grid=(n_rows // tm,),
        in_specs=[pl.BlockSpec((tm, D), lambda i: (first_block + i, 0)), _const_spec((1, D)),
                  _const_spec((3 * D, LANES)), _const_spec((tm, tm))],
        out_specs=[pl.BlockSpec((tm, LANES), lambda i: (i, 0)), pl.BlockSpec((SUBLANES, tm), lambda i: (0, i)),
                   _tiled_spec(tm, D // 2, lambda i: i), _const_spec((1, LANES))],
        out_shape=[jax.ShapeDtypeStruct((n_rows, LANES), F32), jax.ShapeDtypeStruct((SUBLANES, n_rows), F32),
                   _tiled_shape(n_rows, D // 2, U32), jax.ShapeDtypeStruct((1, LANES), F32)],
        scratch_shapes=[pltpu.VMEM((1, LANES), F32)],
        compiler_params=pltpu.CompilerParams(
            dimension_semantics=("arbitrary",), vmem_limit_bytes=VMEM_LIMIT),
        name="router",
    )(x2, g.reshape(1, -1), rw_split, tril)


def _sc_mesh():
    return plsc.VectorSubcoreMesh(core_axis_name="core", subcore_axis_name="subcore")


def _sc_dispatch(tiled, dest1, dest2, n_out):
    k, n, _ = tiled.shape
    d1, d2 = (_piece_index(d, k, n_out).reshape(-1) for d in (dest1, dest2))
    return _sc_scatter_pieces(tiled.reshape(k * n, LANES), d1, d2, k * n_out).reshape(k, n_out, LANES)


def _sc_gather(tiled, idx):
    k, n_rows, _ = tiled.shape
    G, n = idx.shape
    out = _sc_gather_pieces(tiled.reshape(k * n_rows, LANES), _piece_index(idx, k, n_rows).reshape(-1))
    return out.reshape(G, k, n, LANES)


def _sc_scatter_pieces(rows, dest1, dest2, n_out):
    T, W = rows.shape
    win = SC_WIN

    @pl.kernel(out_type=jax.ShapeDtypeStruct((n_out, W), rows.dtype), mesh=_sc_mesh(), scratch_types=[])
    def scatter_kernel(x_hbm, i1_hbm, i2_hbm, o_hbm):
        def body(x_vmem, i1_vmem, i2_vmem):
            pltpu.sync_copy(x_vmem, o_hbm.at[i1_vmem.at[0]])
            pltpu.sync_copy(x_vmem, o_hbm.at[i2_vmem.at[0]])

        pltpu.emit_pipeline(
            body,
            grid=(T // win,),
            in_specs=[pl.BlockSpec((win, W), lambda i: (i, 0)),
                      pl.BlockSpec((1, win), lambda i: (0, i)),
                      pl.BlockSpec((1, win), lambda i: (0, i))],
            out_specs=[],
            core_axis_name=("core", "subcore"),
            dimension_semantics=(pltpu.PARALLEL,),
        )(x_hbm, i1_hbm, i2_hbm)

    return scatter_kernel(rows, dest1.reshape(1, T), dest2.reshape(1, T))


def _sc_gather_pieces(rows, idx):
    n = idx.shape[0]
    W = rows.shape[1]
    win = SC_WIN

    @pl.kernel(out_type=jax.ShapeDtypeStruct((n, W), rows.dtype), mesh=_sc_mesh(), scratch_types=[])
    def gather_kernel(x_hbm, i_hbm, o_hbm):
        def body(i_vmem, o_vmem):
            pltpu.sync_copy(x_hbm.at[i_vmem.at[0]], o_vmem)

        pltpu.emit_pipeline(
            body,
            grid=(n // win,),
            in_specs=[pl.BlockSpec((1, win), lambda i: (0, i))],
            out_specs=[pl.BlockSpec((win, W), lambda i: (i, 0))],
            core_axis_name=("core", "subcore"),
            dimension_semantics=(pltpu.PARALLEL,),
        )(i_hbm, o_hbm)

    return gather_kernel(rows, idx.reshape(1, n))


def _swiglu(h, wg, wu, wd):
    gate = _dot(h, wg)
    up = _dot(h, wu)
    return _dot((gate * jax.nn.sigmoid(gate) * up).astype(BF16), wd)


def _swiglu_chunked(h, wg_ref, wu_ref, wd_ref, acc=None):
    F = wg_ref.shape[1]
    n_chunks = -(-F // FFN_FC)
    cols = -(-F // (n_chunks * MXU_WIDTH)) * MXU_WIDTH
    for lo in range(0, F, cols):
        sl = slice(lo, min(lo + cols, F))
        y = _swiglu(h, wg_ref[:, sl], wu_ref[:, sl], wd_ref[sl, :])
        acc = y if acc is None else acc + y
    return acc


def _ffn_kernel(x_ref, g_ref, wg_ref, wu_ref, wd_ref, *refs):
    n_cast = len(refs) // 2
    o_ref = refs[n_cast]
    x = x_ref[...]
    h = _rms(x, g_ref[...]).astype(BF16)
    o_ref[...] = _swiglu_chunked(h, wg_ref, wu_ref, wd_ref, acc=x)
    for src_ref, dst_ref in zip(refs[:n_cast], refs[n_cast + 1:]):
        dst_ref[...] = src_ref[...].astype(BF16)


def _ffn(x2, g, wg, wu, wd, cast_weights=()):
    T, D = x2.shape
    F = wg.shape[1]
    tm = FFN_TM
    n_steps = T // tm
    single = lambda shape: pl.BlockSpec(shape, lambda i: (0,) * len(shape), pipeline_mode=pl.Buffered(1))
    cast_specs = []
    for w in cast_weights:
        E, rows, cols = w.shape
        per_expert = n_steps // E
        assert n_steps % E == 0 and rows % (per_expert * 2 * SUBLANES) == 0
        cast_specs.append(pl.BlockSpec((1, rows // per_expert, cols),
                                       lambda i, per_expert=per_expert: (i // per_expert, i % per_expert, 0)))
    out, *cast = pl.pallas_call(
        _ffn_kernel,
        grid=(n_steps,),
        in_specs=[pl.BlockSpec((tm, D), lambda i: (i, 0)), single((1, D)),
                  single((D, F)), single((D, F)), single((F, D))] + cast_specs,
        out_specs=[pl.BlockSpec((tm, D), lambda i: (i, 0))] + cast_specs,
        out_shape=[jax.ShapeDtypeStruct((T, D), F32)] + [jax.ShapeDtypeStruct(w.shape, BF16) for w in cast_weights],
        compiler_params=pltpu.CompilerParams(
            dimension_semantics=("arbitrary",), vmem_limit_bytes=VMEM_LIMIT),
        name="swiglu",
    )(x2, g.reshape(1, -1), wg, wu, wd, *cast_weights)
    return out, cast


def _moe_kernel(tidx_ref, texp_ref, nvalid_ref, xs_ref, wg_ref, wu_ref, wd_ref, y_ref):
    del tidx_ref, texp_ref

    @pl.when(pl.program_id(0) < nvalid_ref[0])
    def _():
        h = _unpack_bf16_pairs(_load_tiles(xs_ref)).astype(BF16)
        _store_tiles(y_ref, _pack_bf16_pairs(_swiglu_chunked(h, wg_ref.at[0], wu_ref.at[0], wd_ref.at[0])))


def _moe_experts(xs, tile_idx, tile_expert, n_valid, wg, wu, wd):
    E, D, F = wg.shape
    tm = MOE_TM
    P = xs.shape[1]
    half = xs.shape[0] * LANES
    row_block = lambda g, ti, te, nv: ti[g]
    w_map = lambda g, ti, te, nv: (te[g], 0, 0)
    return pl.pallas_call(
        _moe_kernel,
        grid_spec=pltpu.PrefetchScalarGridSpec(
            num_scalar_prefetch=3,
            grid=(P // tm,),
            in_specs=[_tiled_spec(tm, half, row_block),
                      pl.BlockSpec((1, D, F), w_map), pl.BlockSpec((1, D, F), w_map),
                      pl.BlockSpec((1, F, D), w_map)],
            out_specs=_tiled_spec(tm, D // 2, row_block)),
        out_shape=_tiled_shape(P, D // 2, U32),
        compiler_params=pltpu.CompilerParams(
            dimension_semantics=("arbitrary",), vmem_limit_bytes=VMEM_LIMIT),
        name="moe_experts",
    )(tile_idx, tile_expert, n_valid, xs, wg, wu, wd)


def _combine_kernel(x_ref, y1_ref, y2_ref, meta_ref, fg_ref, *refs, final_norm):
    o_ref = refs[-1]
    meta = meta_ref[...]
    out = x_ref[...] + _meta_col(meta, META_G1) * _unpack_bf16_pairs(_load_tiles(y1_ref.at[0]))
    out = out + _meta_col(meta, META_G2) * _unpack_bf16_pairs(_load_tiles(y2_ref.at[0]))
    if final_norm:
        out = _rms(out, fg_ref[...])
    o_ref[...] = out


def _combine(x2, y12, meta, final_g, first_row, partial_out, *, final_norm):
    T, D = x2.shape
    n_rows = meta.shape[0]
    tm = FFN_TM
    first_block = first_row // tm
    rows = lambda i: (first_block + i, 0)
    args = [x2, y12, y12, meta, final_g.reshape(1, -1)]
    in_specs = [pl.BlockSpec((tm, D), rows),
                _tiled_spec(tm, D // 2, lambda i: i, lead=0), _tiled_spec(tm, D // 2, lambda i: i, lead=1),
                pl.BlockSpec((tm, LANES), lambda i: (i, 0)), _const_spec((1, D))]
    aliases = {}
    if partial_out is not None:
        args.append(partial_out)
        in_specs.append(pl.BlockSpec(memory_space=pl.ANY))
        aliases = {len(args) - 1: 0}
    return pl.pallas_call(
        functools.partial(_combine_kernel, final_norm=final_norm),
        grid=(n_rows // tm,),
        in_specs=in_specs,
        out_specs=pl.BlockSpec((tm, D), rows),
        out_shape=jax.ShapeDtypeStruct((T, D), F32),
        input_output_aliases=aliases,
        compiler_params=pltpu.CompilerParams(
            dimension_semantics=("arbitrary",), vmem_limit_bytes=VMEM_LIMIT),
        name="moe_combine",
    )(*args)


def _route_and_dispatch(x2, g, router_w, first_row, n_rows):
    tm = MOE_TM
    n_tiles = (n_rows * TOP_K) // tm + N_EXPERTS
    meta, meta_t, hp, counts = _router(x2, g, router_w, first_row, n_rows)

    cnt = counts[0, :N_EXPERTS].astype(I32)
    tiles_e = (cnt + tm - 1) // tm
    tile_end = jnp.cumsum(tiles_e)
    row_start = (tile_end - tiles_e) * tm
    experts = jnp.arange(N_EXPERTS, dtype=I32)[:, None]

    def dest(i_row, r_row):
        start = jnp.sum(jnp.where(meta_t[i_row].astype(I32) == experts, row_start[:, None], 0), axis=0)
        return start + meta_t[r_row].astype(I32)

    dests = jnp.stack([dest(META_I1, META_R1), dest(META_I2, META_R2)])
    n_valid = tile_end[-1:]
    tile_idx = jnp.minimum(jnp.arange(n_tiles, dtype=I32), n_valid - 1)
    tile_expert = jnp.minimum(jnp.sum(tile_end[:, None] <= tile_idx, axis=0), N_EXPERTS - 1).astype(I32)
    xs = _sc_dispatch(hp, dests[0], dests[1], n_tiles * tm)
    return xs, (tile_idx, tile_expert, n_valid.astype(I32)), dests, meta


def _moe_layer(x2, g, router_w, wg, wu, wd, final_g, *, final_norm):
    T = x2.shape[0]
    n_rows = T // MOE_ROW_GROUPS
    starts = [grp * n_rows for grp in range(MOE_ROW_GROUPS)]
    routed = [_route_and_dispatch(x2, g, router_w, first_row, n_rows) for first_row in starts]
    ys = [_moe_experts(xs, *tiles, wg, wu, wd) for xs, tiles, _, _ in routed]
    y12s = [_sc_gather(y, dests) for y, (_, _, dests, _) in zip(ys, routed)]
    out = None
    for first_row, y12, (_, _, _, meta) in zip(starts, y12s, routed):
        out = _combine(x2, y12, meta, final_g, first_row, out, final_norm=final_norm)
    return out


def kernel(x, mem, mix_norm_g, w_in, conv_w, conv_b, conv_ln_g, conv_ln_b, sgu_ln_g, sgu_ln_b, sgu_w, sgu_b,
           pool_w, pool_b, pool_scale, w_out, xattn_norm_g, mem_norm_g, xattn_wq, xattn_wk, xattn_wv, xattn_wo,
           ffn_norm_g, ffn_wg, ffn_wu, ffn_wd, router_w, moe_wg, moe_wu, moe_wd, final_norm_g):
    B, S, D = x.shape
    bf = lambda a: a.astype(BF16)
    for l in range(DEPTH):
        sgu_bias = jnp.repeat(sgu_b[l].T, D_SGU // SGU_HEADS, axis=1)
        pool_wbd = jax.scipy.linalg.block_diag(*[pool_w[l, gi] for gi in range(len(POOL_WINDOWS))])
        mix_params = (mix_norm_g[l], bf(w_in[l]), conv_w[l], conv_b[l], conv_ln_g[l], conv_ln_b[l],
                      sgu_ln_g[l], sgu_ln_b[l], sgu_w[l], sgu_bias, bf(pool_wbd), pool_b[l].reshape(-1),
                      pool_scale[l], bf(w_out[l]))
        k, v = _kv(mem, mem_norm_g[l], bf(xattn_wk[l]), bf(xattn_wv[l]))
        x = _mix_attn(x, mix_params, k, v, (xattn_norm_g[l], bf(xattn_wq[l]), bf(xattn_wo[l])))
        x2 = x.reshape(B * S, D)
        j = l // 2
        if l % 2 == 0:
            assert l != DEPTH - 1, "the final RMSNorm is fused into the routed layer's combine kernel"
            x2, moe_bf16 = _ffn(x2, ffn_norm_g[l], bf(ffn_wg[j]), bf(ffn_wu[j]), bf(ffn_wd[j]),
                                cast_weights=(moe_wg[j], moe_wu[j], moe_wd[j]))
        else:
            x2 = _moe_layer(x2, ffn_norm_g[l], router_w[j], *moe_bf16, final_norm_g, final_norm=l == DEPTH - 1)
        x = x2.reshape(B, S, D)
    return x
```

```python
import functools

import jax
import jax.numpy as jnp
from jax import lax
from jax.experimental import pallas as pl
from jax.experimental.pallas import tpu as pltpu
from jax.experimental.pallas import tpu_sc as plsc

F32 = jnp.float32
BF16 = jnp.bfloat16
U32 = jnp.uint32
I32 = jnp.int32

D_MODEL = 1024
DEPTH = 2
CHUNK = 64
D_CONV = 384
CONV_WIDTH = 31
D_SGU = 384
SGU_HEADS = 4
SGU_CHUNK = 128
D_POOL = 256
POOL_WINDOWS = (2, 4, 8, 16)
POOL_GROUP_DIM = D_POOL // len(POOL_WINDOWS)
D_MIX = D_CONV + D_SGU + D_POOL
D_IN = 2 * D_CONV + 2 * D_SGU + D_POOL
X_HEADS = 4
X_HEAD_DIM = D_MODEL // X_HEADS
N_EXPERTS = 8
TOP_K = 2
EPS = 1e-6

LANES = 128
SUBLANES = 8
HIST = 32
MIX_TS = 512
CONV_RB = 64
FFN_TM = 512
FFN_FC = 1408
MXU_WIDTH = 256
MOE_TM = 512
MOE_ROW_GROUPS = 2
SC_WIN = 128
HI16 = 0xFFFF0000
VMEM_LIMIT = 56 * 1024 * 1024


def _rms(x, g):
    return x * lax.rsqrt(jnp.mean(x * x, axis=-1, keepdims=True) + EPS) * g


def _layer_norm(x, g, b):
    mu = jnp.mean(x, axis=-1, keepdims=True)
    xc = x - mu
    var = jnp.mean(xc * xc, axis=-1, keepdims=True)
    return xc * lax.rsqrt(var + EPS) * g + b


def _dot(a, b):
    return jnp.dot(a, b, preferred_element_type=F32)


def _round_robin(*generators):
    live = list(generators)
    while live:
        for gen in list(live):
            try:
                next(gen)
                yield
            except StopIteration:
                live.remove(gen)


def _mixer_tile(x, s, g_ref, win_ref, convw_ref, convb_ref, clng_ref, clnb_ref,
                slng_ref, slnb_ref, sguw_ref, sgub_ref, poolw_ref, poolb_ref, pscale_ref,
                wout_ref, cbuf, cshift, pb0, pb1, pb2, pb3, overlap):
    ts = MIX_TS
    a_end = 2 * D_CONV
    b_end = a_end + 2 * D_SGU
    h = _rms(x, g_ref[...]).astype(BF16)
    z_a = _dot(h, win_ref[:, :a_end])
    z_rest = []

    def project_rest():
        for lo in range(a_end, D_IN, MXU_WIDTH):
            z_rest.append(_dot(h, win_ref[:, lo:lo + MXU_WIDTH]))
            yield

    pieces = _round_robin(project_rest(), overlap)

    cbuf[HIST:HIST + ts, :] = z_a[:, :D_CONV] * jax.nn.sigmoid(z_a[:, D_CONV:])
    span = ts + HIST - SUBLANES
    for r in range(1, SUBLANES):
        cshift[r - 1, 0:span, :] = cbuf[r:r + span, :]
    ya_blocks = []
    n_blocks = ts // CONV_RB
    for rb in range(n_blocks):
        acc = jnp.zeros((CONV_RB, D_CONV), F32)
        for k in range(CONV_WIDTH):
            start = rb * CONV_RB + HIST - (CONV_WIDTH - 1) + k
            r, base = start % SUBLANES, start - start % SUBLANES
            src = cbuf[base:base + CONV_RB, :] if r == 0 else cshift[r - 1, base:base + CONV_RB, :]
            acc = acc + convw_ref[k:k + 1, :] * src
        ya_blocks.append(acc)
        for _ in range(2 if rb < n_blocks // 2 else 1):
            next(pieces, None)
    for _ in pieces:
        pass
    ya = jnp.concatenate(ya_blocks, axis=0) + convb_ref[...]
    ya = _layer_norm(ya, clng_ref[...], clnb_ref[...])
    ya = ya * jax.nn.sigmoid(ya)
    cbuf[0:HIST, :] = cbuf[ts:ts + HIST, :]
    out = x + _dot(ya.astype(BF16), wout_ref[0:D_CONV, :])

    z = jnp.concatenate(z_rest, axis=1)
    zb = jax.nn.gelu(z[:, :b_end - a_end])
    u = zb[:, :D_SGU]
    v = _layer_norm(zb[:, D_SGU:], slng_ref[...], slnb_ref[...]).astype(BF16)
    blk_r = lax.broadcasted_iota(jnp.int32, (SGU_CHUNK, SGU_CHUNK), 0) // CHUNK
    blk_c = lax.broadcasted_iota(jnp.int32, (SGU_CHUNK, SGU_CHUNK), 1) // CHUNK
    head_of_lane = lax.broadcasted_iota(jnp.int32, (SGU_CHUNK, D_SGU), 1) // (D_SGU // SGU_HEADS)
    w_heads = [jnp.where(blk_r >= blk_c, sguw_ref[hd], 0.0).astype(BF16) for hd in range(SGU_HEADS)]
    s_chunks = []
    for c in range(ts // SGU_CHUNK):
        vc = v[c * SGU_CHUNK:(c + 1) * SGU_CHUNK, :]
        sc = jnp.zeros((SGU_CHUNK, D_SGU), F32)
        for hd in range(SGU_HEADS):
            sc = jnp.where(head_of_lane == hd, _dot(w_heads[hd], vc), sc)
        s_chunks.append(sc + sgub_ref[...])
    yb = u * jnp.concatenate(s_chunks, axis=0)
    out = out + _dot(yb.astype(BF16), wout_ref[D_CONV:D_CONV + D_SGU, :])

    cc = z[:, b_end - a_end:]
    pb0[HIST:HIST + ts, :] = cc
    pb1[0:ts + 24, :] = pb0[8:ts + 32, :] + pb0[7:ts + 31, :]
    pb2[0:ts + 16, :] = pb1[8:ts + 24, :] + pb1[6:ts + 22, :]
    pb3[0:ts + 8, :] = pb2[8:ts + 16, :] + pb2[4:ts + 12, :]
    s16 = pb3[8:ts + 8, :] + pb3[0:ts, :]
    s8 = pb3[8:ts + 8, :]
    s4 = pb2[16:ts + 16, :]
    s2 = pb1[24:ts + 24, :]
    grp = lax.broadcasted_iota(jnp.int32, (ts, D_POOL), 1) // POOL_GROUP_DIM
    wsum = jnp.where(grp == 0, s2, jnp.where(grp == 1, s4, jnp.where(grp == 2, s8, s16)))
    win = jnp.where(grp == 0, 2, jnp.where(grp == 1, 4, jnp.where(grp == 2, 8, 16)))
    pos = s * ts + lax.broadcasted_iota(jnp.int32, (ts, D_POOL), 0)
    cnt = jnp.minimum(pos + 1, win).astype(F32)
    p = (wsum / cnt - cc).astype(BF16)
    yc = (_dot(p, poolw_ref[...]) + poolb_ref[...]) * pscale_ref[...]
    pb0[0:HIST, :] = pb0[ts:ts + HIST, :]

    return out + _dot(yc.astype(BF16), wout_ref[D_CONV + D_SGU:D_MIX, :])


def _const_spec(shape):
    zeros = (0,) * len(shape)
    return pl.BlockSpec(shape, lambda *_: zeros)


N_MIX_PARAMS = 14
N_ATT_PARAMS = 5


def _mix_attn_kernel(x_ref, *refs, n_tiles, tiles_per_row):
    mix_refs = refs[:N_MIX_PARAMS]
    att_refs = refs[N_MIX_PARAMS:N_MIX_PARAMS + N_ATT_PARAMS]
    o_ref, mid, cbuf, cshift, pb0, pb1, pb2, pb3 = refs[N_MIX_PARAMS + N_ATT_PARAMS:]
    i = pl.program_id(0)
    s = jnp.minimum(i, n_tiles - 1) % tiles_per_row

    @pl.when(i == 0)
    def _():
        mid[...] = jnp.zeros_like(mid)

    @pl.when(s == 0)
    def _():
        cbuf[0:HIST, :] = jnp.zeros((HIST, D_CONV), F32)
        pb0[0:HIST, :] = jnp.zeros((HIST, D_POOL), F32)

    attention = _xattn_pieces(mid[...], o_ref, *att_refs)
    mid[...] = _mixer_tile(x_ref[0], s, *mix_refs, cbuf, cshift, pb0, pb1, pb2, pb3, overlap=attention)


def _mix_attn(x, mix_params, k, v, att_params):
    B, S, D = x.shape
    M = k.shape[1]
    ts = MIX_TS
    tiles_per_row = S // ts
    n_tiles = B * tiles_per_row
    g_att, wq, wo = att_params
    row = lambda a: a.reshape(1, -1) if a.ndim == 1 else a
    mix_args = [row(a) for a in mix_params]
    att_args = [row(g_att), wq, k, v, wo]
    assert len(mix_args) == N_MIX_PARAMS and len(att_args) == N_ATT_PARAMS

    def mix_tile(i):
        t = jnp.minimum(i, n_tiles - 1)
        return t // tiles_per_row, t % tiles_per_row, 0

    def att_tile(i):
        t = jnp.maximum(i - 1, 0)
        return t // tiles_per_row, t % tiles_per_row, 0

    single = lambda a: pl.BlockSpec(a.shape, lambda i, nd=a.ndim: (0,) * nd, pipeline_mode=pl.Buffered(1))
    kv_spec = pl.BlockSpec((1, M, D), lambda i: (att_tile(i)[0], 0, 0))
    in_specs = [pl.BlockSpec((1, ts, D), mix_tile)] + [single(a) for a in mix_args]
    in_specs += [single(att_args[0]), single(wq), kv_spec, kv_spec, single(wo)]
    return pl.pallas_call(
        functools.partial(_mix_attn_kernel, n_tiles=n_tiles, tiles_per_row=tiles_per_row),
        grid=(n_tiles + 1,),
        in_specs=in_specs,
        out_specs=pl.BlockSpec((1, ts, D), att_tile),
        out_shape=jax.ShapeDtypeStruct((B, S, D), F32),
        scratch_shapes=[pltpu.VMEM((ts, D), F32),
                        pltpu.VMEM((ts + HIST, D_CONV), F32),
                        pltpu.VMEM((SUBLANES - 1, ts + HIST - SUBLANES, D_CONV), F32)]
                       + [pltpu.VMEM((ts + HIST, D_POOL), F32)] * 4,
        compiler_params=pltpu.CompilerParams(
            dimension_semantics=("arbitrary",), vmem_limit_bytes=VMEM_LIMIT),
        name="mix_attn",
    )(x, *mix_args, *att_args)


def _kv_kernel(mem_ref, g_ref, wk_ref, wv_ref, k_ref, v_ref):
    m = _rms(mem_ref[0], g_ref[...]).astype(BF16)
    k_ref[0] = _dot(m, wk_ref[...]).astype(BF16)
    v_ref[0] = _dot(m, wv_ref[...]).astype(BF16)


def _kv(mem, g, wk, wv):
    B, M, D = mem.shape
    return pl.pallas_call(
        _kv_kernel,
        grid=(B,),
        in_specs=[pl.BlockSpec((1, M, D), lambda b: (b, 0, 0)), _const_spec((1, D)),
                  _const_spec((D, D)), _const_spec((D, D))],
        out_specs=[pl.BlockSpec((1, M, D), lambda b: (b, 0, 0))] * 2,
        out_shape=[jax.ShapeDtypeStruct((B, M, D), BF16)] * 2,
        compiler_params=pltpu.CompilerParams(
            dimension_semantics=("arbitrary",), vmem_limit_bytes=VMEM_LIMIT),
        name="mem_kv",
    )(mem, g.reshape(1, -1), wk, wv)


def _xattn_pieces(x, o_ref, g_ref, wq_ref, k_ref, v_ref, wo_ref):
    h = _rms(x, g_ref[...]).astype(BF16)
    heads = []
    for hd in range(X_HEADS):
        sl = slice(hd * X_HEAD_DIM, (hd + 1) * X_HEAD_DIM)
        q = _dot(h, wq_ref[:, sl]).astype(BF16)
        sc = lax.dot_general(q, k_ref[0, :, sl], (((1,), (1,)), ((), ())),
                             preferred_element_type=F32) * (X_HEAD_DIM ** -0.5)
        e = jnp.exp(sc - jnp.max(sc, axis=-1, keepdims=True))
        heads.append(_dot(e.astype(BF16), v_ref[0, :, sl]) / jnp.sum(e, axis=-1, keepdims=True))
        yield
    o = jnp.concatenate(heads, axis=-1).astype(BF16)
    for c in range(X_HEADS):
        sl = slice(c * X_HEAD_DIM, (c + 1) * X_HEAD_DIM)
        o_ref[0, :, sl] = x[:, sl] + _dot(o, wo_ref[:, sl])
        yield


def _store_tiles(ref, val):
    for j in range(ref.shape[0]):
        ref[j] = val[:, j * LANES:(j + 1) * LANES]


def _load_tiles(ref):
    return jnp.concatenate([ref[j] for j in range(ref.shape[0])], axis=1)


def _pack_bf16_pairs(v):
    bits = lax.bitcast_convert_type(v.astype(BF16).astype(F32), U32)
    half = bits.shape[1] // 2
    return (bits[:, :half] & jnp.uint32(HI16)) | (bits[:, half:] >> 16)


def _unpack_bf16_pairs(w):
    hi = lax.bitcast_convert_type(w & jnp.uint32(HI16), F32)
    lo = lax.bitcast_convert_type(w << 16, F32)
    return jnp.concatenate([hi, lo], axis=1)


def _tiled_shape(rows, width, dtype):
    return jax.ShapeDtypeStruct((width // LANES, rows, LANES), dtype)


def _tiled_spec(rows, width, row_block, lead=None):
    block = (width // LANES, rows, LANES)
    if lead is None:
        return pl.BlockSpec(block, lambda *a: (0, row_block(*a), 0))
    return pl.BlockSpec((1,) + block, lambda *a: (lead, 0, row_block(*a), 0))


def _piece_index(rows, k, n_rows):
    return jnp.arange(k, dtype=I32)[:, None] * n_rows + rows[..., None, :]


META_I1, META_I2, META_G1, META_G2, META_R1, META_R2 = range(6)


def _router_kernel(x_ref, g_ref, rw_ref, triu_ref, meta_ref, hp_ref, cnt_ref, carry):
    @pl.when(pl.program_id(0) == 0)
    def _():
        carry[...] = jnp.zeros_like(carry)

    h = _rms(x_ref[...], g_ref[...])
    h_hi = h.astype(BF16)
    h_lo = (h - h_hi.astype(F32)).astype(BF16)
    logits = _dot(jnp.concatenate([h_hi, h_hi, h_lo], axis=1), rw_ref[...])

    neg = jnp.float32(-jnp.inf)
    lane = lax.broadcasted_iota(I32, logits.shape, 1)
    lt = jnp.where(lane < N_EXPERTS, logits, neg).T[:N_EXPERTS, :]
    expert = lax.broadcasted_iota(I32, lt.shape, 0)
    m1 = jnp.max(lt, axis=0, keepdims=True)
    i1 = jnp.min(jnp.where(lt == m1, expert, N_EXPERTS), axis=0, keepdims=True)
    lt2 = jnp.where(expert == i1, neg, lt)
    m2 = jnp.max(lt2, axis=0, keepdims=True)
    i2 = jnp.min(jnp.where(lt2 == m2, expert, N_EXPERTS), axis=0, keepdims=True)
    d = jnp.exp(m2 - m1)
    g1 = 1.0 / (1.0 + d)
    g2 = d / (1.0 + d)

    sel1 = expert == i1
    sel2 = expert == i2
    onehot = jnp.where(jnp.logical_or(sel1, sel2), 1.0, 0.0)
    before = _dot(onehot, triu_ref[...]) + carry[:, 0:1]
    r1 = jnp.sum(jnp.where(sel1, before, 0.0), axis=0, keepdims=True)
    r2 = jnp.sum(jnp.where(sel2, before, 0.0), axis=0, keepdims=True)
    carry[...] += jnp.sum(onehot, axis=1, keepdims=True)
    cnt_ref[...] = carry[...]

    rows = [i1.astype(F32), i2.astype(F32), g1, g2, r1, r2]
    meta_ref[...] = jnp.concatenate(rows + [jnp.zeros_like(g1)] * (SUBLANES - len(rows)), axis=0)
    _store_tiles(hp_ref, _pack_bf16_pairs(h))


def _router(x2, g, router_w, first_row, n_rows):
    D = x2.shape[1]
    tm = FFN_TM
    first_block = first_row // tm
    assert N_EXPERTS == SUBLANES
    rw = jnp.zeros((D, LANES), F32).at[:, :N_EXPERTS].set(router_w)
    rw_hi = rw.astype(BF16)
    rw_lo = (rw - rw_hi.astype(F32)).astype(BF16)
    rw_split = jnp.concatenate([rw_hi, rw_lo, rw_hi], axis=0)
    triu = jnp.triu(jnp.ones((tm, tm), F32), 1)
    return pl.pallas_call(
        _router_kernel,
        grid=(n_rows // tm,),
        in_specs=[pl.BlockSpec((tm, D), lambda i: (first_block + i, 0)), _const_spec((1, D)),
                  _const_spec((3 * D, LANES)), _const_spec((tm, tm))],
        out_specs=[pl.BlockSpec((SUBLANES, tm), lambda i: (0, i)),
                   _tiled_spec(tm, D // 2, lambda i: i), _const_spec((N_EXPERTS, LANES))],
        out_shape=[jax.ShapeDtypeStruct((SUBLANES, n_rows), F32),
                   _tiled_shape(n_rows, D // 2, U32), jax.ShapeDtypeStruct((N_EXPERTS, LANES), F32)],
        scratch_shapes=[pltpu.VMEM((N_EXPERTS, LANES), F32)],
        compiler_params=pltpu.CompilerParams(
            dimension_semantics=("arbitrary",), vmem_limit_bytes=VMEM_LIMIT),
        name="router",
    )(x2, g.reshape(1, -1), rw_split, triu)


def _sc_mesh():
    return plsc.VectorSubcoreMesh(core_axis_name="core", subcore_axis_name="subcore")


def _sc_dispatch(tiled, dest1, dest2, n_out):
    k, n, _ = tiled.shape
    d1, d2 = (_piece_index(d, k, n_out).reshape(-1) for d in (dest1, dest2))
    return _sc_scatter_pieces(tiled.reshape(k * n, LANES), d1, d2, k * n_out).reshape(k, n_out, LANES)


def _sc_gather(tiled, idx):
    k, n_rows, _ = tiled.shape
    G, n = idx.shape
    out = _sc_gather_pieces(tiled.reshape(k * n_rows, LANES), _piece_index(idx, k, n_rows).reshape(-1))
    return out.reshape(G, k, n, LANES)


def _sc_scatter_pieces(rows, dest1, dest2, n_out):
    T, W = rows.shape
    win = SC_WIN

    @pl.kernel(out_type=jax.ShapeDtypeStruct((n_out, W), rows.dtype), mesh=_sc_mesh(), scratch_types=[])
    def scatter_kernel(x_hbm, i1_hbm, i2_hbm, o_hbm):
        def body(x_vmem, i1_vmem, i2_vmem):
            pltpu.sync_copy(x_vmem, o_hbm.at[i1_vmem.at[0]])
            pltpu.sync_copy(x_vmem, o_hbm.at[i2_vmem.at[0]])

        pltpu.emit_pipeline(
            body,
            grid=(T // win,),
            in_specs=[pl.BlockSpec((win, W), lambda i: (i, 0)),
                      pl.BlockSpec((1, win), lambda i: (0, i)),
                      pl.BlockSpec((1, win), lambda i: (0, i))],
            out_specs=[],
            core_axis_name=("core", "subcore"),
            dimension_semantics=(pltpu.PARALLEL,),
        )(x_hbm, i1_hbm, i2_hbm)

    return scatter_kernel(rows, dest1.reshape(1, T), dest2.reshape(1, T))


def _sc_gather_pieces(rows, idx):
    n = idx.shape[0]
    W = rows.shape[1]
    win = SC_WIN

    @pl.kernel(out_type=jax.ShapeDtypeStruct((n, W), rows.dtype), mesh=_sc_mesh(), scratch_types=[])
    def gather_kernel(x_hbm, i_hbm, o_hbm):
        def body(i_vmem, o_vmem):
            pltpu.sync_copy(x_hbm.at[i_vmem.at[0]], o_vmem)

        pltpu.emit_pipeline(
            body,
            grid=(n // win,),
            in_specs=[pl.BlockSpec((1, win), lambda i: (0, i))],
            out_specs=[pl.BlockSpec((win, W), lambda i: (i, 0))],
            core_axis_name=("core", "subcore"),
            dimension_semantics=(pltpu.PARALLEL,),
        )(i_hbm, o_hbm)

    return gather_kernel(rows, idx.reshape(1, n))


def _swiglu(h, wg, wu, wd):
    gate = _dot(h, wg)
    up = _dot(h, wu)
    return _dot((gate * jax.nn.sigmoid(gate) * up).astype(BF16), wd)


def _swiglu_chunked(h, wg_ref, wu_ref, wd_ref, acc=None):
    F = wg_ref.shape[1]
    n_chunks = -(-F // FFN_FC)
    cols = -(-F // (n_chunks * MXU_WIDTH)) * MXU_WIDTH
    for lo in range(0, F, cols):
        sl = slice(lo, min(lo + cols, F))
        y = _swiglu(h, wg_ref[:, sl], wu_ref[:, sl], wd_ref[sl, :])
        acc = y if acc is None else acc + y
    return acc


def _ffn_kernel(x_ref, g_ref, wg_ref, wu_ref, wd_ref, *refs):
    n_cast = len(refs) // 2
    o_ref = refs[n_cast]
    x = x_ref[...]
    h = _rms(x, g_ref[...]).astype(BF16)
    o_ref[...] = _swiglu_chunked(h, wg_ref, wu_ref, wd_ref, acc=x)
    for src_ref, dst_ref in zip(refs[:n_cast], refs[n_cast + 1:]):
        dst_ref[...] = src_ref[...].astype(BF16)


def _ffn(x2, g, wg, wu, wd, cast_weights=()):
    T, D = x2.shape
    F = wg.shape[1]
    tm = FFN_TM
    n_steps = T // tm
    single = lambda shape: pl.BlockSpec(shape, lambda i: (0,) * len(shape), pipeline_mode=pl.Buffered(1))
    cast_specs = []
    for w in cast_weights:
        E, rows, cols = w.shape
        per_expert = n_steps // E
        assert n_steps % E == 0 and rows % (per_expert * 2 * SUBLANES) == 0
        cast_specs.append(pl.BlockSpec((1, rows // per_expert, cols),
                                       lambda i, per_expert=per_expert: (i // per_expert, i % per_expert, 0)))
    out, *cast = pl.pallas_call(
        _ffn_kernel,
        grid=(n_steps,),
        in_specs=[pl.BlockSpec((tm, D), lambda i: (i, 0)), single((1, D)),
                  single((D, F)), single((D, F)), single((F, D))] + cast_specs,
        out_specs=[pl.BlockSpec((tm, D), lambda i: (i, 0))] + cast_specs,
        out_shape=[jax.ShapeDtypeStruct((T, D), F32)] + [jax.ShapeDtypeStruct(w.shape, BF16) for w in cast_weights],
        compiler_params=pltpu.CompilerParams(
            dimension_semantics=("arbitrary",), vmem_limit_bytes=VMEM_LIMIT),
        name="swiglu",
    )(x2, g.reshape(1, -1), wg, wu, wd, *cast_weights)
    return out, cast


def _moe_kernel(tidx_ref, texp_ref, nvalid_ref, xs_ref, wg_ref, wu_ref, wd_ref, y_ref):
    del tidx_ref, texp_ref

    @pl.when(pl.program_id(0) < nvalid_ref[0])
    def _():
        h = _unpack_bf16_pairs(_load_tiles(xs_ref)).astype(BF16)
        _store_tiles(y_ref, _pack_bf16_pairs(_swiglu_chunked(h, wg_ref.at[0], wu_ref.at[0], wd_ref.at[0])))


def _moe_experts(xs, tile_idx, tile_expert, n_valid, wg, wu, wd):
    E, D, F = wg.shape
    tm = MOE_TM
    P = xs.shape[1]
    half = xs.shape[0] * LANES
    row_block = lambda g, ti, te, nv: ti[g]
    w_map = lambda g, ti, te, nv: (te[g], 0, 0)
    return pl.pallas_call(
        _moe_kernel,
        grid_spec=pltpu.PrefetchScalarGridSpec(
            num_scalar_prefetch=3,
            grid=(P // tm,),
            in_specs=[_tiled_spec(tm, half, row_block),
                      pl.BlockSpec((1, D, F), w_map), pl.BlockSpec((1, D, F), w_map),
                      pl.BlockSpec((1, F, D), w_map)],
            out_specs=_tiled_spec(tm, D // 2, row_block)),
        out_shape=_tiled_shape(P, D // 2, U32),
        compiler_params=pltpu.CompilerParams(
            dimension_semantics=("arbitrary",), vmem_limit_bytes=VMEM_LIMIT),
        name="moe_experts",
    )(tile_idx, tile_expert, n_valid, xs, wg, wu, wd)


def _combine_kernel(x_ref, y1_ref, y2_ref, meta_ref, fg_ref, *refs, final_norm):
    o_ref = refs[-1]
    rec = meta_ref[...]
    rec_t = jnp.concatenate([rec, jnp.zeros((LANES - rec.shape[0], rec.shape[1]), F32)], axis=0).T
    lane = lax.broadcasted_iota(I32, rec_t.shape, 1)
    gate = lambda row: jnp.sum(jnp.where(lane == row, rec_t, 0.0), axis=-1, keepdims=True)
    out = x_ref[...] + gate(META_G1) * _unpack_bf16_pairs(_load_tiles(y1_ref.at[0]))
    out = out + gate(META_G2) * _unpack_bf16_pairs(_load_tiles(y2_ref.at[0]))
    if final_norm:
        out = _rms(out, fg_ref[...])
    o_ref[...] = out


def _combine(x2, y12, meta, final_g, first_row, partial_out, *, final_norm):
    T, D = x2.shape
    n_rows = meta.shape[1]
    tm = FFN_TM
    first_block = first_row // tm
    rows = lambda i: (first_block + i, 0)
    args = [x2, y12, y12, meta, final_g.reshape(1, -1)]
    in_specs = [pl.BlockSpec((tm, D), rows),
                _tiled_spec(tm, D // 2, lambda i: i, lead=0), _tiled_spec(tm, D // 2, lambda i: i, lead=1),
                pl.BlockSpec((SUBLANES, tm), lambda i: (0, i)), _const_spec((1, D))]
    aliases = {}
    if partial_out is not None:
        args.append(partial_out)
        in_specs.append(pl.BlockSpec(memory_space=pl.ANY))
        aliases = {len(args) - 1: 0}
    return pl.pallas_call(
        functools.partial(_combine_kernel, final_norm=final_norm),
        grid=(n_rows // tm,),
        in_specs=in_specs,
        out_specs=pl.BlockSpec((tm, D), rows),
        out_shape=jax.ShapeDtypeStruct((T, D), F32),
        input_output_aliases=aliases,
        compiler_params=pltpu.CompilerParams(
            dimension_semantics=("arbitrary",), vmem_limit_bytes=VMEM_LIMIT),
        name="moe_combine",
    )(*args)


def _route_and_dispatch(x2, g, router_w, first_row, n_rows):
    tm = MOE_TM
    n_tiles = (n_rows * TOP_K) // tm + N_EXPERTS
    meta, hp, counts = _router(x2, g, router_w, first_row, n_rows)

    cnt = counts[:, 0].astype(I32)
    tiles_e = (cnt + tm - 1) // tm
    tile_end = jnp.cumsum(tiles_e)
    row_start = (tile_end - tiles_e) * tm
    experts = jnp.arange(N_EXPERTS, dtype=I32)[:, None]

    def dest(i_row, r_row):
        start = jnp.sum(jnp.where(meta[i_row].astype(I32) == experts, row_start[:, None], 0), axis=0)
        return start + meta[r_row].astype(I32)

    dests = jnp.stack([dest(META_I1, META_R1), dest(META_I2, META_R2)])
    n_valid = tile_end[-1:]
    tile_idx = jnp.minimum(jnp.arange(n_tiles, dtype=I32), n_valid - 1)
    tile_expert = jnp.minimum(jnp.sum(tile_end[:, None] <= tile_idx, axis=0), N_EXPERTS - 1).astype(I32)
    xs = _sc_dispatch(hp, dests[0], dests[1], n_tiles * tm)
    return xs, (tile_idx, tile_expert, n_valid.astype(I32)), dests, meta


def _moe_layer(x2, g, router_w, wg, wu, wd, final_g, *, final_norm):
    T = x2.shape[0]
    n_rows = T // MOE_ROW_GROUPS
    starts = [grp * n_rows for grp in range(MOE_ROW_GROUPS)]
    routed = [_route_and_dispatch(x2, g, router_w, first_row, n_rows) for first_row in starts]
    ys = [_moe_experts(xs, *tiles, wg, wu, wd) for xs, tiles, _, _ in routed]
    y12s = [_sc_gather(y, dests) for y, (_, _, dests, _) in zip(ys, routed)]
    out = None
    for first_row, y12, (_, _, _, meta) in zip(starts, y12s, routed):
        out = _combine(x2, y12, meta, final_g, first_row, out, final_norm=final_norm)
    return out


def kernel(x, mem, mix_norm_g, w_in, conv_w, conv_b, conv_ln_g, conv_ln_b, sgu_ln_g, sgu_ln_b, sgu_w, sgu_b,
           pool_w, pool_b, pool_scale, w_out, xattn_norm_g, mem_norm_g, xattn_wq, xattn_wk, xattn_wv, xattn_wo,
           ffn_norm_g, ffn_wg, ffn_wu, ffn_wd, router_w, moe_wg, moe_wu, moe_wd, final_norm_g):
    B, S, D = x.shape
    bf = lambda a: a.astype(BF16)
    for l in range(DEPTH):
        sgu_bias = jnp.repeat(sgu_b[l].T, D_SGU // SGU_HEADS, axis=1)
        pool_wbd = jax.scipy.linalg.block_diag(*[pool_w[l, gi] for gi in range(len(POOL_WINDOWS))])
        mix_params = (mix_norm_g[l], bf(w_in[l]), conv_w[l], conv_b[l], conv_ln_g[l], conv_ln_b[l],
                      sgu_ln_g[l], sgu_ln_b[l], sgu_w[l], sgu_bias, bf(pool_wbd), pool_b[l].reshape(-1),
                      pool_scale[l], bf(w_out[l]))
        k, v = _kv(mem, mem_norm_g[l], bf(xattn_wk[l]), bf(xattn_wv[l]))
        x = _mix_attn(x, mix_params, k, v, (xattn_norm_g[l], bf(xattn_wq[l]), bf(xattn_wo[l])))
        x2 = x.reshape(B * S, D)
        j = l // 2
        if l % 2 == 0:
            assert l != DEPTH - 1, "the final RMSNorm is fused into the routed layer's combine kernel"
            x2, moe_bf16 = _ffn(x2, ffn_norm_g[l], bf(ffn_wg[j]), bf(ffn_wu[j]), bf(ffn_wd[j]),
                                cast_weights=(moe_wg[j], moe_wu[j], moe_wd[j]))
        else:
            x2 = _moe_layer(x2, ffn_norm_g[l], router_w[j], *moe_bf16, final_norm_g, final_norm=l == DEPTH - 1)
        x = x2.reshape(B, S, D)
    return x
```

```python
import functools

import jax
import jax.numpy as jnp
from jax import lax
from jax.experimental import pallas as pl
from jax.experimental.pallas import tpu as pltpu
from jax.experimental.pallas import tpu_sc as plsc

F32 = jnp.float32
BF16 = jnp.bfloat16
U32 = jnp.uint32
I32 = jnp.int32

D_MODEL = 1024
DEPTH = 2
CHUNK = 64
D_CONV = 384
CONV_WIDTH = 31
D_SGU = 384
SGU_HEADS = 4
SGU_CHUNK = 128
D_POOL = 256
POOL_WINDOWS = (2, 4, 8, 16)
POOL_GROUP_DIM = D_POOL // len(POOL_WINDOWS)
D_MIX = D_CONV + D_SGU + D_POOL
D_IN = 2 * D_CONV + 2 * D_SGU + D_POOL
X_HEADS = 4
X_HEAD_DIM = D_MODEL // X_HEADS
N_EXPERTS = 8
TOP_K = 2
EPS = 1e-6

LANES = 128
SUBLANES = 8
HIST = 32
MIX_TS = 512
CONV_RB = 64
FFN_TM = 512
FFN_FC = 1408
MXU_WIDTH = 256
MOE_TM = 512
MOE_ROW_GROUPS = 2
SC_WIN = 128
HI16 = 0xFFFF0000
VMEM_LIMIT = 56 * 1024 * 1024


def _rms(x, g):
    return x * lax.rsqrt(jnp.mean(x * x, axis=-1, keepdims=True) + EPS) * g


def _layer_norm(x, g, b):
    mu = jnp.mean(x, axis=-1, keepdims=True)
    xc = x - mu
    var = jnp.mean(xc * xc, axis=-1, keepdims=True)
    return xc * lax.rsqrt(var + EPS) * g + b


def _dot(a, b):
    return jnp.dot(a, b, preferred_element_type=F32)


def _round_robin(*generators):
    live = list(generators)
    while live:
        for gen in list(live):
            try:
                next(gen)
                yield
            except StopIteration:
                live.remove(gen)


def _mixer_tile(x, s, g_ref, win_ref, convw_ref, convb_ref, clng_ref, clnb_ref,
                slng_ref, slnb_ref, sguw_ref, sgub_ref, poolw_ref, poolb_ref, pscale_ref,
                wout_ref, cbuf, cshift, pb0, pb1, pb2, pb3, overlap):
    ts = MIX_TS
    a_end = 2 * D_CONV
    b_end = a_end + 2 * D_SGU
    h = _rms(x, g_ref[...]).astype(BF16)
    z_a = _dot(h, win_ref[:, :a_end])
    z_rest = []

    def project_rest():
        for lo in range(a_end, D_IN, MXU_WIDTH):
            z_rest.append(_dot(h, win_ref[:, lo:lo + MXU_WIDTH]))
            yield

    pieces = _round_robin(project_rest(), overlap)

    cbuf[HIST:HIST + ts, :] = z_a[:, :D_CONV] * jax.nn.sigmoid(z_a[:, D_CONV:])
    span = ts + HIST - SUBLANES
    for r in range(1, SUBLANES):
        cshift[r - 1, 0:span, :] = cbuf[r:r + span, :]
    ya_blocks = []
    n_blocks = ts // CONV_RB
    for rb in range(n_blocks):
        acc = jnp.zeros((CONV_RB, D_CONV), F32)
        for k in range(CONV_WIDTH):
            start = rb * CONV_RB + HIST - (CONV_WIDTH - 1) + k
            r, base = start % SUBLANES, start - start % SUBLANES
            src = cbuf[base:base + CONV_RB, :] if r == 0 else cshift[r - 1, base:base + CONV_RB, :]
            acc = acc + convw_ref[k:k + 1, :] * src
        ya_blocks.append(acc)
        for _ in range(2 if rb < n_blocks // 2 else 1):
            next(pieces, None)
    for _ in pieces:
        pass
    ya = jnp.concatenate(ya_blocks, axis=0) + convb_ref[...]
    ya = _layer_norm(ya, clng_ref[...], clnb_ref[...])
    ya = ya * jax.nn.sigmoid(ya)
    cbuf[0:HIST, :] = cbuf[ts:ts + HIST, :]
    out = x + _dot(ya.astype(BF16), wout_ref[0:D_CONV, :])

    z = jnp.concatenate(z_rest, axis=1)
    zb = jax.nn.gelu(z[:, :b_end - a_end])
    u = zb[:, :D_SGU]
    v = _layer_norm(zb[:, D_SGU:], slng_ref[...], slnb_ref[...]).astype(BF16)
    blk_r = lax.broadcasted_iota(jnp.int32, (SGU_CHUNK, SGU_CHUNK), 0) // CHUNK
    blk_c = lax.broadcasted_iota(jnp.int32, (SGU_CHUNK, SGU_CHUNK), 1) // CHUNK
    head_of_lane = lax.broadcasted_iota(jnp.int32, (SGU_CHUNK, D_SGU), 1) // (D_SGU // SGU_HEADS)
    w_heads = [jnp.where(blk_r >= blk_c, sguw_ref[hd], 0.0).astype(BF16) for hd in range(SGU_HEADS)]
    s_chunks = []
    for c in range(ts // SGU_CHUNK):
        vc = v[c * SGU_CHUNK:(c + 1) * SGU_CHUNK, :]
        sc = jnp.zeros((SGU_CHUNK, D_SGU), F32)
        for hd in range(SGU_HEADS):
            sc = jnp.where(head_of_lane == hd, _dot(w_heads[hd], vc), sc)
        s_chunks.append(sc + sgub_ref[...])
    yb = u * jnp.concatenate(s_chunks, axis=0)
    out = out + _dot(yb.astype(BF16), wout_ref[D_CONV:D_CONV + D_SGU, :])

    cc = z[:, b_end - a_end:]
    pb0[HIST:HIST + ts, :] = cc
    pb1[0:ts + 24, :] = pb0[8:ts + 32, :] + pb0[7:ts + 31, :]
    pb2[0:ts + 16, :] = pb1[8:ts + 24, :] + pb1[6:ts + 22, :]
    pb3[0:ts + 8, :] = pb2[8:ts + 16, :] + pb2[4:ts + 12, :]
    s16 = pb3[8:ts + 8, :] + pb3[0:ts, :]
    s8 = pb3[8:ts + 8, :]
    s4 = pb2[16:ts + 16, :]
    s2 = pb1[24:ts + 24, :]
    grp = lax.broadcasted_iota(jnp.int32, (ts, D_POOL), 1) // POOL_GROUP_DIM
    wsum = jnp.where(grp == 0, s2, jnp.where(grp == 1, s4, jnp.where(grp == 2, s8, s16)))
    win = jnp.where(grp == 0, 2, jnp.where(grp == 1, 4, jnp.where(grp == 2, 8, 16)))
    pos = s * ts + lax.broadcasted_iota(jnp.int32, (ts, D_POOL), 0)
    cnt = jnp.minimum(pos + 1, win).astype(F32)
    p = (wsum / cnt - cc).astype(BF16)
    yc = (_dot(p, poolw_ref[...]) + poolb_ref[...]) * pscale_ref[...]
    pb0[0:HIST, :] = pb0[ts:ts + HIST, :]

    return out + _dot(yc.astype(BF16), wout_ref[D_CONV + D_SGU:D_MIX, :])


def _const_spec(shape):
    zeros = (0,) * len(shape)
    return pl.BlockSpec(shape, lambda *_: zeros)


N_MIX_PARAMS = 14
N_ATT_PARAMS = 5


def _mix_attn_kernel(x_ref, *refs, n_tiles, tiles_per_row):
    mix_refs = refs[:N_MIX_PARAMS]
    att_refs = refs[N_MIX_PARAMS:N_MIX_PARAMS + N_ATT_PARAMS]
    o_ref, mid, cbuf, cshift, pb0, pb1, pb2, pb3 = refs[N_MIX_PARAMS + N_ATT_PARAMS:]
    i = pl.program_id(0)
    s = jnp.minimum(i, n_tiles - 1) % tiles_per_row

    @pl.when(i == 0)
    def _():
        mid[...] = jnp.zeros_like(mid)

    @pl.when(s == 0)
    def _():
        cbuf[0:HIST, :] = jnp.zeros((HIST, D_CONV), F32)
        pb0[0:HIST, :] = jnp.zeros((HIST, D_POOL), F32)

    attention = _xattn_pieces(mid[...], o_ref, *att_refs)
    mid[...] = _mixer_tile(x_ref[0], s, *mix_refs, cbuf, cshift, pb0, pb1, pb2, pb3, overlap=attention)


def _mix_attn(x, mix_params, k, v, att_params):
    B, S, D = x.shape
    M = k.shape[1]
    ts = MIX_TS
    tiles_per_row = S // ts
    n_tiles = B * tiles_per_row
    g_att, wq, wo = att_params
    row = lambda a: a.reshape(1, -1) if a.ndim == 1 else a
    mix_args = [row(a) for a in mix_params]
    att_args = [row(g_att), wq, k, v, wo]
    assert len(mix_args) == N_MIX_PARAMS and len(att_args) == N_ATT_PARAMS

    def mix_tile(i):
        t = jnp.minimum(i, n_tiles - 1)
        return t // tiles_per_row, t % tiles_per_row, 0

    def att_tile(i):
        t = jnp.maximum(i - 1, 0)
        return t // tiles_per_row, t % tiles_per_row, 0

    single = lambda a: pl.BlockSpec(a.shape, lambda i, nd=a.ndim: (0,) * nd, pipeline_mode=pl.Buffered(1))
    kv_spec = pl.BlockSpec((1, M, D), lambda i: (att_tile(i)[0], 0, 0))
    in_specs = [pl.BlockSpec((1, ts, D), mix_tile)] + [single(a) for a in mix_args]
    in_specs += [single(att_args[0]), single(wq), kv_spec, kv_spec, single(wo)]
    return pl.pallas_call(
        functools.partial(_mix_attn_kernel, n_tiles=n_tiles, tiles_per_row=tiles_per_row),
        grid=(n_tiles + 1,),
        in_specs=in_specs,
        out_specs=pl.BlockSpec((1, ts, D), att_tile),
        out_shape=jax.ShapeDtypeStruct((B, S, D), F32),
        scratch_shapes=[pltpu.VMEM((ts, D), F32),
                        pltpu.VMEM((ts + HIST, D_CONV), F32),
                        pltpu.VMEM((SUBLANES - 1, ts + HIST - SUBLANES, D_CONV), F32)]
                       + [pltpu.VMEM((ts + HIST, D_POOL), F32)] * 4,
        compiler_params=pltpu.CompilerParams(
            dimension_semantics=("arbitrary",), vmem_limit_bytes=VMEM_LIMIT),
        name="mix_attn",
    )(x, *mix_args, *att_args)


def _kv_kernel(mem_ref, g_ref, wk_ref, wv_ref, k_ref, v_ref):
    m = _rms(mem_ref[0], g_ref[...]).astype(BF16)
    k_ref[0] = _dot(m, wk_ref[...]).astype(BF16)
    v_ref[0] = _dot(m, wv_ref[...]).astype(BF16)


def _kv(mem, g, wk, wv):
    B, M, D = mem.shape
    return pl.pallas_call(
        _kv_kernel,
        grid=(B,),
        in_specs=[pl.BlockSpec((1, M, D), lambda b: (b, 0, 0)), _const_spec((1, D)),
                  _const_spec((D, D)), _const_spec((D, D))],
        out_specs=[pl.BlockSpec((1, M, D), lambda b: (b, 0, 0))] * 2,
        out_shape=[jax.ShapeDtypeStruct((B, M, D), BF16)] * 2,
        compiler_params=pltpu.CompilerParams(
            dimension_semantics=("arbitrary",), vmem_limit_bytes=VMEM_LIMIT),
        name="mem_kv",
    )(mem, g.reshape(1, -1), wk, wv)


def _xattn_pieces(x, o_ref, g_ref, wq_ref, k_ref, v_ref, wo_ref):
    h = _rms(x, g_ref[...]).astype(BF16)
    heads = []
    for hd in range(X_HEADS):
        sl = slice(hd * X_HEAD_DIM, (hd + 1) * X_HEAD_DIM)
        q = _dot(h, wq_ref[:, sl]).astype(BF16)
        sc = lax.dot_general(q, k_ref[0, :, sl], (((1,), (1,)), ((), ())),
                             preferred_element_type=F32) * (X_HEAD_DIM ** -0.5)
        e = jnp.exp(sc - jnp.max(sc, axis=-1, keepdims=True))
        heads.append(_dot(e.astype(BF16), v_ref[0, :, sl]) / jnp.sum(e, axis=-1, keepdims=True))
        yield
    o = jnp.concatenate(heads, axis=-1).astype(BF16)
    for c in range(X_HEADS):
        sl = slice(c * X_HEAD_DIM, (c + 1) * X_HEAD_DIM)
        o_ref[0, :, sl] = x[:, sl] + _dot(o, wo_ref[:, sl])
        yield


def _store_tiles(ref, val):
    for j in range(ref.shape[0]):
        ref[j] = val[:, j * LANES:(j + 1) * LANES]


def _load_tiles(ref):
    return jnp.concatenate([ref[j] for j in range(ref.shape[0])], axis=1)


def _pack_bf16_pairs(v):
    bits = lax.bitcast_convert_type(v.astype(BF16).astype(F32), U32)
    half = bits.shape[1] // 2
    return (bits[:, :half] & jnp.uint32(HI16)) | (bits[:, half:] >> 16)


def _unpack_bf16_pairs(w):
    hi = lax.bitcast_convert_type(w & jnp.uint32(HI16), F32)
    lo = lax.bitcast_convert_type(w << 16, F32)
    return jnp.concatenate([hi, lo], axis=1)


def _tiled_shape(rows, width, dtype):
    return jax.ShapeDtypeStruct((width // LANES, rows, LANES), dtype)


def _tiled_spec(rows, width, row_block, lead=None):
    block = (width // LANES, rows, LANES)
    if lead is None:
        return pl.BlockSpec(block, lambda *a: (0, row_block(*a), 0))
    return pl.BlockSpec((1,) + block, lambda *a: (lead, 0, row_block(*a), 0))


def _piece_index(rows, k, n_rows):
    return jnp.arange(k, dtype=I32)[:, None] * n_rows + rows[..., None, :]


META_I1, META_I2, META_G1, META_G2, META_R1, META_R2 = range(6)


def _router_kernel(x_ref, g_ref, rw_ref, triu_ref, meta_ref, hp_ref, cnt_ref, carry):
    @pl.when(pl.program_id(0) == 0)
    def _():
        carry[...] = jnp.zeros_like(carry)

    h = _rms(x_ref[...], g_ref[...])
    h_hi = h.astype(BF16)
    h_lo = (h - h_hi.astype(F32)).astype(BF16)
    logits = _dot(jnp.concatenate([h_hi, h_hi, h_lo], axis=1), rw_ref[...])

    neg = jnp.float32(-jnp.inf)
    lane = lax.broadcasted_iota(I32, logits.shape, 1)
    lt = jnp.where(lane < N_EXPERTS, logits, neg).T[:N_EXPERTS, :]
    expert = lax.broadcasted_iota(I32, lt.shape, 0)
    m1 = jnp.max(lt, axis=0, keepdims=True)
    i1 = jnp.min(jnp.where(lt == m1, expert, N_EXPERTS), axis=0, keepdims=True)
    lt2 = jnp.where(expert == i1, neg, lt)
    m2 = jnp.max(lt2, axis=0, keepdims=True)
    i2 = jnp.min(jnp.where(lt2 == m2, expert, N_EXPERTS), axis=0, keepdims=True)
    d = jnp.exp(m2 - m1)
    g1 = 1.0 / (1.0 + d)
    g2 = d / (1.0 + d)

    sel1 = expert == i1
    sel2 = expert == i2
    onehot = jnp.where(jnp.logical_or(sel1, sel2), 1.0, 0.0)
    before = _dot(onehot, triu_ref[...]) + carry[:, 0:1]
    r1 = jnp.sum(jnp.where(sel1, before, 0.0), axis=0, keepdims=True)
    r2 = jnp.sum(jnp.where(sel2, before, 0.0), axis=0, keepdims=True)
    carry[...] += jnp.sum(onehot, axis=1, keepdims=True)
    cnt_ref[...] = carry[...]

    rows = [i1.astype(F32), i2.astype(F32), g1, g2, r1, r2]
    meta_ref[...] = jnp.concatenate(rows + [jnp.zeros_like(g1)] * (SUBLANES - len(rows)), axis=0)
    _store_tiles(hp_ref, _pack_bf16_pairs(h))


def _router(x2, g, router_w, first_row, n_rows):
    D = x2.shape[1]
    tm = FFN_TM
    first_block = first_row // tm
    assert N_EXPERTS == SUBLANES
    rw = jnp.zeros((D, LANES), F32).at[:, :N_EXPERTS].set(router_w)
    rw_hi = rw.astype(BF16)
    rw_lo = (rw - rw_hi.astype(F32)).astype(BF16)
    rw_split = jnp.concatenate([rw_hi, rw_lo, rw_hi], axis=0)
    triu = jnp.triu(jnp.ones((tm, tm), F32), 1)
    return pl.pallas_call(
        _router_kernel,
        grid=(n_rows // tm,),
        in_specs=[pl.BlockSpec((tm, D), lambda i: (first_block + i, 0)), _const_spec((1, D)),
                  _const_spec((3 * D, LANES)), _const_spec((tm, tm))],
        out_specs=[pl.BlockSpec((SUBLANES, tm), lambda i: (0, i)),
                   _tiled_spec(tm, D // 2, lambda i: i), _const_spec((N_EXPERTS, LANES))],
        out_shape=[jax.ShapeDtypeStruct((SUBLANES, n_rows), F32),
                   _tiled_shape(n_rows, D // 2, U32), jax.ShapeDtypeStruct((N_EXPERTS, LANES), F32)],
        scratch_shapes=[pltpu.VMEM((N_EXPERTS, LANES), F32)],
        compiler_params=pltpu.CompilerParams(
            dimension_semantics=("arbitrary",), vmem_limit_bytes=VMEM_LIMIT),
        name="router",
    )(x2, g.reshape(1, -1), rw_split, triu)


def _sc_mesh():
    return plsc.VectorSubcoreMesh(core_axis_name="core", subcore_axis_name="subcore")


def _sc_dispatch(tiled, dest1, dest2, n_out):
    k, n, _ = tiled.shape
    d1, d2 = (_piece_index(d, k, n_out).reshape(-1) for d in (dest1, dest2))
    return _sc_scatter_pieces(tiled.reshape(k * n, LANES), d1, d2, k * n_out).reshape(k, n_out, LANES)


def _sc_gather(tiled, idx):
    k, n_rows, _ = tiled.shape
    G, n = idx.shape
    out = _sc_gather_pieces(tiled.reshape(k * n_rows, LANES), _piece_index(idx, k, n_rows).reshape(-1))
    return out.reshape(G, k, n, LANES)


def _sc_scatter_pieces(rows, dest1, dest2, n_out):
    T, W = rows.shape
    win = SC_WIN

    @pl.kernel(out_type=jax.ShapeDtypeStruct((n_out, W), rows.dtype), mesh=_sc_mesh(), scratch_types=[])
    def scatter_kernel(x_hbm, i1_hbm, i2_hbm, o_hbm):
        def body(x_vmem, i1_vmem, i2_vmem):
            pltpu.sync_copy(x_vmem, o_hbm.at[i1_vmem.at[0]])
            pltpu.sync_copy(x_vmem, o_hbm.at[i2_vmem.at[0]])

        pltpu.emit_pipeline(
            body,
            grid=(T // win,),
            in_specs=[pl.BlockSpec((win, W), lambda i: (i, 0)),
                      pl.BlockSpec((1, win), lambda i: (0, i)),
                      pl.BlockSpec((1, win), lambda i: (0, i))],
            out_specs=[],
            core_axis_name=("core", "subcore"),
            dimension_semantics=(pltpu.PARALLEL,),
        )(x_hbm, i1_hbm, i2_hbm)

    return scatter_kernel(rows, dest1.reshape(1, T), dest2.reshape(1, T))


def _sc_gather_pieces(rows, idx):
    n = idx.shape[0]
    W = rows.shape[1]
    win = SC_WIN

    @pl.kernel(out_type=jax.ShapeDtypeStruct((n, W), rows.dtype), mesh=_sc_mesh(), scratch_types=[])
    def gather_kernel(x_hbm, i_hbm, o_hbm):
        def body(i_vmem, o_vmem):
            pltpu.sync_copy(x_hbm.at[i_vmem.at[0]], o_vmem)

        pltpu.emit_pipeline(
            body,
            grid=(n // win,),
            in_specs=[pl.BlockSpec((1, win), lambda i: (0, i))],
            out_specs=[pl.BlockSpec((win, W), lambda i: (i, 0))],
            core_axis_name=("core", "subcore"),
            dimension_semantics=(pltpu.PARALLEL,),
        )(i_hbm, o_hbm)

    return gather_kernel(rows, idx.reshape(1, n))


def _swiglu(h, wg, wu, wd):
    gate = _dot(h, wg)
    up = _dot(h, wu)
    return _dot((gate * jax.nn.sigmoid(gate) * up).astype(BF16), wd)


def _swiglu_chunked(h, wg_ref, wu_ref, wd_ref, acc=None):
    F = wg_ref.shape[1]
    n_chunks = -(-F // FFN_FC)
    cols = -(-F // (n_chunks * MXU_WIDTH)) * MXU_WIDTH
    for lo in range(0, F, cols):
        sl = slice(lo, min(lo + cols, F))
        y = _swiglu(h, wg_ref[:, sl], wu_ref[:, sl], wd_ref[sl, :])
        acc = y if acc is None else acc + y
    return acc


def _ffn_kernel(x_ref, g_ref, wg_ref, wu_ref, wd_ref, *refs):
    n_cast = len(refs) // 2
    o_ref = refs[n_cast]
    x = x_ref[...]
    h = _rms(x, g_ref[...]).astype(BF16)
    o_ref[...] = _swiglu_chunked(h, wg_ref, wu_ref, wd_ref, acc=x)
    for src_ref, dst_ref in zip(refs[:n_cast], refs[n_cast + 1:]):
        dst_ref[...] = src_ref[...].astype(BF16)


def _ffn(x2, g, wg, wu, wd, cast_weights=()):
    T, D = x2.shape
    F = wg.shape[1]
    tm = FFN_TM
    n_steps = T // tm
    single = lambda shape: pl.BlockSpec(shape, lambda i: (0,) * len(shape), pipeline_mode=pl.Buffered(1))
    cast_specs = []
    for w in cast_weights:
        E, rows, cols = w.shape
        per_expert = n_steps // E
        assert n_steps % E == 0 and rows % (per_expert * 2 * SUBLANES) == 0
        cast_specs.append(pl.BlockSpec((1, rows // per_expert, cols),
                                       lambda i, per_expert=per_expert: (i // per_expert, i % per_expert, 0)))
    out, *cast = pl.pallas_call(
        _ffn_kernel,
        grid=(n_steps,),
        in_specs=[pl.BlockSpec((tm, D), lambda i: (i, 0)), single((1, D)),
                  single((D, F)), single((D, F)), single((F, D))] + cast_specs,
        out_specs=[pl.BlockSpec((tm, D), lambda i: (i, 0))] + cast_specs,
        out_shape=[jax.ShapeDtypeStruct((T, D), F32)] + [jax.ShapeDtypeStruct(w.shape, BF16) for w in cast_weights],
        compiler_params=pltpu.CompilerParams(
            dimension_semantics=("arbitrary",), vmem_limit_bytes=VMEM_LIMIT),
        name="swiglu",
    )(x2, g.reshape(1, -1), wg, wu, wd, *cast_weights)
    return out, cast


def _moe_kernel(tidx_ref, texp_ref, nvalid_ref, xs_ref, wg_ref, wu_ref, wd_ref, y_ref):
    del tidx_ref, texp_ref

    @pl.when(pl.program_id(0) < nvalid_ref[0])
    def _():
        h = _unpack_bf16_pairs(_load_tiles(xs_ref)).astype(BF16)
        _store_tiles(y_ref, _pack_bf16_pairs(_swiglu_chunked(h, wg_ref.at[0], wu_ref.at[0], wd_ref.at[0])))


def _moe_experts(xs, tile_idx, tile_expert, n_valid, wg, wu, wd):
    E, D, F = wg.shape
    tm = MOE_TM
    P = xs.shape[1]
    half = xs.shape[0] * LANES
    row_block = lambda g, ti, te, nv: ti[g]
    w_map = lambda g, ti, te, nv: (te[g], 0, 0)
    return pl.pallas_call(
        _moe_kernel,
        grid_spec=pltpu.PrefetchScalarGridSpec(
            num_scalar_prefetch=3,
            grid=(P // tm,),
            in_specs=[_tiled_spec(tm, half, row_block),
                      pl.BlockSpec((1, D, F), w_map), pl.BlockSpec((1, D, F), w_map),
                      pl.BlockSpec((1, F, D), w_map)],
            out_specs=_tiled_spec(tm, D // 2, row_block)),
        out_shape=_tiled_shape(P, D // 2, U32),
        compiler_params=pltpu.CompilerParams(
            dimension_semantics=("arbitrary",), vmem_limit_bytes=VMEM_LIMIT),
        name="moe_experts",
    )(tile_idx, tile_expert, n_valid, xs, wg, wu, wd)


def _combine_kernel(x_ref, y1_ref, y2_ref, meta_ref, fg_ref, *refs, final_norm):
    o_ref = refs[-1]
    rec = meta_ref[...]
    rec_t = jnp.concatenate([rec, jnp.zeros((LANES - rec.shape[0], rec.shape[1]), F32)], axis=0).T
    lane = lax.broadcasted_iota(I32, rec_t.shape, 1)
    gate = lambda row: jnp.sum(jnp.where(lane == row, rec_t, 0.0), axis=-1, keepdims=True)
    out = x_ref[...] + gate(META_G1) * _unpack_bf16_pairs(_load_tiles(y1_ref.at[0]))
    out = out + gate(META_G2) * _unpack_bf16_pairs(_load_tiles(y2_ref.at[0]))
    if final_norm:
        out = _rms(out, fg_ref[...])
    o_ref[...] = out


def _combine(x2, y12, meta, final_g, first_row, partial_out, *, final_norm):
    T, D = x2.shape
    n_rows = meta.shape[1]
    tm = FFN_TM
    first_block = first_row // tm
    rows = lambda i: (first_block + i, 0)
    args = [x2, y12, y12, meta, final_g.reshape(1, -1)]
    in_specs = [pl.BlockSpec((tm, D), rows),
                _tiled_spec(tm, D // 2, lambda i: i, lead=0), _tiled_spec(tm, D // 2, lambda i: i, lead=1),
                pl.BlockSpec((SUBLANES, tm), lambda i: (0, i)), _const_spec((1, D))]
    aliases = {}
    if partial_out is not None:
        args.append(partial_out)
        in_specs.append(pl.BlockSpec(memory_space=pl.ANY))
        aliases = {len(args) - 1: 0}
    return pl.pallas_call(
        functools.partial(_combine_kernel, final_norm=final_norm),
        grid=(n_rows // tm,),
        in_specs=in_specs,
        out_specs=pl.BlockSpec((tm, D), rows),
        out_shape=jax.ShapeDtypeStruct((T, D), F32),
        input_output_aliases=aliases,
        compiler_params=pltpu.CompilerParams(
            dimension_semantics=("arbitrary",), vmem_limit_bytes=VMEM_LIMIT),
        name="moe_combine",
    )(*args)


def _route_and_dispatch(x2, g, router_w, first_row, n_rows):
    tm = MOE_TM
    n_tiles = (n_rows * TOP_K) // tm + N_EXPERTS
    meta, hp, counts = _router(x2, g, router_w, first_row, n_rows)

    cnt = counts[:, 0].astype(I32)
    tiles_e = (cnt + tm - 1) // tm
    tile_end = jnp.cumsum(tiles_e)
    row_start = (tile_end - tiles_e) * tm
    experts = jnp.arange(N_EXPERTS, dtype=I32)[:, None]

    def dest(i_row, r_row):
        start = jnp.sum(jnp.where(meta[i_row].astype(I32) == experts, row_start[:, None], 0), axis=0)
        return start + meta[r_row].astype(I32)

    dests = jnp.stack([dest(META_I1, META_R1), dest(META_I2, META_R2)])
    n_valid = tile_end[-1:]
    tile_idx = jnp.minimum(jnp.arange(n_tiles, dtype=I32), n_valid - 1)
    tile_expert = jnp.minimum(jnp.sum(tile_end[:, None] <= tile_idx, axis=0), N_EXPERTS - 1).astype(I32)
    xs = _sc_dispatch(hp, dests[0], dests[1], n_tiles * tm)
    return xs, (tile_idx, tile_expert, n_valid.astype(I32)), dests, meta


def _moe_layer(x2, g, router_w, wg, wu, wd, final_g, *, final_norm):
    T = x2.shape[0]
    n_rows = T // MOE_ROW_GROUPS
    starts = [grp * n_rows for grp in range(MOE_ROW_GROUPS)]
    routed = [_route_and_dispatch(x2, g, router_w, first_row, n_rows) for first_row in starts]
    ys = [_moe_experts(xs, *tiles, wg, wu, wd) for xs, tiles, _, _ in routed]
    y12s = [_sc_gather(y, dests) for y, (_, _, dests, _) in zip(ys, routed)]
    out = None
    for first_row, y12, (_, _, _, meta) in zip(starts, y12s, routed):
        out = _combine(x2, y12, meta, final_g, first_row, out, final_norm=final_norm)
    return out


def kernel(x, mem, mix_norm_g, w_in, conv_w, conv_b, conv_ln_g, conv_ln_b, sgu_ln_g, sgu_ln_b, sgu_w, sgu_b,
           pool_w, pool_b, pool_scale, w_out, xattn_norm_g, mem_norm_g, xattn_wq, xattn_wk, xattn_wv, xattn_wo,
           ffn_norm_g, ffn_wg, ffn_wu, ffn_wd, router_w, moe_wg, moe_wu, moe_wd, final_norm_g):
    B, S, D = x.shape
    bf = lambda a: a.astype(BF16)
    layer_mats = dict(w_in=w_in, w_out=w_out, wq=xattn_wq, wk=xattn_wk, wv=xattn_wv, wo=xattn_wo)
    precast = {}

    def mat(name, l):
        return precast[name, l] if (name, l) in precast else bf(layer_mats[name][l])

    for l in range(DEPTH):
        sgu_bias = jnp.repeat(sgu_b[l].T, D_SGU // SGU_HEADS, axis=1)
        pool_wbd = jax.scipy.linalg.block_diag(*[pool_w[l, gi] for gi in range(len(POOL_WINDOWS))])
        mix_params = (mix_norm_g[l], mat("w_in", l), conv_w[l], conv_b[l], conv_ln_g[l], conv_ln_b[l],
                      sgu_ln_g[l], sgu_ln_b[l], sgu_w[l], sgu_bias, bf(pool_wbd), pool_b[l].reshape(-1),
                      pool_scale[l], mat("w_out", l))
        k, v = _kv(mem, mem_norm_g[l], mat("wk", l), mat("wv", l))
        x = _mix_attn(x, mix_params, k, v, (xattn_norm_g[l], mat("wq", l), mat("wo", l)))
        x2 = x.reshape(B * S, D)
        j = l // 2
        if l % 2 == 0:
            assert l != DEPTH - 1, "the final RMSNorm is fused into the routed layer's combine kernel"
            names = list(layer_mats)
            side = [layer_mats[n][l + 1].reshape(N_EXPERTS, -1, layer_mats[n].shape[-1]) for n in names]
            x2, cast = _ffn(x2, ffn_norm_g[l], bf(ffn_wg[j]), bf(ffn_wu[j]), bf(ffn_wd[j]),
                            cast_weights=(moe_wg[j], moe_wu[j], moe_wd[j], *side))
            moe_bf16 = cast[:3]
            for n, c in zip(names, cast[3:]):
                precast[n, l + 1] = c.reshape(layer_mats[n].shape[1:])
        else:
            x2 = _moe_layer(x2, ffn_norm_g[l], router_w[j], *moe_bf16, final_norm_g, final_norm=l == DEPTH - 1)
        x = x2.reshape(B, S, D)
    return x
```

```python
import functools

import jax
import jax.numpy as jnp
from jax import lax
from jax.experimental import pallas as pl
from jax.experimental.pallas import tpu as pltpu
from jax.experimental.pallas import tpu_sc as plsc

F32 = jnp.float32
BF16 = jnp.bfloat16
U32 = jnp.uint32
I32 = jnp.int32

D_MODEL = 1024
DEPTH = 2
CHUNK = 64
D_CONV = 384
CONV_WIDTH = 31
D_SGU = 384
SGU_HEADS = 4
SGU_CHUNK = 128
D_POOL = 256
POOL_WINDOWS = (2, 4, 8, 16)
POOL_GROUP_DIM = D_POOL // len(POOL_WINDOWS)
D_MIX = D_CONV + D_SGU + D_POOL
D_IN = 2 * D_CONV + 2 * D_SGU + D_POOL
X_HEADS = 4
X_HEAD_DIM = D_MODEL // X_HEADS
N_EXPERTS = 8
TOP_K = 2
EPS = 1e-6

LANES = 128
SUBLANES = 8
HIST = 32
MIX_TS = 512
CONV_RB = 64
FFN_TM = 512
FFN_FC = 1408
MXU_WIDTH = 256
MOE_TM = 512
MOE_ROW_GROUPS = 2
SC_WIN = 128
HI16 = 0xFFFF0000
VMEM_LIMIT = 56 * 1024 * 1024


def _rms(x, g):
    return x * lax.rsqrt(jnp.mean(x * x, axis=-1, keepdims=True) + EPS) * g


def _layer_norm(x, g, b):
    mu = jnp.mean(x, axis=-1, keepdims=True)
    xc = x - mu
    var = jnp.mean(xc * xc, axis=-1, keepdims=True)
    return xc * lax.rsqrt(var + EPS) * g + b


def _dot(a, b):
    return jnp.dot(a, b, preferred_element_type=F32)


def _round_robin(*generators):
    live = list(generators)
    while live:
        for gen in list(live):
            try:
                next(gen)
                yield
            except StopIteration:
                live.remove(gen)


def _mixer_tile(x, s, g_ref, win_ref, convw_ref, convb_ref, clng_ref, clnb_ref,
                slng_ref, slnb_ref, sguw_ref, sgub_ref, poolw_ref, poolb_ref, pscale_ref,
                wout_ref, cbuf, cshift, pb0, pb1, pb2, pb3, overlap):
    ts = MIX_TS
    a_end = 2 * D_CONV
    b_end = a_end + 2 * D_SGU
    h = _rms(x, g_ref[...]).astype(BF16)
    z_a = _dot(h, win_ref[:, :a_end])
    z_rest = []

    def project_rest():
        for lo in range(a_end, D_IN, MXU_WIDTH):
            z_rest.append(_dot(h, win_ref[:, lo:lo + MXU_WIDTH]))
            yield

    pieces = _round_robin(project_rest(), overlap)

    cbuf[HIST:HIST + ts, :] = z_a[:, :D_CONV] * jax.nn.sigmoid(z_a[:, D_CONV:])
    span = ts + HIST - SUBLANES
    for r in range(1, SUBLANES):
        cshift[r - 1, 0:span, :] = cbuf[r:r + span, :]
    ya_blocks = []
    n_blocks = ts // CONV_RB
    for rb in range(n_blocks):
        acc = jnp.zeros((CONV_RB, D_CONV), F32)
        for k in range(CONV_WIDTH):
            start = rb * CONV_RB + HIST - (CONV_WIDTH - 1) + k
            r, base = start % SUBLANES, start - start % SUBLANES
            src = cbuf[base:base + CONV_RB, :] if r == 0 else cshift[r - 1, base:base + CONV_RB, :]
            acc = acc + convw_ref[k:k + 1, :] * src
        ya_blocks.append(acc)
        for _ in range(2 if rb < n_blocks // 2 else 1):
            next(pieces, None)
    for _ in pieces:
        pass
    ya = jnp.concatenate(ya_blocks, axis=0) + convb_ref[...]
    ya = _layer_norm(ya, clng_ref[...], clnb_ref[...])
    ya = ya * jax.nn.sigmoid(ya)
    cbuf[0:HIST, :] = cbuf[ts:ts + HIST, :]
    out = x + _dot(ya.astype(BF16), wout_ref[0:D_CONV, :])

    z = jnp.concatenate(z_rest, axis=1)
    zb = jax.nn.gelu(z[:, :b_end - a_end])
    u = zb[:, :D_SGU]
    v = _layer_norm(zb[:, D_SGU:], slng_ref[...], slnb_ref[...]).astype(BF16)
    blk_r = lax.broadcasted_iota(jnp.int32, (SGU_CHUNK, SGU_CHUNK), 0) // CHUNK
    blk_c = lax.broadcasted_iota(jnp.int32, (SGU_CHUNK, SGU_CHUNK), 1) // CHUNK
    head_of_lane = lax.broadcasted_iota(jnp.int32, (SGU_CHUNK, D_SGU), 1) // (D_SGU // SGU_HEADS)
    w_heads = [jnp.where(blk_r >= blk_c, sguw_ref[hd], 0.0).astype(BF16) for hd in range(SGU_HEADS)]
    s_chunks = []
    for c in range(ts // SGU_CHUNK):
        vc = v[c * SGU_CHUNK:(c + 1) * SGU_CHUNK, :]
        sc = jnp.zeros((SGU_CHUNK, D_SGU), F32)
        for hd in range(SGU_HEADS):
            sc = jnp.where(head_of_lane == hd, _dot(w_heads[hd], vc), sc)
        s_chunks.append(sc + sgub_ref[...])
    yb = u * jnp.concatenate(s_chunks, axis=0)
    out = out + _dot(yb.astype(BF16), wout_ref[D_CONV:D_CONV + D_SGU, :])

    cc = z[:, b_end - a_end:]
    pb0[HIST:HIST + ts, :] = cc
    pb1[0:ts + 24, :] = pb0[8:ts + 32, :] + pb0[7:ts + 31, :]
    pb2[0:ts + 16, :] = pb1[8:ts + 24, :] + pb1[6:ts + 22, :]
    pb3[0:ts + 8, :] = pb2[8:ts + 16, :] + pb2[4:ts + 12, :]
    s16 = pb3[8:ts + 8, :] + pb3[0:ts, :]
    s8 = pb3[8:ts + 8, :]
    s4 = pb2[16:ts + 16, :]
    s2 = pb1[24:ts + 24, :]
    grp = lax.broadcasted_iota(jnp.int32, (ts, D_POOL), 1) // POOL_GROUP_DIM
    wsum = jnp.where(grp == 0, s2, jnp.where(grp == 1, s4, jnp.where(grp == 2, s8, s16)))
    win = jnp.where(grp == 0, 2, jnp.where(grp == 1, 4, jnp.where(grp == 2, 8, 16)))
    pos = s * ts + lax.broadcasted_iota(jnp.int32, (ts, D_POOL), 0)
    cnt = jnp.minimum(pos + 1, win).astype(F32)
    p = (wsum / cnt - cc).astype(BF16)
    yc = (_dot(p, poolw_ref[...]) + poolb_ref[...]) * pscale_ref[...]
    pb0[0:HIST, :] = pb0[ts:ts + HIST, :]

    return out + _dot(yc.astype(BF16), wout_ref[D_CONV + D_SGU:D_MIX, :])


def _const_spec(shape):
    zeros = (0,) * len(shape)
    return pl.BlockSpec(shape, lambda *_: zeros)


N_MIX_PARAMS = 14
N_ATT_PARAMS = 5


def _mix_attn_kernel(x_ref, *refs, n_tiles, tiles_per_row):
    mix_refs = refs[:N_MIX_PARAMS]
    att_refs = refs[N_MIX_PARAMS:N_MIX_PARAMS + N_ATT_PARAMS]
    o_ref, mid, cbuf, cshift, pb0, pb1, pb2, pb3 = refs[N_MIX_PARAMS + N_ATT_PARAMS:]
    i = pl.program_id(0)
    s = jnp.minimum(i, n_tiles - 1) % tiles_per_row

    @pl.when(i == 0)
    def _():
        mid[...] = jnp.zeros_like(mid)

    @pl.when(s == 0)
    def _():
        cbuf[0:HIST, :] = jnp.zeros((HIST, D_CONV), F32)
        pb0[0:HIST, :] = jnp.zeros((HIST, D_POOL), F32)

    attention = _xattn_pieces(mid[...], o_ref, *att_refs)
    mid[...] = _mixer_tile(x_ref[0], s, *mix_refs, cbuf, cshift, pb0, pb1, pb2, pb3, overlap=attention)


def _mix_attn(x, mix_params, k, v, att_params):
    B, S, D = x.shape
    M = k.shape[1]
    ts = MIX_TS
    tiles_per_row = S // ts
    n_tiles = B * tiles_per_row
    g_att, wq, wo = att_params
    row = lambda a: a.reshape(1, -1) if a.ndim == 1 else a
    mix_args = [row(a) for a in mix_params]
    att_args = [row(g_att), wq, k, v, wo]
    assert len(mix_args) == N_MIX_PARAMS and len(att_args) == N_ATT_PARAMS

    def mix_tile(i):
        t = jnp.minimum(i, n_tiles - 1)
        return t // tiles_per_row, t % tiles_per_row, 0

    def att_tile(i):
        t = jnp.maximum(i - 1, 0)
        return t // tiles_per_row, t % tiles_per_row, 0

    single = lambda a: pl.BlockSpec(a.shape, lambda i, nd=a.ndim: (0,) * nd, pipeline_mode=pl.Buffered(1))
    kv_spec = pl.BlockSpec((1, M, D), lambda i: (att_tile(i)[0], 0, 0))
    in_specs = [pl.BlockSpec((1, ts, D), mix_tile)] + [single(a) for a in mix_args]
    in_specs += [single(att_args[0]), single(wq), kv_spec, kv_spec, single(wo)]
    return pl.pallas_call(
        functools.partial(_mix_attn_kernel, n_tiles=n_tiles, tiles_per_row=tiles_per_row),
        grid=(n_tiles + 1,),
        in_specs=in_specs,
        out_specs=pl.BlockSpec((1, ts, D), att_tile),
        out_shape=jax.ShapeDtypeStruct((B, S, D), F32),
        scratch_shapes=[pltpu.VMEM((ts, D), F32),
                        pltpu.VMEM((ts + HIST, D_CONV), F32),
                        pltpu.VMEM((SUBLANES - 1, ts + HIST - SUBLANES, D_CONV), F32)]
                       + [pltpu.VMEM((ts + HIST, D_POOL), F32)] * 4,
        compiler_params=pltpu.CompilerParams(
            dimension_semantics=("arbitrary",), vmem_limit_bytes=VMEM_LIMIT),
        name="mix_attn",
    )(x, *mix_args, *att_args)


def _kv_kernel(mem_ref, g_ref, wk_ref, wv_ref, k_ref, v_ref):
    m = _rms(mem_ref[0], g_ref[...]).astype(BF16)
    k_ref[0] = _dot(m, wk_ref[...]).astype(BF16)
    v_ref[0] = _dot(m, wv_ref[...]).astype(BF16)


def _kv(mem, g, wk, wv):
    B, M, D = mem.shape
    return pl.pallas_call(
        _kv_kernel,
        grid=(B,),
        in_specs=[pl.BlockSpec((1, M, D), lambda b: (b, 0, 0)), _const_spec((1, D)),
                  _const_spec((D, D)), _const_spec((D, D))],
        out_specs=[pl.BlockSpec((1, M, D), lambda b: (b, 0, 0))] * 2,
        out_shape=[jax.ShapeDtypeStruct((B, M, D), BF16)] * 2,
        compiler_params=pltpu.CompilerParams(
            dimension_semantics=("arbitrary",), vmem_limit_bytes=VMEM_LIMIT),
        name="mem_kv",
    )(mem, g.reshape(1, -1), wk, wv)


def _xattn_pieces(x, o_ref, g_ref, wq_ref, k_ref, v_ref, wo_ref):
    h = _rms(x, g_ref[...]).astype(BF16)
    heads = []
    for hd in range(X_HEADS):
        sl = slice(hd * X_HEAD_DIM, (hd + 1) * X_HEAD_DIM)
        q = _dot(h, wq_ref[:, sl]).astype(BF16)
        sc = lax.dot_general(q, k_ref[0, :, sl], (((1,), (1,)), ((), ())),
                             preferred_element_type=F32) * (X_HEAD_DIM ** -0.5)
        e = jnp.exp(sc - jnp.max(sc, axis=-1, keepdims=True))
        heads.append(_dot(e.astype(BF16), v_ref[0, :, sl]) / jnp.sum(e, axis=-1, keepdims=True))
        yield
    o = jnp.concatenate(heads, axis=-1).astype(BF16)
    for c in range(X_HEADS):
        sl = slice(c * X_HEAD_DIM, (c + 1) * X_HEAD_DIM)
        o_ref[0, :, sl] = x[:, sl] + _dot(o, wo_ref[:, sl])
        yield


def _store_tiles(ref, val):
    for j in range(ref.shape[0]):
        ref[j] = val[:, j * LANES:(j + 1) * LANES]


def _load_tiles(ref):
    return jnp.concatenate([ref[j] for j in range(ref.shape[0])], axis=1)


def _pack_bf16_pairs(v):
    bits = lax.bitcast_convert_type(v.astype(BF16).astype(F32), U32)
    half = bits.shape[1] // 2
    return (bits[:, :half] & jnp.uint32(HI16)) | (bits[:, half:] >> 16)


def _unpack_bf16_pairs(w):
    hi = lax.bitcast_convert_type(w & jnp.uint32(HI16), F32)
    lo = lax.bitcast_convert_type(w << 16, F32)
    return jnp.concatenate([hi, lo], axis=1)


def _tiled_shape(rows, width, dtype):
    return jax.ShapeDtypeStruct((width // LANES, rows, LANES), dtype)


def _tiled_spec(rows, width, row_block, lead=None):
    block = (width // LANES, rows, LANES)
    if lead is None:
        return pl.BlockSpec(block, lambda *a: (0, row_block(*a), 0))
    return pl.BlockSpec((1,) + block, lambda *a: (lead, 0, row_block(*a), 0))


def _piece_index(rows, k, n_rows):
    return jnp.arange(k, dtype=I32)[:, None] * n_rows + rows[..., None, :]


META_I1, META_I2, META_G1, META_G2, META_R1, META_R2 = range(6)


def _router_kernel(x_ref, g_ref, rw_ref, triu_ref, meta_ref, hp_ref, cnt_ref, carry):
    @pl.when(pl.program_id(0) == 0)
    def _():
        carry[...] = jnp.zeros_like(carry)

    h = _rms(x_ref[...], g_ref[...])
    h_hi = h.astype(BF16)
    h_lo = (h - h_hi.astype(F32)).astype(BF16)
    logits = _dot(jnp.concatenate([h_hi, h_hi, h_lo], axis=1), rw_ref[...])

    neg = jnp.float32(-jnp.inf)
    lane = lax.broadcasted_iota(I32, logits.shape, 1)
    lt = jnp.where(lane < N_EXPERTS, logits, neg).T[:N_EXPERTS, :]
    expert = lax.broadcasted_iota(I32, lt.shape, 0)
    m1 = jnp.max(lt, axis=0, keepdims=True)
    i1 = jnp.min(jnp.where(lt == m1, expert, N_EXPERTS), axis=0, keepdims=True)
    lt2 = jnp.where(expert == i1, neg, lt)
    m2 = jnp.max(lt2, axis=0, keepdims=True)
    i2 = jnp.min(jnp.where(lt2 == m2, expert, N_EXPERTS), axis=0, keepdims=True)
    d = jnp.exp(m2 - m1)
    g1 = 1.0 / (1.0 + d)
    g2 = d / (1.0 + d)

    sel1 = expert == i1
    sel2 = expert == i2
    onehot = jnp.where(jnp.logical_or(sel1, sel2), 1.0, 0.0)
    before = _dot(onehot, triu_ref[...]) + carry[:, 0:1]
    r1 = jnp.sum(jnp.where(sel1, before, 0.0), axis=0, keepdims=True)
    r2 = jnp.sum(jnp.where(sel2, before, 0.0), axis=0, keepdims=True)
    carry[...] += jnp.sum(onehot, axis=1, keepdims=True)
    cnt_ref[...] = carry[...]

    rows = [i1.astype(F32), i2.astype(F32), g1, g2, r1, r2]
    meta_ref[...] = jnp.concatenate(rows + [jnp.zeros_like(g1)] * (SUBLANES - len(rows)), axis=0)
    _store_tiles(hp_ref, _pack_bf16_pairs(h))


def _router(x2, g, router_w, first_row, n_rows):
    D = x2.shape[1]
    tm = FFN_TM
    first_block = first_row // tm
    assert N_EXPERTS == SUBLANES
    rw = jnp.zeros((D, LANES), F32).at[:, :N_EXPERTS].set(router_w)
    rw_hi = rw.astype(BF16)
    rw_lo = (rw - rw_hi.astype(F32)).astype(BF16)
    rw_split = jnp.concatenate([rw_hi, rw_lo, rw_hi], axis=0)
    triu = jnp.triu(jnp.ones((tm, tm), F32), 1)
    return pl.pallas_call(
        _router_kernel,
        grid=(n_rows // tm,),
        in_specs=[pl.BlockSpec((tm, D), lambda i: (first_block + i, 0)), _const_spec((1, D)),
                  _const_spec((3 * D, LANES)), _const_spec((tm, tm))],
        out_specs=[pl.BlockSpec((SUBLANES, tm), lambda i: (0, i)),
                   _tiled_spec(tm, D // 2, lambda i: i), _const_spec((N_EXPERTS, LANES))],
        out_shape=[jax.ShapeDtypeStruct((SUBLANES, n_rows), F32),
                   _tiled_shape(n_rows, D // 2, U32), jax.ShapeDtypeStruct((N_EXPERTS, LANES), F32)],
        scratch_shapes=[pltpu.VMEM((N_EXPERTS, LANES), F32)],
        compiler_params=pltpu.CompilerParams(
            dimension_semantics=("arbitrary",), vmem_limit_bytes=VMEM_LIMIT),
        name="router",
    )(x2, g.reshape(1, -1), rw_split, triu)


def _sc_mesh():
    return plsc.VectorSubcoreMesh(core_axis_name="core", subcore_axis_name="subcore")


def _sc_dispatch(tiled, dest1, dest2, n_out):
    k, n, _ = tiled.shape
    d1, d2 = (_piece_index(d, k, n_out).reshape(-1) for d in (dest1, dest2))
    return _sc_scatter_pieces(tiled.reshape(k * n, LANES), d1, d2, k * n_out).reshape(k, n_out, LANES)


def _sc_gather(tiled, idx):
    k, n_rows, _ = tiled.shape
    G, n = idx.shape
    out = _sc_gather_pieces(tiled.reshape(k * n_rows, LANES), _piece_index(idx, k, n_rows).reshape(-1))
    return out.reshape(G, k, n, LANES)


def _sc_scatter_pieces(rows, dest1, dest2, n_out):
    T, W = rows.shape
    win = SC_WIN

    @pl.kernel(out_type=jax.ShapeDtypeStruct((n_out, W), rows.dtype), mesh=_sc_mesh(), scratch_types=[])
    def scatter_kernel(x_hbm, i1_hbm, i2_hbm, o_hbm):
        def body(x_vmem, i1_vmem, i2_vmem):
            pltpu.sync_copy(x_vmem, o_hbm.at[i1_vmem.at[0]])
            pltpu.sync_copy(x_vmem, o_hbm.at[i2_vmem.at[0]])

        pltpu.emit_pipeline(
            body,
            grid=(T // win,),
            in_specs=[pl.BlockSpec((win, W), lambda i: (i, 0)),
                      pl.BlockSpec((1, win), lambda i: (0, i)),
                      pl.BlockSpec((1, win), lambda i: (0, i))],
            out_specs=[],
            core_axis_name=("core", "subcore"),
            dimension_semantics=(pltpu.PARALLEL,),
        )(x_hbm, i1_hbm, i2_hbm)

    return scatter_kernel(rows, dest1.reshape(1, T), dest2.reshape(1, T))


def _sc_gather_pieces(rows, idx):
    n = idx.shape[0]
    W = rows.shape[1]
    win = SC_WIN

    @pl.kernel(out_type=jax.ShapeDtypeStruct((n, W), rows.dtype), mesh=_sc_mesh(), scratch_types=[])
    def gather_kernel(x_hbm, i_hbm, o_hbm):
        def body(i_vmem, o_vmem):
            pltpu.sync_copy(x_hbm.at[i_vmem.at[0]], o_vmem)

        pltpu.emit_pipeline(
            body,
            grid=(n // win,),
            in_specs=[pl.BlockSpec((1, win), lambda i: (0, i))],
            out_specs=[pl.BlockSpec((win, W), lambda i: (i, 0))],
            core_axis_name=("core", "subcore"),
            dimension_semantics=(pltpu.PARALLEL,),
        )(i_hbm, o_hbm)

    return gather_kernel(rows, idx.reshape(1, n))


def _swiglu(h, wg, wu, wd):
    gate = _dot(h, wg)
    up = _dot(h, wu)
    return _dot((gate * jax.nn.sigmoid(gate) * up).astype(BF16), wd)


def _swiglu_chunked(h, wg_ref, wu_ref, wd_ref, acc=None):
    F = wg_ref.shape[1]
    n_chunks = -(-F // FFN_FC)
    cols = -(-F // (n_chunks * MXU_WIDTH)) * MXU_WIDTH
    for lo in range(0, F, cols):
        sl = slice(lo, min(lo + cols, F))
        y = _swiglu(h, wg_ref[:, sl], wu_ref[:, sl], wd_ref[sl, :])
        acc = y if acc is None else acc + y
    return acc


def _ffn_kernel(x_ref, g_ref, wg_ref, wu_ref, wd_ref, *refs):
    n_cast = len(refs) // 2
    o_ref = refs[n_cast]
    x = x_ref[...]
    h = _rms(x, g_ref[...]).astype(BF16)
    o_ref[...] = _swiglu_chunked(h, wg_ref, wu_ref, wd_ref, acc=x)
    for src_ref, dst_ref in zip(refs[:n_cast], refs[n_cast + 1:]):
        dst_ref[...] = src_ref[0].astype(BF16)


def _ffn(x2, g, wg, wu, wd, cast_weights=()):
    T, D = x2.shape
    F = wg.shape[1]
    tm = FFN_TM
    n_steps = T // tm
    single = lambda shape: pl.BlockSpec(shape, lambda i: (0,) * len(shape), pipeline_mode=pl.Buffered(1))
    cast_in, cast_out, cast_shapes = [], [], []
    for w, lead in cast_weights:
        _, E, rows, cols = w.shape
        per_group = n_steps // E
        assert n_steps % E == 0 and rows % (per_group * 2 * SUBLANES) == 0
        cast_in.append(pl.BlockSpec((1, 1, rows // per_group, cols),
                                    lambda i, lead=lead, per=per_group: (lead, i // per, i % per, 0)))
        cast_out.append(pl.BlockSpec((1, rows // per_group, cols), lambda i, per=per_group: (i // per, i % per, 0)))
        cast_shapes.append(jax.ShapeDtypeStruct((E, rows, cols), BF16))
    out, *cast = pl.pallas_call(
        _ffn_kernel,
        grid=(n_steps,),
        in_specs=[pl.BlockSpec((tm, D), lambda i: (i, 0)), single((1, D)),
                  single((D, F)), single((D, F)), single((F, D))] + cast_in,
        out_specs=[pl.BlockSpec((tm, D), lambda i: (i, 0))] + cast_out,
        out_shape=[jax.ShapeDtypeStruct((T, D), F32)] + cast_shapes,
        compiler_params=pltpu.CompilerParams(
            dimension_semantics=("arbitrary",), vmem_limit_bytes=VMEM_LIMIT),
        name="swiglu",
    )(x2, g.reshape(1, -1), wg, wu, wd, *[w for w, _ in cast_weights])
    return out, cast


def _moe_kernel(tidx_ref, texp_ref, nvalid_ref, xs_ref, wg_ref, wu_ref, wd_ref, y_ref):
    del tidx_ref, texp_ref

    @pl.when(pl.program_id(0) < nvalid_ref[0])
    def _():
        h = _unpack_bf16_pairs(_load_tiles(xs_ref)).astype(BF16)
        _store_tiles(y_ref, _pack_bf16_pairs(_swiglu_chunked(h, wg_ref.at[0], wu_ref.at[0], wd_ref.at[0])))


def _moe_experts(xs, tile_idx, tile_expert, n_valid, wg, wu, wd):
    E, D, F = wg.shape
    tm = MOE_TM
    P = xs.shape[1]
    half = xs.shape[0] * LANES
    row_block = lambda g, ti, te, nv: ti[g]
    w_map = lambda g, ti, te, nv: (te[g], 0, 0)
    return pl.pallas_call(
        _moe_kernel,
        grid_spec=pltpu.PrefetchScalarGridSpec(
            num_scalar_prefetch=3,
            grid=(P // tm,),
            in_specs=[_tiled_spec(tm, half, row_block),
                      pl.BlockSpec((1, D, F), w_map), pl.BlockSpec((1, D, F), w_map),
                      pl.BlockSpec((1, F, D), w_map)],
            out_specs=_tiled_spec(tm, D // 2, row_block)),
        out_shape=_tiled_shape(P, D // 2, U32),
        compiler_params=pltpu.CompilerParams(
            dimension_semantics=("arbitrary",), vmem_limit_bytes=VMEM_LIMIT),
        name="moe_experts",
    )(tile_idx, tile_expert, n_valid, xs, wg, wu, wd)


def _combine_kernel(x_ref, y1_ref, y2_ref, meta_ref, fg_ref, *refs, final_norm):
    o_ref = refs[-1]
    rec = meta_ref[...]
    rec_t = jnp.concatenate([rec, jnp.zeros((LANES - rec.shape[0], rec.shape[1]), F32)], axis=0).T
    lane = lax.broadcasted_iota(I32, rec_t.shape, 1)
    gate = lambda row: jnp.sum(jnp.where(lane == row, rec_t, 0.0), axis=-1, keepdims=True)
    out = x_ref[...] + gate(META_G1) * _unpack_bf16_pairs(_load_tiles(y1_ref.at[0]))
    out = out + gate(META_G2) * _unpack_bf16_pairs(_load_tiles(y2_ref.at[0]))
    if final_norm:
        out = _rms(out, fg_ref[...])
    o_ref[...] = out


def _combine(x2, y12, meta, final_g, first_row, partial_out, *, final_norm):
    T, D = x2.shape
    n_rows = meta.shape[1]
    tm = FFN_TM
    first_block = first_row // tm
    rows = lambda i: (first_block + i, 0)
    args = [x2, y12, y12, meta, final_g.reshape(1, -1)]
    in_specs = [pl.BlockSpec((tm, D), rows),
                _tiled_spec(tm, D // 2, lambda i: i, lead=0), _tiled_spec(tm, D // 2, lambda i: i, lead=1),
                pl.BlockSpec((SUBLANES, tm), lambda i: (0, i)), _const_spec((1, D))]
    aliases = {}
    if partial_out is not None:
        args.append(partial_out)
        in_specs.append(pl.BlockSpec(memory_space=pl.ANY))
        aliases = {len(args) - 1: 0}
    return pl.pallas_call(
        functools.partial(_combine_kernel, final_norm=final_norm),
        grid=(n_rows // tm,),
        in_specs=in_specs,
        out_specs=pl.BlockSpec((tm, D), rows),
        out_shape=jax.ShapeDtypeStruct((T, D), F32),
        input_output_aliases=aliases,
        compiler_params=pltpu.CompilerParams(
            dimension_semantics=("arbitrary",), vmem_limit_bytes=VMEM_LIMIT),
        name="moe_combine",
    )(*args)


def _route_and_dispatch(x2, g, router_w, first_row, n_rows):
    tm = MOE_TM
    n_tiles = (n_rows * TOP_K) // tm + N_EXPERTS
    meta, hp, counts = _router(x2, g, router_w, first_row, n_rows)

    cnt = counts[:, 0].astype(I32)
    tiles_e = (cnt + tm - 1) // tm
    tile_end = jnp.cumsum(tiles_e)
    row_start = (tile_end - tiles_e) * tm
    experts = jnp.arange(N_EXPERTS, dtype=I32)[:, None]

    def dest(i_row, r_row):
        start = jnp.sum(jnp.where(meta[i_row].astype(I32) == experts, row_start[:, None], 0), axis=0)
        return start + meta[r_row].astype(I32)

    dests = jnp.stack([dest(META_I1, META_R1), dest(META_I2, META_R2)])
    n_valid = tile_end[-1:]
    tile_idx = jnp.minimum(jnp.arange(n_tiles, dtype=I32), n_valid - 1)
    tile_expert = jnp.minimum(jnp.sum(tile_end[:, None] <= tile_idx, axis=0), N_EXPERTS - 1).astype(I32)
    xs = _sc_dispatch(hp, dests[0], dests[1], n_tiles * tm)
    return xs, (tile_idx, tile_expert, n_valid.astype(I32)), dests, meta


def _moe_layer(x2, g, router_w, wg, wu, wd, final_g, *, final_norm):
    T = x2.shape[0]
    n_rows = T // MOE_ROW_GROUPS
    starts = [grp * n_rows for grp in range(MOE_ROW_GROUPS)]
    routed = [_route_and_dispatch(x2, g, router_w, first_row, n_rows) for first_row in starts]
    ys = [_moe_experts(xs, *tiles, wg, wu, wd) for xs, tiles, _, _ in routed]
    y12s = [_sc_gather(y, dests) for y, (_, _, dests, _) in zip(ys, routed)]
    out = None
    for first_row, y12, (_, _, _, meta) in zip(starts, y12s, routed):
        out = _combine(x2, y12, meta, final_g, first_row, out, final_norm=final_norm)
    return out


def kernel(x, mem, mix_norm_g, w_in, conv_w, conv_b, conv_ln_g, conv_ln_b, sgu_ln_g, sgu_ln_b, sgu_w, sgu_b,
           pool_w, pool_b, pool_scale, w_out, xattn_norm_g, mem_norm_g, xattn_wq, xattn_wk, xattn_wv, xattn_wo,
           ffn_norm_g, ffn_wg, ffn_wu, ffn_wd, router_w, moe_wg, moe_wu, moe_wd, final_norm_g):
    B, S, D = x.shape
    bf = lambda a: a.astype(BF16)
    layer_mats = dict(w_in=w_in, w_out=w_out, wq=xattn_wq, wk=xattn_wk, wv=xattn_wv, wo=xattn_wo)
    precast = {}

    def mat(name, l):
        return precast[name, l] if (name, l) in precast else bf(layer_mats[name][l])

    for l in range(DEPTH):
        sgu_bias = jnp.repeat(sgu_b[l].T, D_SGU // SGU_HEADS, axis=1)
        pool_wbd = jax.scipy.linalg.block_diag(*[pool_w[l, gi] for gi in range(len(POOL_WINDOWS))])
        mix_params = (mix_norm_g[l], mat("w_in", l), conv_w[l], conv_b[l], conv_ln_g[l], conv_ln_b[l],
                      sgu_ln_g[l], sgu_ln_b[l], sgu_w[l], sgu_bias, bf(pool_wbd), pool_b[l].reshape(-1),
                      pool_scale[l], mat("w_out", l))
        k, v = _kv(mem, mem_norm_g[l], mat("wk", l), mat("wv", l))
        x = _mix_attn(x, mix_params, k, v, (xattn_norm_g[l], mat("wq", l), mat("wo", l)))
        x2 = x.reshape(B * S, D)
        j = l // 2
        if l % 2 == 0:
            assert l != DEPTH - 1, "the final RMSNorm is fused into the routed layer's combine kernel"
            names = list(layer_mats)
            split_rows = lambda a: a.reshape(a.shape[0], N_EXPERTS, a.shape[1] // N_EXPERTS, a.shape[2])
            side = [(moe_wg, j), (moe_wu, j), (moe_wd, j)] + [(split_rows(layer_mats[n]), l + 1) for n in names]
            x2, cast = _ffn(x2, ffn_norm_g[l], bf(ffn_wg[j]), bf(ffn_wu[j]), bf(ffn_wd[j]), cast_weights=side)
            moe_bf16 = cast[:3]
            for n, c in zip(names, cast[3:]):
                precast[n, l + 1] = c.reshape(layer_mats[n].shape[1:])
        else:
            x2 = _moe_layer(x2, ffn_norm_g[l], router_w[j], *moe_bf16, final_norm_g, final_norm=l == DEPTH - 1)
        x = x2.reshape(B, S, D)
    return x
```

```python
import functools

import jax
import jax.numpy as jnp
from jax import lax
from jax.experimental import pallas as pl
from jax.experimental.pallas import tpu as pltpu
from jax.experimental.pallas import tpu_sc as plsc

F32 = jnp.float32
BF16 = jnp.bfloat16
U32 = jnp.uint32
I32 = jnp.int32

D_MODEL = 1024
DEPTH = 2
CHUNK = 64
D_CONV = 384
CONV_WIDTH = 31
D_SGU = 384
SGU_HEADS = 4
SGU_CHUNK = 128
D_POOL = 256
POOL_WINDOWS = (2, 4, 8, 16)
POOL_GROUP_DIM = D_POOL // len(POOL_WINDOWS)
D_MIX = D_CONV + D_SGU + D_POOL
D_IN = 2 * D_CONV + 2 * D_SGU + D_POOL
X_HEADS = 4
X_HEAD_DIM = D_MODEL // X_HEADS
N_EXPERTS = 8
TOP_K = 2
EPS = 1e-6

LANES = 128
SUBLANES = 8
HIST = 32
MIX_TS = 512
CONV_RB = 64
FFN_TM = 512
FFN_FC = 1408
MXU_WIDTH = 256
MOE_TM = 512
MOE_ROW_GROUPS = 2
SC_WIN = 128
HI16 = 0xFFFF0000
VMEM_LIMIT = 56 * 1024 * 1024


def _rms(x, g):
    return x * lax.rsqrt(jnp.mean(x * x, axis=-1, keepdims=True) + EPS) * g


def _layer_norm(x, g, b):
    mu = jnp.mean(x, axis=-1, keepdims=True)
    xc = x - mu
    var = jnp.mean(xc * xc, axis=-1, keepdims=True)
    return xc * lax.rsqrt(var + EPS) * g + b


def _dot(a, b):
    return jnp.dot(a, b, preferred_element_type=F32)


def _round_robin(*generators):
    live = list(generators)
    while live:
        for gen in list(live):
            try:
                next(gen)
                yield
            except StopIteration:
                live.remove(gen)


def _mixer_tile(x, s, g_ref, win_ref, convw_ref, convb_ref, clng_ref, clnb_ref,
                slng_ref, slnb_ref, sguw_ref, sgub_ref, poolw_ref, poolb_ref, pscale_ref,
                wout_ref, cbuf, cshift, pb0, pb1, pb2, pb3, overlap):
    ts = MIX_TS
    a_end = 2 * D_CONV
    b_end = a_end + 2 * D_SGU
    h = _rms(x, g_ref[...]).astype(BF16)
    z_a = _dot(h, win_ref[:, :a_end])
    z_rest = []

    def project_rest():
        for lo in range(a_end, D_IN, MXU_WIDTH):
            z_rest.append(_dot(h, win_ref[:, lo:lo + MXU_WIDTH]))
            yield

    pieces = _round_robin(project_rest(), overlap)

    cbuf[HIST:HIST + ts, :] = z_a[:, :D_CONV] * jax.nn.sigmoid(z_a[:, D_CONV:])
    span = ts + HIST - SUBLANES
    for r in range(1, SUBLANES):
        cshift[r - 1, 0:span, :] = cbuf[r:r + span, :]
    ya_blocks = []
    n_blocks = ts // CONV_RB
    for rb in range(n_blocks):
        acc = jnp.zeros((CONV_RB, D_CONV), F32)
        for k in range(CONV_WIDTH):
            start = rb * CONV_RB + HIST - (CONV_WIDTH - 1) + k
            r, base = start % SUBLANES, start - start % SUBLANES
            src = cbuf[base:base + CONV_RB, :] if r == 0 else cshift[r - 1, base:base + CONV_RB, :]
            acc = acc + convw_ref[k:k + 1, :] * src
        ya_blocks.append(acc)
        for _ in range(2 if rb < n_blocks // 2 else 1):
            next(pieces, None)
    for _ in pieces:
        pass
    ya = jnp.concatenate(ya_blocks, axis=0) + convb_ref[...]
    ya = _layer_norm(ya, clng_ref[...], clnb_ref[...])
    ya = ya * jax.nn.sigmoid(ya)
    cbuf[0:HIST, :] = cbuf[ts:ts + HIST, :]
    out = x + _dot(ya.astype(BF16), wout_ref[0:D_CONV, :])

    z = jnp.concatenate(z_rest, axis=1)
    zb = jax.nn.gelu(z[:, :b_end - a_end])
    u = zb[:, :D_SGU]
    v = _layer_norm(zb[:, D_SGU:], slng_ref[...], slnb_ref[...]).astype(BF16)
    blk_r = lax.broadcasted_iota(jnp.int32, (SGU_CHUNK, SGU_CHUNK), 0) // CHUNK
    blk_c = lax.broadcasted_iota(jnp.int32, (SGU_CHUNK, SGU_CHUNK), 1) // CHUNK
    head_of_lane = lax.broadcasted_iota(jnp.int32, (SGU_CHUNK, D_SGU), 1) // (D_SGU // SGU_HEADS)
    w_heads = [jnp.where(blk_r >= blk_c, sguw_ref[hd], 0.0).astype(BF16) for hd in range(SGU_HEADS)]
    s_chunks = []
    for c in range(ts // SGU_CHUNK):
        vc = v[c * SGU_CHUNK:(c + 1) * SGU_CHUNK, :]
        sc = jnp.zeros((SGU_CHUNK, D_SGU), F32)
        for hd in range(SGU_HEADS):
            sc = jnp.where(head_of_lane == hd, _dot(w_heads[hd], vc), sc)
        s_chunks.append(sc + sgub_ref[...])
    yb = u * jnp.concatenate(s_chunks, axis=0)
    out = out + _dot(yb.astype(BF16), wout_ref[D_CONV:D_CONV + D_SGU, :])

    cc = z[:, b_end - a_end:]
    pb0[HIST:HIST + ts, :] = cc
    pb1[0:ts + 24, :] = pb0[8:ts + 32, :] + pb0[7:ts + 31, :]
    pb2[0:ts + 16, :] = pb1[8:ts + 24, :] + pb1[6:ts + 22, :]
    pb3[0:ts + 8, :] = pb2[8:ts + 16, :] + pb2[4:ts + 12, :]
    s16 = pb3[8:ts + 8, :] + pb3[0:ts, :]
    s8 = pb3[8:ts + 8, :]
    s4 = pb2[16:ts + 16, :]
    s2 = pb1[24:ts + 24, :]
    grp = lax.broadcasted_iota(jnp.int32, (ts, D_POOL), 1) // POOL_GROUP_DIM
    wsum = jnp.where(grp == 0, s2, jnp.where(grp == 1, s4, jnp.where(grp == 2, s8, s16)))
    win = jnp.where(grp == 0, 2, jnp.where(grp == 1, 4, jnp.where(grp == 2, 8, 16)))
    pos = s * ts + lax.broadcasted_iota(jnp.int32, (ts, D_POOL), 0)
    cnt = jnp.minimum(pos + 1, win).astype(F32)
    p = (wsum / cnt - cc).astype(BF16)
    yc = (_dot(p, poolw_ref[...]) + poolb_ref[...]) * pscale_ref[...]
    pb0[0:HIST, :] = pb0[ts:ts + HIST, :]

    return out + _dot(yc.astype(BF16), wout_ref[D_CONV + D_SGU:D_MIX, :])


def _const_spec(shape):
    zeros = (0,) * len(shape)
    return pl.BlockSpec(shape, lambda *_: zeros)


N_MIX_PARAMS = 14
N_ATT_PARAMS = 5


def _mix_attn_kernel(x_ref, *refs, n_tiles, tiles_per_row):
    mix_refs = refs[:N_MIX_PARAMS]
    att_refs = refs[N_MIX_PARAMS:N_MIX_PARAMS + N_ATT_PARAMS]
    o_ref, mid, cbuf, cshift, pb0, pb1, pb2, pb3 = refs[N_MIX_PARAMS + N_ATT_PARAMS:]
    i = pl.program_id(0)
    s = jnp.minimum(i, n_tiles - 1) % tiles_per_row

    @pl.when(i == 0)
    def _():
        mid[...] = jnp.zeros_like(mid)

    @pl.when(s == 0)
    def _():
        cbuf[0:HIST, :] = jnp.zeros((HIST, D_CONV), F32)
        pb0[0:HIST, :] = jnp.zeros((HIST, D_POOL), F32)

    attention = _xattn_pieces(mid[...], o_ref, *att_refs)
    mid[...] = _mixer_tile(x_ref[0], s, *mix_refs, cbuf, cshift, pb0, pb1, pb2, pb3, overlap=attention)


def _mix_attn(x, mix_params, k, v, att_params):
    B, S, D = x.shape
    M = k.shape[1]
    ts = MIX_TS
    tiles_per_row = S // ts
    n_tiles = B * tiles_per_row
    g_att, wq, wo = att_params
    row = lambda a: a.reshape(1, -1) if a.ndim == 1 else a
    mix_args = [row(a) for a in mix_params]
    att_args = [row(g_att), wq, k, v, wo]
    assert len(mix_args) == N_MIX_PARAMS and len(att_args) == N_ATT_PARAMS

    def mix_tile(i):
        t = jnp.minimum(i, n_tiles - 1)
        return t // tiles_per_row, t % tiles_per_row, 0

    def att_tile(i):
        t = jnp.maximum(i - 1, 0)
        return t // tiles_per_row, t % tiles_per_row, 0

    single = lambda a: pl.BlockSpec(a.shape, lambda i, nd=a.ndim: (0,) * nd, pipeline_mode=pl.Buffered(1))
    kv_spec = pl.BlockSpec((1, M, D), lambda i: (att_tile(i)[0], 0, 0))
    in_specs = [pl.BlockSpec((1, ts, D), mix_tile)] + [single(a) for a in mix_args]
    in_specs += [single(att_args[0]), single(wq), kv_spec, kv_spec, single(wo)]
    return pl.pallas_call(
        functools.partial(_mix_attn_kernel, n_tiles=n_tiles, tiles_per_row=tiles_per_row),
        grid=(n_tiles + 1,),
        in_specs=in_specs,
        out_specs=pl.BlockSpec((1, ts, D), att_tile),
        out_shape=jax.ShapeDtypeStruct((B, S, D), F32),
        scratch_shapes=[pltpu.VMEM((ts, D), F32),
                        pltpu.VMEM((ts + HIST, D_CONV), F32),
                        pltpu.VMEM((SUBLANES - 1, ts + HIST - SUBLANES, D_CONV), F32)]
                       + [pltpu.VMEM((ts + HIST, D_POOL), F32)] * 4,
        compiler_params=pltpu.CompilerParams(
            dimension_semantics=("arbitrary",), vmem_limit_bytes=VMEM_LIMIT),
        name="mix_attn",
    )(x, *mix_args, *att_args)


def _kv_kernel(mem_ref, g_ref, wk_ref, wv_ref, k_ref, v_ref):
    m = _rms(mem_ref[0], g_ref[...]).astype(BF16)
    k_ref[0] = _dot(m, wk_ref[...]).astype(BF16)
    v_ref[0] = _dot(m, wv_ref[...]).astype(BF16)


def _kv(mem, g, wk, wv):
    B, M, D = mem.shape
    return pl.pallas_call(
        _kv_kernel,
        grid=(B,),
        in_specs=[pl.BlockSpec((1, M, D), lambda b: (b, 0, 0)), _const_spec((1, D)),
                  _const_spec((D, D)), _const_spec((D, D))],
        out_specs=[pl.BlockSpec((1, M, D), lambda b: (b, 0, 0))] * 2,
        out_shape=[jax.ShapeDtypeStruct((B, M, D), BF16)] * 2,
        compiler_params=pltpu.CompilerParams(
            dimension_semantics=("arbitrary",), vmem_limit_bytes=VMEM_LIMIT),
        name="mem_kv",
    )(mem, g.reshape(1, -1), wk, wv)


def _xattn_pieces(x, o_ref, g_ref, wq_ref, k_ref, v_ref, wo_ref):
    h = _rms(x, g_ref[...]).astype(BF16)
    heads = []
    for hd in range(X_HEADS):
        sl = slice(hd * X_HEAD_DIM, (hd + 1) * X_HEAD_DIM)
        q = _dot(h, wq_ref[:, sl]).astype(BF16)
        sc = lax.dot_general(q, k_ref[0, :, sl], (((1,), (1,)), ((), ())),
                             preferred_element_type=F32) * (X_HEAD_DIM ** -0.5)
        e = jnp.exp(sc - jnp.max(sc, axis=-1, keepdims=True))
        heads.append(_dot(e.astype(BF16), v_ref[0, :, sl]) / jnp.sum(e, axis=-1, keepdims=True))
        yield
    o = jnp.concatenate(heads, axis=-1).astype(BF16)
    for c in range(X_HEADS):
        sl = slice(c * X_HEAD_DIM, (c + 1) * X_HEAD_DIM)
        o_ref[0, :, sl] = x[:, sl] + _dot(o, wo_ref[:, sl])
        yield


def _store_tiles(ref, val):
    n = val.shape[0]
    for j in range(ref.shape[0]):
        ref[j, :n, :] = val[:, j * LANES:(j + 1) * LANES]


def _load_tiles(ref, n=None):
    n = ref.shape[1] if n is None else n
    return jnp.concatenate([ref[j, :n, :] for j in range(ref.shape[0])], axis=1)


def _pack_bf16_pairs(v):
    bits = lax.bitcast_convert_type(v.astype(BF16).astype(F32), U32)
    half = bits.shape[1] // 2
    return (bits[:, :half] & jnp.uint32(HI16)) | (bits[:, half:] >> 16)


def _unpack_bf16_pairs(w):
    hi = lax.bitcast_convert_type(w & jnp.uint32(HI16), F32)
    lo = lax.bitcast_convert_type(w << 16, F32)
    return jnp.concatenate([hi, lo], axis=1)


def _tiled_shape(rows, width, dtype):
    return jax.ShapeDtypeStruct((width // LANES, rows, LANES), dtype)


def _tiled_spec(rows, width, row_block, lead=None):
    block = (width // LANES, rows, LANES)
    if lead is None:
        return pl.BlockSpec(block, lambda *a: (0, row_block(*a), 0))
    return pl.BlockSpec((1,) + block, lambda *a: (lead, 0, row_block(*a), 0))


def _piece_index(rows, k, n_rows):
    return jnp.arange(k, dtype=I32)[:, None] * n_rows + rows[..., None, :]


META_I1, META_I2, META_G1, META_G2, META_R1, META_R2 = range(6)


def _router_kernel(x_ref, g_ref, rw_ref, triu_ref, meta_ref, hp_ref, cnt_ref, carry):
    @pl.when(pl.program_id(0) == 0)
    def _():
        carry[...] = jnp.zeros_like(carry)

    h = _rms(x_ref[...], g_ref[...])
    h_hi = h.astype(BF16)
    h_lo = (h - h_hi.astype(F32)).astype(BF16)
    logits = _dot(jnp.concatenate([h_hi, h_hi, h_lo], axis=1), rw_ref[...])

    neg = jnp.float32(-jnp.inf)
    lane = lax.broadcasted_iota(I32, logits.shape, 1)
    lt = jnp.where(lane < N_EXPERTS, logits, neg).T[:N_EXPERTS, :]
    expert = lax.broadcasted_iota(I32, lt.shape, 0)
    m1 = jnp.max(lt, axis=0, keepdims=True)
    i1 = jnp.min(jnp.where(lt == m1, expert, N_EXPERTS), axis=0, keepdims=True)
    lt2 = jnp.where(expert == i1, neg, lt)
    m2 = jnp.max(lt2, axis=0, keepdims=True)
    i2 = jnp.min(jnp.where(lt2 == m2, expert, N_EXPERTS), axis=0, keepdims=True)
    d = jnp.exp(m2 - m1)
    g1 = 1.0 / (1.0 + d)
    g2 = d / (1.0 + d)

    sel1 = expert == i1
    sel2 = expert == i2
    onehot = jnp.where(jnp.logical_or(sel1, sel2), 1.0, 0.0)
    before = _dot(onehot, triu_ref[...]) + carry[:, 0:1]
    r1 = jnp.sum(jnp.where(sel1, before, 0.0), axis=0, keepdims=True)
    r2 = jnp.sum(jnp.where(sel2, before, 0.0), axis=0, keepdims=True)
    carry[...] += jnp.sum(onehot, axis=1, keepdims=True)
    cnt_ref[...] = carry[...]

    rows = [i1.astype(F32), i2.astype(F32), g1, g2, r1, r2]
    meta_ref[...] = jnp.concatenate(rows + [jnp.zeros_like(g1)] * (SUBLANES - len(rows)), axis=0)
    _store_tiles(hp_ref, _pack_bf16_pairs(h))


def _router(x2, g, router_w, first_row, n_rows):
    D = x2.shape[1]
    tm = FFN_TM
    first_block = first_row // tm
    assert N_EXPERTS == SUBLANES
    rw = jnp.zeros((D, LANES), F32).at[:, :N_EXPERTS].set(router_w)
    rw_hi = rw.astype(BF16)
    rw_lo = (rw - rw_hi.astype(F32)).astype(BF16)
    rw_split = jnp.concatenate([rw_hi, rw_lo, rw_hi], axis=0)
    triu = jnp.triu(jnp.ones((tm, tm), F32), 1)
    return pl.pallas_call(
        _router_kernel,
        grid=(n_rows // tm,),
        in_specs=[pl.BlockSpec((tm, D), lambda i: (first_block + i, 0)), _const_spec((1, D)),
                  _const_spec((3 * D, LANES)), _const_spec((tm, tm))],
        out_specs=[pl.BlockSpec((SUBLANES, tm), lambda i: (0, i)),
                   _tiled_spec(tm, D // 2, lambda i: i), _const_spec((N_EXPERTS, LANES))],
        out_shape=[jax.ShapeDtypeStruct((SUBLANES, n_rows), F32),
                   _tiled_shape(n_rows, D // 2, U32), jax.ShapeDtypeStruct((N_EXPERTS, LANES), F32)],
        scratch_shapes=[pltpu.VMEM((N_EXPERTS, LANES), F32)],
        compiler_params=pltpu.CompilerParams(
            dimension_semantics=("arbitrary",), vmem_limit_bytes=VMEM_LIMIT),
        name="router",
    )(x2, g.reshape(1, -1), rw_split, triu)


def _sc_mesh():
    return plsc.VectorSubcoreMesh(core_axis_name="core", subcore_axis_name="subcore")


def _sc_dispatch(tiled, dest1, dest2, n_out):
    k, n, _ = tiled.shape
    d1, d2 = (_piece_index(d, k, n_out).reshape(-1) for d in (dest1, dest2))
    return _sc_scatter_pieces(tiled.reshape(k * n, LANES), d1, d2, k * n_out).reshape(k, n_out, LANES)


def _sc_gather(tiled, idx):
    k, n_rows, _ = tiled.shape
    G, n = idx.shape
    out = _sc_gather_pieces(tiled.reshape(k * n_rows, LANES), _piece_index(idx, k, n_rows).reshape(-1))
    return out.reshape(G, k, n, LANES)


def _sc_scatter_pieces(rows, dest1, dest2, n_out):
    T, W = rows.shape
    win = SC_WIN

    @pl.kernel(out_type=jax.ShapeDtypeStruct((n_out, W), rows.dtype), mesh=_sc_mesh(), scratch_types=[])
    def scatter_kernel(x_hbm, i1_hbm, i2_hbm, o_hbm):
        def body(x_vmem, i1_vmem, i2_vmem):
            pltpu.sync_copy(x_vmem, o_hbm.at[i1_vmem.at[0]])
            pltpu.sync_copy(x_vmem, o_hbm.at[i2_vmem.at[0]])

        pltpu.emit_pipeline(
            body,
            grid=(T // win,),
            in_specs=[pl.BlockSpec((win, W), lambda i: (i, 0)),
                      pl.BlockSpec((1, win), lambda i: (0, i)),
                      pl.BlockSpec((1, win), lambda i: (0, i))],
            out_specs=[],
            core_axis_name=("core", "subcore"),
            dimension_semantics=(pltpu.PARALLEL,),
        )(x_hbm, i1_hbm, i2_hbm)

    return scatter_kernel(rows, dest1.reshape(1, T), dest2.reshape(1, T))


def _sc_gather_pieces(rows, idx):
    n = idx.shape[0]
    W = rows.shape[1]
    win = SC_WIN

    @pl.kernel(out_type=jax.ShapeDtypeStruct((n, W), rows.dtype), mesh=_sc_mesh(), scratch_types=[])
    def gather_kernel(x_hbm, i_hbm, o_hbm):
        def body(i_vmem, o_vmem):
            pltpu.sync_copy(x_hbm.at[i_vmem.at[0]], o_vmem)

        pltpu.emit_pipeline(
            body,
            grid=(n // win,),
            in_specs=[pl.BlockSpec((1, win), lambda i: (0, i))],
            out_specs=[pl.BlockSpec((win, W), lambda i: (i, 0))],
            core_axis_name=("core", "subcore"),
            dimension_semantics=(pltpu.PARALLEL,),
        )(i_hbm, o_hbm)

    return gather_kernel(rows, idx.reshape(1, n))


def _swiglu(h, wg, wu, wd):
    gate = _dot(h, wg)
    up = _dot(h, wu)
    return _dot((gate * jax.nn.sigmoid(gate) * up).astype(BF16), wd)


def _swiglu_chunked(h, wg_ref, wu_ref, wd_ref, acc=None):
    F = wg_ref.shape[1]
    n_chunks = -(-F // FFN_FC)
    cols = -(-F // (n_chunks * MXU_WIDTH)) * MXU_WIDTH
    for lo in range(0, F, cols):
        sl = slice(lo, min(lo + cols, F))
        y = _swiglu(h, wg_ref[:, sl], wu_ref[:, sl], wd_ref[sl, :])
        acc = y if acc is None else acc + y
    return acc


def _ffn_kernel(x_ref, g_ref, wg_ref, wu_ref, wd_ref, *refs):
    n_cast = len(refs) // 2
    o_ref = refs[n_cast]
    x = x_ref[...]
    h = _rms(x, g_ref[...]).astype(BF16)
    o_ref[...] = _swiglu_chunked(h, wg_ref, wu_ref, wd_ref, acc=x)
    for src_ref, dst_ref in zip(refs[:n_cast], refs[n_cast + 1:]):
        dst_ref[...] = src_ref[0].astype(BF16)


def _ffn(x2, g, wg, wu, wd, cast_weights=()):
    T, D = x2.shape
    F = wg.shape[1]
    tm = FFN_TM
    n_steps = T // tm
    single = lambda shape: pl.BlockSpec(shape, lambda i: (0,) * len(shape), pipeline_mode=pl.Buffered(1))
    cast_in, cast_out, cast_shapes = [], [], []
    for w, lead in cast_weights:
        _, E, rows, cols = w.shape
        per_group = n_steps // E
        assert n_steps % E == 0 and rows % (per_group * 2 * SUBLANES) == 0
        cast_in.append(pl.BlockSpec((1, 1, rows // per_group, cols),
                                    lambda i, lead=lead, per=per_group: (lead, i // per, i % per, 0)))
        cast_out.append(pl.BlockSpec((1, rows // per_group, cols), lambda i, per=per_group: (i // per, i % per, 0)))
        cast_shapes.append(jax.ShapeDtypeStruct((E, rows, cols), BF16))
    out, *cast = pl.pallas_call(
        _ffn_kernel,
        grid=(n_steps,),
        in_specs=[pl.BlockSpec((tm, D), lambda i: (i, 0)), single((1, D)),
                  single((D, F)), single((D, F)), single((F, D))] + cast_in,
        out_specs=[pl.BlockSpec((tm, D), lambda i: (i, 0))] + cast_out,
        out_shape=[jax.ShapeDtypeStruct((T, D), F32)] + cast_shapes,
        compiler_params=pltpu.CompilerParams(
            dimension_semantics=("arbitrary",), vmem_limit_bytes=VMEM_LIMIT),
        name="swiglu",
    )(x2, g.reshape(1, -1), wg, wu, wd, *[w for w, _ in cast_weights])
    return out, cast


def _moe_kernel(tidx_ref, texp_ref, trows_ref, nvalid_ref, xs_ref, wg_ref, wu_ref, wd_ref, y_ref):
    del tidx_ref, texp_ref
    g = pl.program_id(0)
    tm = xs_ref.shape[1]
    live = g < nvalid_ref[0]
    half_full = trows_ref[g] <= tm // 2

    def run(n):
        h = _unpack_bf16_pairs(_load_tiles(xs_ref, n)).astype(BF16)
        _store_tiles(y_ref, _pack_bf16_pairs(_swiglu_chunked(h, wg_ref.at[0], wu_ref.at[0], wd_ref.at[0])))

    pl.when(jnp.logical_and(live, jnp.logical_not(half_full)))(lambda: run(tm))
    pl.when(jnp.logical_and(live, half_full))(lambda: run(tm // 2))


def _moe_experts(xs, tile_idx, tile_expert, tile_rows, n_valid, wg, wu, wd):
    E, D, F = wg.shape
    tm = MOE_TM
    P = xs.shape[1]
    half = xs.shape[0] * LANES
    row_block = lambda g, ti, te, tr, nv: ti[g]
    w_map = lambda g, ti, te, tr, nv: (te[g], 0, 0)
    return pl.pallas_call(
        _moe_kernel,
        grid_spec=pltpu.PrefetchScalarGridSpec(
            num_scalar_prefetch=4,
            grid=(P // tm,),
            in_specs=[_tiled_spec(tm, half, row_block),
                      pl.BlockSpec((1, D, F), w_map), pl.BlockSpec((1, D, F), w_map),
                      pl.BlockSpec((1, F, D), w_map)],
            out_specs=_tiled_spec(tm, D // 2, row_block)),
        out_shape=_tiled_shape(P, D // 2, U32),
        compiler_params=pltpu.CompilerParams(
            dimension_semantics=("arbitrary",), vmem_limit_bytes=VMEM_LIMIT),
        name="moe_experts",
    )(tile_idx, tile_expert, tile_rows, n_valid, xs, wg, wu, wd)


def _combine_kernel(x_ref, y1_ref, y2_ref, meta_ref, fg_ref, *refs, final_norm):
    o_ref = refs[-1]
    rec = meta_ref[...]
    rec_t = jnp.concatenate([rec, jnp.zeros((LANES - rec.shape[0], rec.shape[1]), F32)], axis=0).T
    lane = lax.broadcasted_iota(I32, rec_t.shape, 1)
    gate = lambda row: jnp.sum(jnp.where(lane == row, rec_t, 0.0), axis=-1, keepdims=True)
    out = x_ref[...] + gate(META_G1) * _unpack_bf16_pairs(_load_tiles(y1_ref.at[0]))
    out = out + gate(META_G2) * _unpack_bf16_pairs(_load_tiles(y2_ref.at[0]))
    if final_norm:
        out = _rms(out, fg_ref[...])
    o_ref[...] = out


def _combine(x2, y12, meta, final_g, first_row, partial_out, *, final_norm):
    T, D = x2.shape
    n_rows = meta.shape[1]
    tm = FFN_TM
    first_block = first_row // tm
    rows = lambda i: (first_block + i, 0)
    args = [x2, y12, y12, meta, final_g.reshape(1, -1)]
    in_specs = [pl.BlockSpec((tm, D), rows),
                _tiled_spec(tm, D // 2, lambda i: i, lead=0), _tiled_spec(tm, D // 2, lambda i: i, lead=1),
                pl.BlockSpec((SUBLANES, tm), lambda i: (0, i)), _const_spec((1, D))]
    aliases = {}
    if partial_out is not None:
        args.append(partial_out)
        in_specs.append(pl.BlockSpec(memory_space=pl.ANY))
        aliases = {len(args) - 1: 0}
    return pl.pallas_call(
        functools.partial(_combine_kernel, final_norm=final_norm),
        grid=(n_rows // tm,),
        in_specs=in_specs,
        out_specs=pl.BlockSpec((tm, D), rows),
        out_shape=jax.ShapeDtypeStruct((T, D), F32),
        input_output_aliases=aliases,
        compiler_params=pltpu.CompilerParams(
            dimension_semantics=("arbitrary",), vmem_limit_bytes=VMEM_LIMIT),
        name="moe_combine",
    )(*args)


def _route_and_dispatch(x2, g, router_w, first_row, n_rows):
    tm = MOE_TM
    n_tiles = (n_rows * TOP_K) // tm + N_EXPERTS
    meta, hp, counts = _router(x2, g, router_w, first_row, n_rows)

    cnt = counts[:, 0].astype(I32)
    tiles_e = (cnt + tm - 1) // tm
    tile_end = jnp.cumsum(tiles_e)
    row_start = (tile_end - tiles_e) * tm
    experts = jnp.arange(N_EXPERTS, dtype=I32)[:, None]

    def dest(i_row, r_row):
        start = jnp.sum(jnp.where(meta[i_row].astype(I32) == experts, row_start[:, None], 0), axis=0)
        return start + meta[r_row].astype(I32)

    dests = jnp.stack([dest(META_I1, META_R1), dest(META_I2, META_R2)])
    n_valid = tile_end[-1:]
    tile_idx = jnp.minimum(jnp.arange(n_tiles, dtype=I32), n_valid - 1)
    tile_expert = jnp.minimum(jnp.sum(tile_end[:, None] <= tile_idx, axis=0), N_EXPERTS - 1).astype(I32)
    tile_rows = jnp.clip(cnt[tile_expert] - (tile_idx - (tile_end - tiles_e)[tile_expert]) * tm, 0, tm).astype(I32)
    xs = _sc_dispatch(hp, dests[0], dests[1], n_tiles * tm)
    return xs, (tile_idx, tile_expert, tile_rows, n_valid.astype(I32)), dests, meta


def _moe_layer(x2, g, router_w, wg, wu, wd, final_g, *, final_norm):
    T = x2.shape[0]
    n_rows = T // MOE_ROW_GROUPS
    starts = [grp * n_rows for grp in range(MOE_ROW_GROUPS)]
    routed = [_route_and_dispatch(x2, g, router_w, first_row, n_rows) for first_row in starts]
    ys = [_moe_experts(xs, *tiles, wg, wu, wd) for xs, tiles, _, _ in routed]
    y12s = [_sc_gather(y, dests) for y, (_, _, dests, _) in zip(ys, routed)]
    out = None
    for first_row, y12, (_, _, _, meta) in zip(starts, y12s, routed):
        out = _combine(x2, y12, meta, final_g, first_row, out, final_norm=final_norm)
    return out


def kernel(x, mem, mix_norm_g, w_in, conv_w, conv_b, conv_ln_g, conv_ln_b, sgu_ln_g, sgu_ln_b, sgu_w, sgu_b,
           pool_w, pool_b, pool_scale, w_out, xattn_norm_g, mem_norm_g, xattn_wq, xattn_wk, xattn_wv, xattn_wo,
           ffn_norm_g, ffn_wg, ffn_wu, ffn_wd, router_w, moe_wg, moe_wu, moe_wd, final_norm_g):
    B, S, D = x.shape
    bf = lambda a: a.astype(BF16)
    layer_mats = dict(w_in=w_in, w_out=w_out, wq=xattn_wq, wk=xattn_wk, wv=xattn_wv, wo=xattn_wo)
    precast = {}

    def mat(name, l):
        return precast[name, l] if (name, l) in precast else bf(layer_mats[name][l])

    for l in range(DEPTH):
        sgu_bias = jnp.repeat(sgu_b[l].T, D_SGU // SGU_HEADS, axis=1)
        pool_wbd = jax.scipy.linalg.block_diag(*[pool_w[l, gi] for gi in range(len(POOL_WINDOWS))])
        mix_params = (mix_norm_g[l], mat("w_in", l), conv_w[l], conv_b[l], conv_ln_g[l], conv_ln_b[l],
                      sgu_ln_g[l], sgu_ln_b[l], sgu_w[l], sgu_bias, bf(pool_wbd), pool_b[l].reshape(-1),
                      pool_scale[l], mat("w_out", l))
        k, v = _kv(mem, mem_norm_g[l], mat("wk", l), mat("wv", l))
        x = _mix_attn(x, mix_params, k, v, (xattn_norm_g[l], mat("wq", l), mat("wo", l)))
        x2 = x.reshape(B * S, D)
        j = l // 2
        if l % 2 == 0:
            assert l != DEPTH - 1, "the final RMSNorm is fused into the routed layer's combine kernel"
            names = list(layer_mats)
            split_rows = lambda a: a.reshape(a.shape[0], N_EXPERTS, a.shape[1] // N_EXPERTS, a.shape[2])
            side = [(moe_wg, j), (moe_wu, j), (moe_wd, j)] + [(split_rows(layer_mats[n]), l + 1) for n in names]
            x2, cast = _ffn(x2, ffn_norm_g[l], bf(ffn_wg[j]), bf(ffn_wu[j]), bf(ffn_wd[j]), cast_weights=side)
            moe_bf16 = cast[:3]
            for n, c in zip(names, cast[3:]):
                precast[n, l + 1] = c.reshape(layer_mats[n].shape[1:])
        else:
            x2 = _moe_layer(x2, ffn_norm_g[l], router_w[j], *moe_bf16, final_norm_g, final_norm=l == DEPTH - 1)
        x = x2.reshape(B, S, D)
    return x
```

```python
import functools

import jax
import jax.numpy as jnp
from jax import lax
from jax.experimental import pallas as pl
from jax.experimental.pallas import tpu as pltpu
from jax.experimental.pallas import tpu_sc as plsc

F32 = jnp.float32
BF16 = jnp.bfloat16
U32 = jnp.uint32
I32 = jnp.int32

D_MODEL = 1024
DEPTH = 2
CHUNK = 64
D_CONV = 384
CONV_WIDTH = 31
D_SGU = 384
SGU_HEADS = 4
SGU_CHUNK = 128
D_POOL = 256
POOL_WINDOWS = (2, 4, 8, 16)
POOL_GROUP_DIM = D_POOL // len(POOL_WINDOWS)
D_MIX = D_CONV + D_SGU + D_POOL
D_IN = 2 * D_CONV + 2 * D_SGU + D_POOL
X_HEADS = 4
X_HEAD_DIM = D_MODEL // X_HEADS
N_EXPERTS = 8
TOP_K = 2
EPS = 1e-6

LANES = 128
SUBLANES = 8
HIST = 32
MIX_TS = 512
CONV_RB = 64
FFN_TM = 512
FFN_FC = 1408
MXU_WIDTH = 256
MOE_TM = 512
MOE_ROW_GROUPS = 2
SC_WIN = 128
HI16 = 0xFFFF0000
VMEM_LIMIT = 56 * 1024 * 1024


def _rms(x, g):
    return x * lax.rsqrt(jnp.mean(x * x, axis=-1, keepdims=True) + EPS) * g


def _layer_norm(x, g, b):
    mu = jnp.mean(x, axis=-1, keepdims=True)
    xc = x - mu
    var = jnp.mean(xc * xc, axis=-1, keepdims=True)
    return xc * lax.rsqrt(var + EPS) * g + b


def _dot(a, b):
    return jnp.dot(a, b, preferred_element_type=F32)


def _round_robin(*generators):
    live = list(generators)
    while live:
        for gen in list(live):
            try:
                next(gen)
                yield
            except StopIteration:
                live.remove(gen)


def _mixer_tile(x, s, g_ref, win_ref, convw_ref, convb_ref, clng_ref, clnb_ref,
                slng_ref, slnb_ref, sguw_ref, sgub_ref, poolw_ref, poolb_ref, pscale_ref,
                wout_ref, cbuf, cshift, pb0, pb1, pb2, pb3, overlap):
    ts = MIX_TS
    a_end = 2 * D_CONV
    b_end = a_end + 2 * D_SGU
    h = _rms(x, g_ref[...]).astype(BF16)
    z_a = _dot(h, win_ref[:, :a_end])
    z_rest = []

    def project_rest():
        for lo in range(a_end, D_IN, MXU_WIDTH):
            z_rest.append(_dot(h, win_ref[:, lo:lo + MXU_WIDTH]))
            yield

    pieces = _round_robin(project_rest(), overlap)

    cbuf[HIST:HIST + ts, :] = z_a[:, :D_CONV] * jax.nn.sigmoid(z_a[:, D_CONV:])
    span = ts + HIST - SUBLANES
    for r in range(1, SUBLANES):
        cshift[r - 1, 0:span, :] = cbuf[r:r + span, :]
    ya_blocks = []
    n_blocks = ts // CONV_RB
    for rb in range(n_blocks):
        acc = jnp.zeros((CONV_RB, D_CONV), F32)
        for k in range(CONV_WIDTH):
            start = rb * CONV_RB + HIST - (CONV_WIDTH - 1) + k
            r, base = start % SUBLANES, start - start % SUBLANES
            src = cbuf[base:base + CONV_RB, :] if r == 0 else cshift[r - 1, base:base + CONV_RB, :]
            acc = acc + convw_ref[k:k + 1, :] * src
        ya_blocks.append(acc)
        for _ in range(2 if rb < n_blocks // 2 else 1):
            next(pieces, None)
    for _ in pieces:
        pass
    ya = jnp.concatenate(ya_blocks, axis=0) + convb_ref[...]
    ya = _layer_norm(ya, clng_ref[...], clnb_ref[...])
    ya = ya * jax.nn.sigmoid(ya)
    cbuf[0:HIST, :] = cbuf[ts:ts + HIST, :]
    out = x + _dot(ya.astype(BF16), wout_ref[0:D_CONV, :])

    z = jnp.concatenate(z_rest, axis=1)
    zb = jax.nn.gelu(z[:, :b_end - a_end])
    u = zb[:, :D_SGU]
    v = _layer_norm(zb[:, D_SGU:], slng_ref[...], slnb_ref[...]).astype(BF16)
    blk_r = lax.broadcasted_iota(jnp.int32, (SGU_CHUNK, SGU_CHUNK), 0) // CHUNK
    blk_c = lax.broadcasted_iota(jnp.int32, (SGU_CHUNK, SGU_CHUNK), 1) // CHUNK
    head_of_lane = lax.broadcasted_iota(jnp.int32, (SGU_CHUNK, D_SGU), 1) // (D_SGU // SGU_HEADS)
    w_heads = [jnp.where(blk_r >= blk_c, sguw_ref[hd], 0.0).astype(BF16) for hd in range(SGU_HEADS)]
    w_cat = jnp.concatenate(w_heads, axis=1)
    zero = jnp.zeros((), BF16)
    s_chunks = []
    for c in range(ts // SGU_CHUNK):
        vc = v[c * SGU_CHUNK:(c + 1) * SGU_CHUNK, :]
        v_stack = jnp.concatenate([jnp.where(head_of_lane == hd, vc, zero) for hd in range(SGU_HEADS)], axis=0)
        s_chunks.append(_dot(w_cat, v_stack) + sgub_ref[...])
    yb = u * jnp.concatenate(s_chunks, axis=0)
    out = out + _dot(yb.astype(BF16), wout_ref[D_CONV:D_CONV + D_SGU, :])

    cc = z[:, b_end - a_end:]
    pb0[HIST:HIST + ts, :] = cc
    pb1[0:ts + 24, :] = pb0[8:ts + 32, :] + pb0[7:ts + 31, :]
    pb2[0:ts + 16, :] = pb1[8:ts + 24, :] + pb1[6:ts + 22, :]
    pb3[0:ts + 8, :] = pb2[8:ts + 16, :] + pb2[4:ts + 12, :]
    s16 = pb3[8:ts + 8, :] + pb3[0:ts, :]
    s8 = pb3[8:ts + 8, :]
    s4 = pb2[16:ts + 16, :]
    s2 = pb1[24:ts + 24, :]
    grp = lax.broadcasted_iota(jnp.int32, (ts, D_POOL), 1) // POOL_GROUP_DIM
    wsum = jnp.where(grp == 0, s2, jnp.where(grp == 1, s4, jnp.where(grp == 2, s8, s16)))
    head = max(POOL_WINDOWS)
    inv_win = jnp.where(grp == 0, 0.5, jnp.where(grp == 1, 0.25, jnp.where(grp == 2, 0.125, 0.0625)))
    grp_h = lax.broadcasted_iota(jnp.int32, (head, D_POOL), 1) // POOL_GROUP_DIM
    win = jnp.where(grp_h == 0, 2, jnp.where(grp_h == 1, 4, jnp.where(grp_h == 2, 8, 16)))
    pos = s * ts + lax.broadcasted_iota(jnp.int32, (head, D_POOL), 0)
    cnt = jnp.minimum(pos + 1, win).astype(F32)
    mean = jnp.concatenate([wsum[:head] / cnt, (wsum * inv_win)[head:]], axis=0)
    p = (mean - cc).astype(BF16)
    yc = (_dot(p, poolw_ref[...]) + poolb_ref[...]) * pscale_ref[...]
    pb0[0:HIST, :] = pb0[ts:ts + HIST, :]

    return out + _dot(yc.astype(BF16), wout_ref[D_CONV + D_SGU:D_MIX, :])


def _const_spec(shape):
    zeros = (0,) * len(shape)
    return pl.BlockSpec(shape, lambda *_: zeros)


N_MIX_PARAMS = 14
N_ATT_PARAMS = 5


def _mix_attn_kernel(x_ref, *refs, n_tiles, tiles_per_row):
    mix_refs = refs[:N_MIX_PARAMS]
    att_refs = refs[N_MIX_PARAMS:N_MIX_PARAMS + N_ATT_PARAMS]
    o_ref, mid, cbuf, cshift, pb0, pb1, pb2, pb3 = refs[N_MIX_PARAMS + N_ATT_PARAMS:]
    i = pl.program_id(0)
    s = jnp.minimum(i, n_tiles - 1) % tiles_per_row

    @pl.when(i == 0)
    def _():
        mid[...] = jnp.zeros_like(mid)

    @pl.when(s == 0)
    def _():
        cbuf[0:HIST, :] = jnp.zeros((HIST, D_CONV), F32)
        pb0[0:HIST, :] = jnp.zeros((HIST, D_POOL), F32)

    attention = _xattn_pieces(mid[...], o_ref, *att_refs)
    mid[...] = _mixer_tile(x_ref[0], s, *mix_refs, cbuf, cshift, pb0, pb1, pb2, pb3, overlap=attention)


def _mix_attn(x, mix_params, k, v, att_params):
    B, S, D = x.shape
    M = k.shape[1]
    ts = MIX_TS
    tiles_per_row = S // ts
    n_tiles = B * tiles_per_row
    g_att, wq, wo = att_params
    row = lambda a: a.reshape(1, -1) if a.ndim == 1 else a
    mix_args = [row(a) for a in mix_params]
    att_args = [row(g_att), wq, k, v, wo]
    assert len(mix_args) == N_MIX_PARAMS and len(att_args) == N_ATT_PARAMS

    def mix_tile(i):
        t = jnp.minimum(i, n_tiles - 1)
        return t // tiles_per_row, t % tiles_per_row, 0

    def att_tile(i):
        t = jnp.maximum(i - 1, 0)
        return t // tiles_per_row, t % tiles_per_row, 0

    single = lambda a: pl.BlockSpec(a.shape, lambda i, nd=a.ndim: (0,) * nd, pipeline_mode=pl.Buffered(1))
    kv_spec = pl.BlockSpec((1, M, D), lambda i: (att_tile(i)[0], 0, 0))
    in_specs = [pl.BlockSpec((1, ts, D), mix_tile)] + [single(a) for a in mix_args]
    in_specs += [single(att_args[0]), single(wq), kv_spec, kv_spec, single(wo)]
    return pl.pallas_call(
        functools.partial(_mix_attn_kernel, n_tiles=n_tiles, tiles_per_row=tiles_per_row),
        grid=(n_tiles + 1,),
        in_specs=in_specs,
        out_specs=pl.BlockSpec((1, ts, D), att_tile),
        out_shape=jax.ShapeDtypeStruct((B, S, D), F32),
        scratch_shapes=[pltpu.VMEM((ts, D), F32),
                        pltpu.VMEM((ts + HIST, D_CONV), F32),
                        pltpu.VMEM((SUBLANES - 1, ts + HIST - SUBLANES, D_CONV), F32)]
                       + [pltpu.VMEM((ts + HIST, D_POOL), F32)] * 4,
        compiler_params=pltpu.CompilerParams(
            dimension_semantics=("arbitrary",), vmem_limit_bytes=VMEM_LIMIT),
        name="mix_attn",
    )(x, *mix_args, *att_args)


def _kv_kernel(mem_ref, g_ref, wk_ref, wv_ref, k_ref, v_ref):
    m = _rms(mem_ref[0], g_ref[...]).astype(BF16)
    k_ref[0] = _dot(m, wk_ref[...]).astype(BF16)
    v_ref[0] = _dot(m, wv_ref[...]).astype(BF16)


def _kv(mem, g, wk, wv):
    B, M, D = mem.shape
    return pl.pallas_call(
        _kv_kernel,
        grid=(B,),
        in_specs=[pl.BlockSpec((1, M, D), lambda b: (b, 0, 0)), _const_spec((1, D)),
                  _const_spec((D, D)), _const_spec((D, D))],
        out_specs=[pl.BlockSpec((1, M, D), lambda b: (b, 0, 0))] * 2,
        out_shape=[jax.ShapeDtypeStruct((B, M, D), BF16)] * 2,
        compiler_params=pltpu.CompilerParams(
            dimension_semantics=("arbitrary",), vmem_limit_bytes=VMEM_LIMIT),
        name="mem_kv",
    )(mem, g.reshape(1, -1), wk, wv)


def _xattn_pieces(x, o_ref, g_ref, wq_ref, k_ref, v_ref, wo_ref):
    h = _rms(x, g_ref[...]).astype(BF16)
    heads = []
    for hd in range(X_HEADS):
        sl = slice(hd * X_HEAD_DIM, (hd + 1) * X_HEAD_DIM)
        q = _dot(h, wq_ref[:, sl]).astype(BF16)
        sc = lax.dot_general(q, k_ref[0, :, sl], (((1,), (1,)), ((), ())),
                             preferred_element_type=F32) * (X_HEAD_DIM ** -0.5)
        e = jnp.exp(sc - jnp.max(sc, axis=-1, keepdims=True))
        heads.append(_dot(e.astype(BF16), v_ref[0, :, sl]) / jnp.sum(e, axis=-1, keepdims=True))
        yield
    o = jnp.concatenate(heads, axis=-1).astype(BF16)
    for c in range(X_HEADS):
        sl = slice(c * X_HEAD_DIM, (c + 1) * X_HEAD_DIM)
        o_ref[0, :, sl] = x[:, sl] + _dot(o, wo_ref[:, sl])
        yield


def _store_tiles(ref, val):
    n = val.shape[0]
    for j in range(ref.shape[0]):
        ref[j, :n, :] = val[:, j * LANES:(j + 1) * LANES]


def _load_tiles(ref, n=None):
    n = ref.shape[1] if n is None else n
    return jnp.concatenate([ref[j, :n, :] for j in range(ref.shape[0])], axis=1)


def _pack_bf16_pairs(v):
    bits = lax.bitcast_convert_type(v.astype(BF16).astype(F32), U32)
    half = bits.shape[1] // 2
    return (bits[:, :half] & jnp.uint32(HI16)) | (bits[:, half:] >> 16)


def _unpack_bf16_pairs(w):
    hi = lax.bitcast_convert_type(w & jnp.uint32(HI16), F32)
    lo = lax.bitcast_convert_type(w << 16, F32)
    return jnp.concatenate([hi, lo], axis=1)


def _tiled_shape(rows, width, dtype):
    return jax.ShapeDtypeStruct((width // LANES, rows, LANES), dtype)


def _tiled_spec(rows, width, row_block, lead=None):
    block = (width // LANES, rows, LANES)
    if lead is None:
        return pl.BlockSpec(block, lambda *a: (0, row_block(*a), 0))
    return pl.BlockSpec((1,) + block, lambda *a: (lead, 0, row_block(*a), 0))


def _piece_index(rows, k, n_rows):
    return jnp.arange(k, dtype=I32)[:, None] * n_rows + rows[..., None, :]


META_I1, META_I2, META_G1, META_G2, META_R1, META_R2 = range(6)


def _router_kernel(x_ref, g_ref, rw_ref, triu_ref, meta_ref, hp_ref, cnt_ref, carry):
    @pl.when(pl.program_id(0) == 0)
    def _():
        carry[...] = jnp.zeros_like(carry)

    h = _rms(x_ref[...], g_ref[...])
    h_hi = h.astype(BF16)
    h_lo = (h - h_hi.astype(F32)).astype(BF16)
    logits = _dot(jnp.concatenate([h_hi, h_hi, h_lo], axis=1), rw_ref[...])

    neg = jnp.float32(-jnp.inf)
    lane = lax.broadcasted_iota(I32, logits.shape, 1)
    lt = jnp.where(lane < N_EXPERTS, logits, neg).T[:N_EXPERTS, :]
    expert = lax.broadcasted_iota(I32, lt.shape, 0)
    m1 = jnp.max(lt, axis=0, keepdims=True)
    i1 = jnp.min(jnp.where(lt == m1, expert, N_EXPERTS), axis=0, keepdims=True)
    lt2 = jnp.where(expert == i1, neg, lt)
    m2 = jnp.max(lt2, axis=0, keepdims=True)
    i2 = jnp.min(jnp.where(lt2 == m2, expert, N_EXPERTS), axis=0, keepdims=True)
    d = jnp.exp(m2 - m1)
    g1 = 1.0 / (1.0 + d)
    g2 = d / (1.0 + d)

    sel1 = expert == i1
    sel2 = expert == i2
    onehot = jnp.where(jnp.logical_or(sel1, sel2), 1.0, 0.0)
    before = _dot(onehot, triu_ref[...]) + carry[:, 0:1]
    r1 = jnp.sum(jnp.where(sel1, before, 0.0), axis=0, keepdims=True)
    r2 = jnp.sum(jnp.where(sel2, before, 0.0), axis=0, keepdims=True)
    carry[...] += jnp.sum(onehot, axis=1, keepdims=True)
    cnt_ref[...] = carry[...]

    rows = [i1.astype(F32), i2.astype(F32), g1, g2, r1, r2]
    meta_ref[...] = jnp.concatenate(rows + [jnp.zeros_like(g1)] * (SUBLANES - len(rows)), axis=0)
    _store_tiles(hp_ref, _pack_bf16_pairs(h))


def _router(x2, g, router_w, first_row, n_rows):
    D = x2.shape[1]
    tm = FFN_TM
    first_block = first_row // tm
    assert N_EXPERTS == SUBLANES
    rw = jnp.zeros((D, LANES), F32).at[:, :N_EXPERTS].set(router_w)
    rw_hi = rw.astype(BF16)
    rw_lo = (rw - rw_hi.astype(F32)).astype(BF16)
    rw_split = jnp.concatenate([rw_hi, rw_lo, rw_hi], axis=0)
    triu = jnp.triu(jnp.ones((tm, tm), F32), 1)
    return pl.pallas_call(
        _router_kernel,
        grid=(n_rows // tm,),
        in_specs=[pl.BlockSpec((tm, D), lambda i: (first_block + i, 0)), _const_spec((1, D)),
                  _const_spec((3 * D, LANES)), _const_spec((tm, tm))],
        out_specs=[pl.BlockSpec((SUBLANES, tm), lambda i: (0, i)),
                   _tiled_spec(tm, D // 2, lambda i: i), _const_spec((N_EXPERTS, LANES))],
        out_shape=[jax.ShapeDtypeStruct((SUBLANES, n_rows), F32),
                   _tiled_shape(n_rows, D // 2, U32), jax.ShapeDtypeStruct((N_EXPERTS, LANES), F32)],
        scratch_shapes=[pltpu.VMEM((N_EXPERTS, LANES), F32)],
        compiler_params=pltpu.CompilerParams(
            dimension_semantics=("arbitrary",), vmem_limit_bytes=VMEM_LIMIT),
        name="router",
    )(x2, g.reshape(1, -1), rw_split, triu)


def _sc_mesh():
    return plsc.VectorSubcoreMesh(core_axis_name="core", subcore_axis_name="subcore")


def _sc_dispatch(tiled, dest1, dest2, n_out):
    k, n, _ = tiled.shape
    d1, d2 = (_piece_index(d, k, n_out).reshape(-1) for d in (dest1, dest2))
    return _sc_scatter_pieces(tiled.reshape(k * n, LANES), d1, d2, k * n_out).reshape(k, n_out, LANES)


def _sc_gather(tiled, idx):
    k, n_rows, _ = tiled.shape
    G, n = idx.shape
    out = _sc_gather_pieces(tiled.reshape(k * n_rows, LANES), _piece_index(idx, k, n_rows).reshape(-1))
    return out.reshape(G, k, n, LANES)


def _sc_scatter_pieces(rows, dest1, dest2, n_out):
    T, W = rows.shape
    win = SC_WIN

    @pl.kernel(out_type=jax.ShapeDtypeStruct((n_out, W), rows.dtype), mesh=_sc_mesh(), scratch_types=[])
    def scatter_kernel(x_hbm, i1_hbm, i2_hbm, o_hbm):
        def body(x_vmem, i1_vmem, i2_vmem):
            pltpu.sync_copy(x_vmem, o_hbm.at[i1_vmem.at[0]])
            pltpu.sync_copy(x_vmem, o_hbm.at[i2_vmem.at[0]])

        pltpu.emit_pipeline(
            body,
            grid=(T // win,),
            in_specs=[pl.BlockSpec((win, W), lambda i: (i, 0)),
                      pl.BlockSpec((1, win), lambda i: (0, i)),
                      pl.BlockSpec((1, win), lambda i: (0, i))],
            out_specs=[],
            core_axis_name=("core", "subcore"),
            dimension_semantics=(pltpu.PARALLEL,),
        )(x_hbm, i1_hbm, i2_hbm)

    return scatter_kernel(rows, dest1.reshape(1, T), dest2.reshape(1, T))


def _sc_gather_pieces(rows, idx):
    n = idx.shape[0]
    W = rows.shape[1]
    win = SC_WIN

    @pl.kernel(out_type=jax.ShapeDtypeStruct((n, W), rows.dtype), mesh=_sc_mesh(), scratch_types=[])
    def gather_kernel(x_hbm, i_hbm, o_hbm):
        def body(i_vmem, o_vmem):
            pltpu.sync_copy(x_hbm.at[i_vmem.at[0]], o_vmem)

        pltpu.emit_pipeline(
            body,
            grid=(n // win,),
            in_specs=[pl.BlockSpec((1, win), lambda i: (0, i))],
            out_specs=[pl.BlockSpec((win, W), lambda i: (i, 0))],
            core_axis_name=("core", "subcore"),
            dimension_semantics=(pltpu.PARALLEL,),
        )(i_hbm, o_hbm)

    return gather_kernel(rows, idx.reshape(1, n))


def _swiglu(h, wg, wu, wd):
    gate = _dot(h, wg)
    up = _dot(h, wu)
    return _dot((gate * jax.nn.sigmoid(gate) * up).astype(BF16), wd)


def _swiglu_chunked(h, wg_ref, wu_ref, wd_ref, acc=None):
    F = wg_ref.shape[1]
    n_chunks = -(-F // FFN_FC)
    cols = -(-F // (n_chunks * MXU_WIDTH)) * MXU_WIDTH
    for lo in range(0, F, cols):
        sl = slice(lo, min(lo + cols, F))
        y = _swiglu(h, wg_ref[:, sl], wu_ref[:, sl], wd_ref[sl, :])
        acc = y if acc is None else acc + y
    return acc


def _ffn_kernel(x_ref, g_ref, wg_ref, wu_ref, wd_ref, *refs):
    n_cast = len(refs) // 2
    o_ref = refs[n_cast]
    x = x_ref[...]
    h = _rms(x, g_ref[...]).astype(BF16)
    o_ref[...] = _swiglu_chunked(h, wg_ref, wu_ref, wd_ref, acc=x)
    for src_ref, dst_ref in zip(refs[:n_cast], refs[n_cast + 1:]):
        dst_ref[...] = src_ref[0].astype(BF16)


def _ffn(x2, g, wg, wu, wd, cast_weights=()):
    T, D = x2.shape
    F = wg.shape[1]
    tm = FFN_TM
    n_steps = T // tm
    single = lambda shape: pl.BlockSpec(shape, lambda i: (0,) * len(shape), pipeline_mode=pl.Buffered(1))
    cast_in, cast_out, cast_shapes = [], [], []
    for w, lead in cast_weights:
        _, E, rows, cols = w.shape
        per_group = n_steps // E
        assert n_steps % E == 0 and rows % (per_group * 2 * SUBLANES) == 0
        cast_in.append(pl.BlockSpec((1, 1, rows // per_group, cols),
                                    lambda i, lead=lead, per=per_group: (lead, i // per, i % per, 0)))
        cast_out.append(pl.BlockSpec((1, rows // per_group, cols), lambda i, per=per_group: (i // per, i % per, 0)))
        cast_shapes.append(jax.ShapeDtypeStruct((E, rows, cols), BF16))
    out, *cast = pl.pallas_call(
        _ffn_kernel,
        grid=(n_steps,),
        in_specs=[pl.BlockSpec((tm, D), lambda i: (i, 0)), single((1, D)),
                  single((D, F)), single((D, F)), single((F, D))] + cast_in,
        out_specs=[pl.BlockSpec((tm, D), lambda i: (i, 0))] + cast_out,
        out_shape=[jax.ShapeDtypeStruct((T, D), F32)] + cast_shapes,
        compiler_params=pltpu.CompilerParams(
            dimension_semantics=("arbitrary",), vmem_limit_bytes=VMEM_LIMIT),
        name="swiglu",
    )(x2, g.reshape(1, -1), wg, wu, wd, *[w for w, _ in cast_weights])
    return out, cast


def _moe_kernel(tidx_ref, texp_ref, trows_ref, nvalid_ref, xs_ref, wg_ref, wu_ref, wd_ref, y_ref):
    del tidx_ref, texp_ref
    g = pl.program_id(0)
    tm = xs_ref.shape[1]
    live = g < nvalid_ref[0]
    half_full = trows_ref[g] <= tm // 2

    def run(n):
        h = _unpack_bf16_pairs(_load_tiles(xs_ref, n)).astype(BF16)
        _store_tiles(y_ref, _pack_bf16_pairs(_swiglu_chunked(h, wg_ref.at[0], wu_ref.at[0], wd_ref.at[0])))

    pl.when(jnp.logical_and(live, jnp.logical_not(half_full)))(lambda: run(tm))
    pl.when(jnp.logical_and(live, half_full))(lambda: run(tm // 2))


def _moe_experts(xs, tile_idx, tile_expert, tile_rows, n_valid, wg, wu, wd):
    E, D, F = wg.shape
    tm = MOE_TM
    P = xs.shape[1]
    half = xs.shape[0] * LANES
    row_block = lambda g, ti, te, tr, nv: ti[g]
    w_map = lambda g, ti, te, tr, nv: (te[g], 0, 0)
    return pl.pallas_call(
        _moe_kernel,
        grid_spec=pltpu.PrefetchScalarGridSpec(
            num_scalar_prefetch=4,
            grid=(P // tm,),
            in_specs=[_tiled_spec(tm, half, row_block),
                      pl.BlockSpec((1, D, F), w_map), pl.BlockSpec((1, D, F), w_map),
                      pl.BlockSpec((1, F, D), w_map)],
            out_specs=_tiled_spec(tm, D // 2, row_block)),
        out_shape=_tiled_shape(P, D // 2, U32),
        compiler_params=pltpu.CompilerParams(
            dimension_semantics=("arbitrary",), vmem_limit_bytes=VMEM_LIMIT),
        name="moe_experts",
    )(tile_idx, tile_expert, tile_rows, n_valid, xs, wg, wu, wd)


def _combine_kernel(x_ref, y1_ref, y2_ref, meta_ref, fg_ref, *refs, final_norm):
    o_ref = refs[-1]
    rec = meta_ref[...]
    rec_t = jnp.concatenate([rec, jnp.zeros((LANES - rec.shape[0], rec.shape[1]), F32)], axis=0).T
    lane = lax.broadcasted_iota(I32, rec_t.shape, 1)
    gate = lambda row: jnp.sum(jnp.where(lane == row, rec_t, 0.0), axis=-1, keepdims=True)
    out = x_ref[...] + gate(META_G1) * _unpack_bf16_pairs(_load_tiles(y1_ref.at[0]))
    out = out + gate(META_G2) * _unpack_bf16_pairs(_load_tiles(y2_ref.at[0]))
    if final_norm:
        out = _rms(out, fg_ref[...])
    o_ref[...] = out


def _combine(x2, y12, meta, final_g, first_row, partial_out, *, final_norm):
    T, D = x2.shape
    n_rows = meta.shape[1]
    tm = FFN_TM
    first_block = first_row // tm
    rows = lambda i: (first_block + i, 0)
    args = [x2, y12, y12, meta, final_g.reshape(1, -1)]
    in_specs = [pl.BlockSpec((tm, D), rows),
                _tiled_spec(tm, D // 2, lambda i: i, lead=0), _tiled_spec(tm, D // 2, lambda i: i, lead=1),
                pl.BlockSpec((SUBLANES, tm), lambda i: (0, i)), _const_spec((1, D))]
    aliases = {}
    if partial_out is not None:
        args.append(partial_out)
        in_specs.append(pl.BlockSpec(memory_space=pl.ANY))
        aliases = {len(args) - 1: 0}
    return pl.pallas_call(
        functools.partial(_combine_kernel, final_norm=final_norm),
        grid=(n_rows // tm,),
        in_specs=in_specs,
        out_specs=pl.BlockSpec((tm, D), rows),
        out_shape=jax.ShapeDtypeStruct((T, D), F32),
        input_output_aliases=aliases,
        compiler_params=pltpu.CompilerParams(
            dimension_semantics=("arbitrary",), vmem_limit_bytes=VMEM_LIMIT),
        name="moe_combine",
    )(*args)


def _route_and_dispatch(x2, g, router_w, first_row, n_rows):
    tm = MOE_TM
    n_tiles = (n_rows * TOP_K) // tm + N_EXPERTS
    meta, hp, counts = _router(x2, g, router_w, first_row, n_rows)

    cnt = counts[:, 0].astype(I32)
    tiles_e = (cnt + tm - 1) // tm
    tile_end = jnp.cumsum(tiles_e)
    row_start = (tile_end - tiles_e) * tm
    experts = jnp.arange(N_EXPERTS, dtype=I32)[:, None]

    def dest(i_row, r_row):
        start = jnp.sum(jnp.where(meta[i_row].astype(I32) == experts, row_start[:, None], 0), axis=0)
        return start + meta[r_row].astype(I32)

    dests = jnp.stack([dest(META_I1, META_R1), dest(META_I2, META_R2)])
    n_valid = tile_end[-1:]
    tile_idx = jnp.minimum(jnp.arange(n_tiles, dtype=I32), n_valid - 1)
    tile_expert = jnp.minimum(jnp.sum(tile_end[:, None] <= tile_idx, axis=0), N_EXPERTS - 1).astype(I32)
    tile_rows = jnp.clip(cnt[tile_expert] - (tile_idx - (tile_end - tiles_e)[tile_expert]) * tm, 0, tm).astype(I32)
    xs = _sc_dispatch(hp, dests[0], dests[1], n_tiles * tm)
    return xs, (tile_idx, tile_expert, tile_rows, n_valid.astype(I32)), dests, meta


def _moe_layer(x2, g, router_w, wg, wu, wd, final_g, *, final_norm):
    T = x2.shape[0]
    n_rows = T // MOE_ROW_GROUPS
    starts = [grp * n_rows for grp in range(MOE_ROW_GROUPS)]
    routed = [_route_and_dispatch(x2, g, router_w, first_row, n_rows) for first_row in starts]
    ys = [_moe_experts(xs, *tiles, wg, wu, wd) for xs, tiles, _, _ in routed]
    y12s = [_sc_gather(y, dests) for y, (_, _, dests, _) in zip(ys, routed)]
    out = None
    for first_row, y12, (_, _, _, meta) in zip(starts, y12s, routed):
        out = _combine(x2, y12, meta, final_g, first_row, out, final_norm=final_norm)
    return out


def kernel(x, mem, mix_norm_g, w_in, conv_w, conv_b, conv_ln_g, conv_ln_b, sgu_ln_g, sgu_ln_b, sgu_w, sgu_b,
           pool_w, pool_b, pool_scale, w_out, xattn_norm_g, mem_norm_g, xattn_wq, xattn_wk, xattn_wv, xattn_wo,
           ffn_norm_g, ffn_wg, ffn_wu, ffn_wd, router_w, moe_wg, moe_wu, moe_wd, final_norm_g):
    B, S, D = x.shape
    bf = lambda a: a.astype(BF16)
    layer_mats = dict(w_in=w_in, w_out=w_out, wq=xattn_wq, wk=xattn_wk, wv=xattn_wv, wo=xattn_wo)
    precast = {}

    def mat(name, l):
        return precast[name, l] if (name, l) in precast else bf(layer_mats[name][l])

    for l in range(DEPTH):
        sgu_bias = jnp.repeat(sgu_b[l].T, D_SGU // SGU_HEADS, axis=1)
        pool_wbd = jax.scipy.linalg.block_diag(*[pool_w[l, gi] for gi in range(len(POOL_WINDOWS))])
        mix_params = (mix_norm_g[l], mat("w_in", l), conv_w[l], conv_b[l], conv_ln_g[l], conv_ln_b[l],
                      sgu_ln_g[l], sgu_ln_b[l], sgu_w[l], sgu_bias, bf(pool_wbd), pool_b[l].reshape(-1),
                      pool_scale[l], mat("w_out", l))
        k, v = _kv(mem, mem_norm_g[l], mat("wk", l), mat("wv", l))
        x = _mix_attn(x, mix_params, k, v, (xattn_norm_g[l], mat("wq", l), mat("wo", l)))
        x2 = x.reshape(B * S, D)
        j = l // 2
        if l % 2 == 0:
            assert l != DEPTH - 1, "the final RMSNorm is fused into the routed layer's combine kernel"
            names = list(layer_mats)
            split_rows = lambda a: a.reshape(a.shape[0], N_EXPERTS, a.shape[1] // N_EXPERTS, a.shape[2])
            side = [(moe_wg, j), (moe_wu, j), (moe_wd, j)] + [(split_rows(layer_mats[n]), l + 1) for n in names]
            x2, cast = _ffn(x2, ffn_norm_g[l], bf(ffn_wg[j]), bf(ffn_wu[j]), bf(ffn_wd[j]), cast_weights=side)
            moe_bf16 = cast[:3]
            for n, c in zip(names, cast[3:]):
                precast[n, l + 1] = c.reshape(layer_mats[n].shape[1:])
        else:
            x2 = _moe_layer(x2, ffn_norm_g[l], router_w[j], *moe_bf16, final_norm_g, final_norm=l == DEPTH - 1)
        x = x2.reshape(B, S, D)
    return x
```

```python
import functools

import jax
import jax.numpy as jnp
from jax import lax
from jax.experimental import pallas as pl
from jax.experimental.pallas import tpu as pltpu
from jax.experimental.pallas import tpu_sc as plsc

F32 = jnp.float32
BF16 = jnp.bfloat16
U32 = jnp.uint32
I32 = jnp.int32

D_MODEL = 1024
DEPTH = 2
CHUNK = 64
D_CONV = 384
CONV_WIDTH = 31
D_SGU = 384
SGU_HEADS = 4
SGU_CHUNK = 128
D_POOL = 256
POOL_WINDOWS = (2, 4, 8, 16)
POOL_GROUP_DIM = D_POOL // len(POOL_WINDOWS)
D_MIX = D_CONV + D_SGU + D_POOL
D_IN = 2 * D_CONV + 2 * D_SGU + D_POOL
X_HEADS = 4
X_HEAD_DIM = D_MODEL // X_HEADS
N_EXPERTS = 8
TOP_K = 2
EPS = 1e-6
LOG2_E = 1.4426950408889634

LANES = 128
SUBLANES = 8
HIST = 32
MIX_TS = 512
CONV_RB = 64
FFN_TM = 512
FFN_FC = 1408
MXU_WIDTH = 256
MOE_TM = 512
MOE_ROW_GROUPS = 2
SC_WIN = 128
HI16 = 0xFFFF0000
VMEM_LIMIT = 56 * 1024 * 1024


def _rms(x, g):
    return x * lax.rsqrt(jnp.mean(x * x, axis=-1, keepdims=True) + EPS) * g


def _layer_norm(x, g, b):
    mu = jnp.mean(x, axis=-1, keepdims=True)
    xc = x - mu
    var = jnp.mean(xc * xc, axis=-1, keepdims=True)
    return xc * lax.rsqrt(var + EPS) * g + b


def _dot(a, b):
    return jnp.dot(a, b, preferred_element_type=F32)


def _round_robin(*generators):
    live = list(generators)
    while live:
        for gen in list(live):
            try:
                next(gen)
                yield
            except StopIteration:
                live.remove(gen)


def _mixer_tile(x, s, g_ref, win_ref, convw_ref, convb_ref, clng_ref, clnb_ref,
                slng_ref, slnb_ref, sguw_ref, sgub_ref, poolw_ref, poolb_ref, pscale_ref,
                wout_ref, cbuf, cshift, pb0, pb1, pb2, pb3, overlap):
    ts = MIX_TS
    a_end = 2 * D_CONV
    b_end = a_end + 2 * D_SGU
    h = _rms(x, g_ref[...]).astype(BF16)
    z_a = _dot(h, win_ref[:, :a_end])
    z_rest = []

    def project_rest():
        for lo in range(a_end, D_IN, MXU_WIDTH):
            z_rest.append(_dot(h, win_ref[:, lo:lo + MXU_WIDTH]))
            yield

    pieces = _round_robin(project_rest(), overlap)

    cbuf[HIST:HIST + ts, :] = z_a[:, :D_CONV] * jax.nn.sigmoid(z_a[:, D_CONV:])
    span = ts + HIST - SUBLANES
    for r in range(1, SUBLANES):
        cshift[r - 1, 0:span, :] = cbuf[r:r + span, :]
    ya_blocks = []
    n_blocks = ts // CONV_RB
    for rb in range(n_blocks):
        acc = jnp.zeros((CONV_RB, D_CONV), F32)
        for k in range(CONV_WIDTH):
            start = rb * CONV_RB + HIST - (CONV_WIDTH - 1) + k
            r, base = start % SUBLANES, start - start % SUBLANES
            src = cbuf[base:base + CONV_RB, :] if r == 0 else cshift[r - 1, base:base + CONV_RB, :]
            acc = acc + convw_ref[k:k + 1, :] * src
        ya_blocks.append(acc)
        for _ in range(2 if rb < n_blocks // 2 else 1):
            next(pieces, None)
    for _ in pieces:
        pass
    ya = jnp.concatenate(ya_blocks, axis=0) + convb_ref[...]
    ya = _layer_norm(ya, clng_ref[...], clnb_ref[...])
    ya = ya * jax.nn.sigmoid(ya)
    cbuf[0:HIST, :] = cbuf[ts:ts + HIST, :]
    out = x + _dot(ya.astype(BF16), wout_ref[0:D_CONV, :])

    z = jnp.concatenate(z_rest, axis=1)
    zb = jax.nn.gelu(z[:, :b_end - a_end])
    u = zb[:, :D_SGU]
    v = _layer_norm(zb[:, D_SGU:], slng_ref[...], slnb_ref[...]).astype(BF16)
    blk_r = lax.broadcasted_iota(jnp.int32, (SGU_CHUNK, SGU_CHUNK), 0) // CHUNK
    blk_c = lax.broadcasted_iota(jnp.int32, (SGU_CHUNK, SGU_CHUNK), 1) // CHUNK
    head_of_lane = lax.broadcasted_iota(jnp.int32, (SGU_CHUNK, D_SGU), 1) // (D_SGU // SGU_HEADS)
    w_heads = [jnp.where(blk_r >= blk_c, sguw_ref[hd], 0.0).astype(BF16) for hd in range(SGU_HEADS)]
    w_cat = jnp.concatenate(w_heads, axis=1)
    zero = jnp.zeros((), BF16)
    s_chunks = []
    for c in range(ts // SGU_CHUNK):
        vc = v[c * SGU_CHUNK:(c + 1) * SGU_CHUNK, :]
        v_stack = jnp.concatenate([jnp.where(head_of_lane == hd, vc, zero) for hd in range(SGU_HEADS)], axis=0)
        s_chunks.append(_dot(w_cat, v_stack) + sgub_ref[...])
    yb = u * jnp.concatenate(s_chunks, axis=0)
    out = out + _dot(yb.astype(BF16), wout_ref[D_CONV:D_CONV + D_SGU, :])

    cc = z[:, b_end - a_end:]
    pb0[HIST:HIST + ts, :] = cc
    pb1[0:ts + 24, :] = pb0[8:ts + 32, :] + pb0[7:ts + 31, :]
    pb2[0:ts + 16, :] = pb1[8:ts + 24, :] + pb1[6:ts + 22, :]
    pb3[0:ts + 8, :] = pb2[8:ts + 16, :] + pb2[4:ts + 12, :]
    s16 = pb3[8:ts + 8, :] + pb3[0:ts, :]
    s8 = pb3[8:ts + 8, :]
    s4 = pb2[16:ts + 16, :]
    s2 = pb1[24:ts + 24, :]
    grp = lax.broadcasted_iota(jnp.int32, (ts, D_POOL), 1) // POOL_GROUP_DIM
    wsum = jnp.where(grp == 0, s2, jnp.where(grp == 1, s4, jnp.where(grp == 2, s8, s16)))
    head = max(POOL_WINDOWS)
    inv_win = jnp.where(grp == 0, 0.5, jnp.where(grp == 1, 0.25, jnp.where(grp == 2, 0.125, 0.0625)))
    grp_h = lax.broadcasted_iota(jnp.int32, (head, D_POOL), 1) // POOL_GROUP_DIM
    win = jnp.where(grp_h == 0, 2, jnp.where(grp_h == 1, 4, jnp.where(grp_h == 2, 8, 16)))
    pos = s * ts + lax.broadcasted_iota(jnp.int32, (head, D_POOL), 0)
    cnt = jnp.minimum(pos + 1, win).astype(F32)
    mean = jnp.concatenate([wsum[:head] / cnt, (wsum * inv_win)[head:]], axis=0)
    p = (mean - cc).astype(BF16)
    yc = (_dot(p, poolw_ref[...]) + poolb_ref[...]) * pscale_ref[...]
    pb0[0:HIST, :] = pb0[ts:ts + HIST, :]

    return out + _dot(yc.astype(BF16), wout_ref[D_CONV + D_SGU:D_MIX, :])


def _const_spec(shape):
    zeros = (0,) * len(shape)
    return pl.BlockSpec(shape, lambda *_: zeros)


N_MIX_PARAMS = 14
N_ATT_PARAMS = 5


def _mix_attn_kernel(x_ref, *refs, n_tiles, tiles_per_row):
    mix_refs = refs[:N_MIX_PARAMS]
    att_refs = refs[N_MIX_PARAMS:N_MIX_PARAMS + N_ATT_PARAMS]
    o_ref, mid, cbuf, cshift, pb0, pb1, pb2, pb3 = refs[N_MIX_PARAMS + N_ATT_PARAMS:]
    i = pl.program_id(0)
    s = jnp.minimum(i, n_tiles - 1) % tiles_per_row

    @pl.when(i == 0)
    def _():
        mid[...] = jnp.zeros_like(mid)

    @pl.when(s == 0)
    def _():
        cbuf[0:HIST, :] = jnp.zeros((HIST, D_CONV), F32)
        pb0[0:HIST, :] = jnp.zeros((HIST, D_POOL), F32)

    attention = _xattn_pieces(mid[...], o_ref, *att_refs)
    mid[...] = _mixer_tile(x_ref[0], s, *mix_refs, cbuf, cshift, pb0, pb1, pb2, pb3, overlap=attention)


def _mix_attn(x, mix_params, k, v, att_params):
    B, S, D = x.shape
    M = k.shape[1]
    ts = MIX_TS
    tiles_per_row = S // ts
    n_tiles = B * tiles_per_row
    g_att, wq, wo = att_params
    row = lambda a: a.reshape(1, -1) if a.ndim == 1 else a
    mix_args = [row(a) for a in mix_params]
    att_args = [row(g_att), wq, k, v, wo]
    assert len(mix_args) == N_MIX_PARAMS and len(att_args) == N_ATT_PARAMS

    def mix_tile(i):
        t = jnp.minimum(i, n_tiles - 1)
        return t // tiles_per_row, t % tiles_per_row, 0

    def att_tile(i):
        t = jnp.maximum(i - 1, 0)
        return t // tiles_per_row, t % tiles_per_row, 0

    single = lambda a: pl.BlockSpec(a.shape, lambda i, nd=a.ndim: (0,) * nd, pipeline_mode=pl.Buffered(1))
    kv_spec = pl.BlockSpec((1, M, D), lambda i: (att_tile(i)[0], 0, 0))
    in_specs = [pl.BlockSpec((1, ts, D), mix_tile)] + [single(a) for a in mix_args]
    in_specs += [single(att_args[0]), single(wq), kv_spec, kv_spec, single(wo)]
    return pl.pallas_call(
        functools.partial(_mix_attn_kernel, n_tiles=n_tiles, tiles_per_row=tiles_per_row),
        grid=(n_tiles + 1,),
        in_specs=in_specs,
        out_specs=pl.BlockSpec((1, ts, D), att_tile),
        out_shape=jax.ShapeDtypeStruct((B, S, D), F32),
        scratch_shapes=[pltpu.VMEM((ts, D), F32),
                        pltpu.VMEM((ts + HIST, D_CONV), F32),
                        pltpu.VMEM((SUBLANES - 1, ts + HIST - SUBLANES, D_CONV), F32)]
                       + [pltpu.VMEM((ts + HIST, D_POOL), F32)] * 4,
        compiler_params=pltpu.CompilerParams(
            dimension_semantics=("arbitrary",), vmem_limit_bytes=VMEM_LIMIT),
        name="mix_attn",
    )(x, *mix_args, *att_args)


def _kv_kernel(mem_ref, g_ref, wk_ref, wv_ref, k_ref, v_ref):
    m = _rms(mem_ref[0], g_ref[...]).astype(BF16)
    k_ref[0] = _dot(m, wk_ref[...]).astype(BF16)
    v_ref[0] = _dot(m, wv_ref[...]).astype(BF16)


def _kv(mem, g, wk, wv):
    B, M, D = mem.shape
    return pl.pallas_call(
        _kv_kernel,
        grid=(B,),
        in_specs=[pl.BlockSpec((1, M, D), lambda b: (b, 0, 0)), _const_spec((1, D)),
                  _const_spec((D, D)), _const_spec((D, D))],
        out_specs=[pl.BlockSpec((1, M, D), lambda b: (b, 0, 0))] * 2,
        out_shape=[jax.ShapeDtypeStruct((B, M, D), BF16)] * 2,
        compiler_params=pltpu.CompilerParams(
            dimension_semantics=("arbitrary",), vmem_limit_bytes=VMEM_LIMIT),
        name="mem_kv",
    )(mem, g.reshape(1, -1), wk, wv)


def _xattn_pieces(x, o_ref, g_ref, wq_ref, k_ref, v_ref, wo_ref):
    h = _rms(x, g_ref[...]).astype(BF16)
    heads = []
    for hd in range(X_HEADS):
        sl = slice(hd * X_HEAD_DIM, (hd + 1) * X_HEAD_DIM)
        q = _dot(h, wq_ref[:, sl]).astype(BF16)
        sc = lax.dot_general(q, k_ref[0, :, sl], (((1,), (1,)), ((), ())), preferred_element_type=F32)
        e = jnp.exp2((sc - jnp.max(sc, axis=-1, keepdims=True)) * (X_HEAD_DIM ** -0.5 * LOG2_E))
        heads.append(_dot(e.astype(BF16), v_ref[0, :, sl]) / jnp.sum(e, axis=-1, keepdims=True))
        yield
    o = jnp.concatenate(heads, axis=-1).astype(BF16)
    for c in range(X_HEADS):
        sl = slice(c * X_HEAD_DIM, (c + 1) * X_HEAD_DIM)
        o_ref[0, :, sl] = x[:, sl] + _dot(o, wo_ref[:, sl])
        yield


def _store_tiles(ref, val):
    n = val.shape[0]
    for j in range(ref.shape[0]):
        ref[j, :n, :] = val[:, j * LANES:(j + 1) * LANES]


def _load_tiles(ref, n=None):
    n = ref.shape[1] if n is None else n
    return jnp.concatenate([ref[j, :n, :] for j in range(ref.shape[0])], axis=1)


def _pack_bf16_pairs(v):
    bits = lax.bitcast_convert_type(v.astype(BF16).astype(F32), U32)
    half = bits.shape[1] // 2
    return (bits[:, :half] & jnp.uint32(HI16)) | (bits[:, half:] >> 16)


def _unpack_bf16_pairs(w):
    hi = lax.bitcast_convert_type(w & jnp.uint32(HI16), F32)
    lo = lax.bitcast_convert_type(w << 16, F32)
    return jnp.concatenate([hi, lo], axis=1)


def _tiled_shape(rows, width, dtype):
    return jax.ShapeDtypeStruct((width // LANES, rows, LANES), dtype)


def _tiled_spec(rows, width, row_block, lead=None):
    block = (width // LANES, rows, LANES)
    if lead is None:
        return pl.BlockSpec(block, lambda *a: (0, row_block(*a), 0))
    return pl.BlockSpec((1,) + block, lambda *a: (lead, 0, row_block(*a), 0))


def _piece_index(rows, k, n_rows):
    return jnp.arange(k, dtype=I32)[:, None] * n_rows + rows[..., None, :]


META_I1, META_I2, META_G1, META_G2, META_R1, META_R2 = range(6)


def _router_kernel(x_ref, g_ref, rw_ref, triu_ref, meta_ref, hp_ref, cnt_ref, carry):
    @pl.when(pl.program_id(0) == 0)
    def _():
        carry[...] = jnp.zeros_like(carry)

    h = _rms(x_ref[...], g_ref[...])
    h_hi = h.astype(BF16)
    h_lo = (h - h_hi.astype(F32)).astype(BF16)
    logits = _dot(jnp.concatenate([h_hi, h_hi, h_lo], axis=1), rw_ref[...])

    neg = jnp.float32(-jnp.inf)
    lane = lax.broadcasted_iota(I32, logits.shape, 1)
    lt = jnp.where(lane < N_EXPERTS, logits, neg).T[:N_EXPERTS, :]
    expert = lax.broadcasted_iota(I32, lt.shape, 0)
    m1 = jnp.max(lt, axis=0, keepdims=True)
    i1 = jnp.min(jnp.where(lt == m1, expert, N_EXPERTS), axis=0, keepdims=True)
    lt2 = jnp.where(expert == i1, neg, lt)
    m2 = jnp.max(lt2, axis=0, keepdims=True)
    i2 = jnp.min(jnp.where(lt2 == m2, expert, N_EXPERTS), axis=0, keepdims=True)
    d = jnp.exp(m2 - m1)
    g1 = 1.0 / (1.0 + d)
    g2 = d / (1.0 + d)

    sel1 = expert == i1
    sel2 = expert == i2
    onehot = jnp.where(jnp.logical_or(sel1, sel2), 1.0, 0.0)
    before = _dot(onehot, triu_ref[...]) + carry[:, 0:1]
    r1 = jnp.sum(jnp.where(sel1, before, 0.0), axis=0, keepdims=True)
    r2 = jnp.sum(jnp.where(sel2, before, 0.0), axis=0, keepdims=True)
    carry[...] += jnp.sum(onehot, axis=1, keepdims=True)
    cnt_ref[...] = carry[...]

    rows = [i1.astype(F32), i2.astype(F32), g1, g2, r1, r2]
    meta_ref[...] = jnp.concatenate(rows + [jnp.zeros_like(g1)] * (SUBLANES - len(rows)), axis=0)
    _store_tiles(hp_ref, _pack_bf16_pairs(h))


def _router(x2, g, router_w, first_row, n_rows):
    D = x2.shape[1]
    tm = FFN_TM
    first_block = first_row // tm
    assert N_EXPERTS == SUBLANES
    rw = jnp.zeros((D, LANES), F32).at[:, :N_EXPERTS].set(router_w)
    rw_hi = rw.astype(BF16)
    rw_lo = (rw - rw_hi.astype(F32)).astype(BF16)
    rw_split = jnp.concatenate([rw_hi, rw_lo, rw_hi], axis=0)
    triu = jnp.triu(jnp.ones((tm, tm), F32), 1)
    return pl.pallas_call(
        _router_kernel,
        grid=(n_rows // tm,),
        in_specs=[pl.BlockSpec((tm, D), lambda i: (first_block + i, 0)), _const_spec((1, D)),
                  _const_spec((3 * D, LANES)), _const_spec((tm, tm))],
        out_specs=[pl.BlockSpec((SUBLANES, tm), lambda i: (0, i)),
                   _tiled_spec(tm, D // 2, lambda i: i), _const_spec((N_EXPERTS, LANES))],
        out_shape=[jax.ShapeDtypeStruct((SUBLANES, n_rows), F32),
                   _tiled_shape(n_rows, D // 2, U32), jax.ShapeDtypeStruct((N_EXPERTS, LANES), F32)],
        scratch_shapes=[pltpu.VMEM((N_EXPERTS, LANES), F32)],
        compiler_params=pltpu.CompilerParams(
            dimension_semantics=("arbitrary",), vmem_limit_bytes=VMEM_LIMIT),
        name="router",
    )(x2, g.reshape(1, -1), rw_split, triu)


def _sc_mesh():
    return plsc.VectorSubcoreMesh(core_axis_name="core", subcore_axis_name="subcore")


def _sc_dispatch(tiled, dest1, dest2, n_out):
    k, n, _ = tiled.shape
    d1, d2 = (_piece_index(d, k, n_out).reshape(-1) for d in (dest1, dest2))
    return _sc_scatter_pieces(tiled.reshape(k * n, LANES), d1, d2, k * n_out).reshape(k, n_out, LANES)


def _sc_gather(tiled, idx):
    k, n_rows, _ = tiled.shape
    G, n = idx.shape
    out = _sc_gather_pieces(tiled.reshape(k * n_rows, LANES), _piece_index(idx, k, n_rows).reshape(-1))
    return out.reshape(G, k, n, LANES)


def _sc_scatter_pieces(rows, dest1, dest2, n_out):
    T, W = rows.shape
    win = SC_WIN

    @pl.kernel(out_type=jax.ShapeDtypeStruct((n_out, W), rows.dtype), mesh=_sc_mesh(), scratch_types=[])
    def scatter_kernel(x_hbm, i1_hbm, i2_hbm, o_hbm):
        def body(x_vmem, i1_vmem, i2_vmem):
            pltpu.sync_copy(x_vmem, o_hbm.at[i1_vmem.at[0]])
            pltpu.sync_copy(x_vmem, o_hbm.at[i2_vmem.at[0]])

        pltpu.emit_pipeline(
            body,
            grid=(T // win,),
            in_specs=[pl.BlockSpec((win, W), lambda i: (i, 0)),
                      pl.BlockSpec((1, win), lambda i: (0, i)),
                      pl.BlockSpec((1, win), lambda i: (0, i))],
            out_specs=[],
            core_axis_name=("core", "subcore"),
            dimension_semantics=(pltpu.PARALLEL,),
        )(x_hbm, i1_hbm, i2_hbm)

    return scatter_kernel(rows, dest1.reshape(1, T), dest2.reshape(1, T))


def _sc_gather_pieces(rows, idx):
    n = idx.shape[0]
    W = rows.shape[1]
    win = SC_WIN

    @pl.kernel(out_type=jax.ShapeDtypeStruct((n, W), rows.dtype), mesh=_sc_mesh(), scratch_types=[])
    def gather_kernel(x_hbm, i_hbm, o_hbm):
        def body(i_vmem, o_vmem):
            pltpu.sync_copy(x_hbm.at[i_vmem.at[0]], o_vmem)

        pltpu.emit_pipeline(
            body,
            grid=(n // win,),
            in_specs=[pl.BlockSpec((1, win), lambda i: (0, i))],
            out_specs=[pl.BlockSpec((win, W), lambda i: (i, 0))],
            core_axis_name=("core", "subcore"),
            dimension_semantics=(pltpu.PARALLEL,),
        )(i_hbm, o_hbm)

    return gather_kernel(rows, idx.reshape(1, n))


def _swiglu(h, wg, wu, wd):
    gate = _dot(h, wg)
    up = _dot(h, wu)
    return _dot((gate * jax.nn.sigmoid(gate) * up).astype(BF16), wd)


def _swiglu_chunked(h, wg_ref, wu_ref, wd_ref, acc=None):
    F = wg_ref.shape[1]
    n_chunks = -(-F // FFN_FC)
    cols = -(-F // (n_chunks * MXU_WIDTH)) * MXU_WIDTH
    for lo in range(0, F, cols):
        sl = slice(lo, min(lo + cols, F))
        y = _swiglu(h, wg_ref[:, sl], wu_ref[:, sl], wd_ref[sl, :])
        acc = y if acc is None else acc + y
    return acc


def _ffn_kernel(x_ref, g_ref, wg_ref, wu_ref, wd_ref, *refs):
    n_cast = len(refs) // 2
    o_ref = refs[n_cast]
    x = x_ref[...]
    h = _rms(x, g_ref[...]).astype(BF16)
    o_ref[...] = _swiglu_chunked(h, wg_ref, wu_ref, wd_ref, acc=x)
    for src_ref, dst_ref in zip(refs[:n_cast], refs[n_cast + 1:]):
        dst_ref[...] = src_ref[0].astype(BF16)


def _ffn(x2, g, wg, wu, wd, cast_weights=()):
    T, D = x2.shape
    F = wg.shape[1]
    tm = FFN_TM
    n_steps = T // tm
    single = lambda shape: pl.BlockSpec(shape, lambda i: (0,) * len(shape), pipeline_mode=pl.Buffered(1))
    cast_in, cast_out, cast_shapes = [], [], []
    for w, lead in cast_weights:
        _, E, rows, cols = w.shape
        per_group = n_steps // E
        assert n_steps % E == 0 and rows % (per_group * 2 * SUBLANES) == 0
        cast_in.append(pl.BlockSpec((1, 1, rows // per_group, cols),
                                    lambda i, lead=lead, per=per_group: (lead, i // per, i % per, 0)))
        cast_out.append(pl.BlockSpec((1, rows // per_group, cols), lambda i, per=per_group: (i // per, i % per, 0)))
        cast_shapes.append(jax.ShapeDtypeStruct((E, rows, cols), BF16))
    out, *cast = pl.pallas_call(
        _ffn_kernel,
        grid=(n_steps,),
        in_specs=[pl.BlockSpec((tm, D), lambda i: (i, 0)), single((1, D)),
                  single((D, F)), single((D, F)), single((F, D))] + cast_in,
        out_specs=[pl.BlockSpec((tm, D), lambda i: (i, 0))] + cast_out,
        out_shape=[jax.ShapeDtypeStruct((T, D), F32)] + cast_shapes,
        compiler_params=pltpu.CompilerParams(
            dimension_semantics=("arbitrary",), vmem_limit_bytes=VMEM_LIMIT),
        name="swiglu",
    )(x2, g.reshape(1, -1), wg, wu, wd, *[w for w, _ in cast_weights])
    return out, cast


def _moe_kernel(tidx_ref, texp_ref, trows_ref, nvalid_ref, xs_ref, wg_ref, wu_ref, wd_ref, y_ref):
    del tidx_ref, texp_ref
    g = pl.program_id(0)
    tm = xs_ref.shape[1]
    live = g < nvalid_ref[0]
    half_full = trows_ref[g] <= tm // 2

    def run(n):
        h = _unpack_bf16_pairs(_load_tiles(xs_ref, n)).astype(BF16)
        _store_tiles(y_ref, _pack_bf16_pairs(_swiglu_chunked(h, wg_ref.at[0], wu_ref.at[0], wd_ref.at[0])))

    pl.when(jnp.logical_and(live, jnp.logical_not(half_full)))(lambda: run(tm))
    pl.when(jnp.logical_and(live, half_full))(lambda: run(tm // 2))


def _moe_experts(xs, tile_idx, tile_expert, tile_rows, n_valid, wg, wu, wd):
    E, D, F = wg.shape
    tm = MOE_TM
    P = xs.shape[1]
    half = xs.shape[0] * LANES
    row_block = lambda g, ti, te, tr, nv: ti[g]
    w_map = lambda g, ti, te, tr, nv: (te[g], 0, 0)
    return pl.pallas_call(
        _moe_kernel,
        grid_spec=pltpu.PrefetchScalarGridSpec(
            num_scalar_prefetch=4,
            grid=(P // tm,),
            in_specs=[_tiled_spec(tm, half, row_block),
                      pl.BlockSpec((1, D, F), w_map), pl.BlockSpec((1, D, F), w_map),
                      pl.BlockSpec((1, F, D), w_map)],
            out_specs=_tiled_spec(tm, D // 2, row_block)),
        out_shape=_tiled_shape(P, D // 2, U32),
        compiler_params=pltpu.CompilerParams(
            dimension_semantics=("arbitrary",), vmem_limit_bytes=VMEM_LIMIT),
        name="moe_experts",
    )(tile_idx, tile_expert, tile_rows, n_valid, xs, wg, wu, wd)


def _combine_kernel(x_ref, y1_ref, y2_ref, meta_ref, fg_ref, *refs, final_norm):
    o_ref = refs[-1]
    rec = meta_ref[...]
    rec_t = jnp.concatenate([rec, jnp.zeros((LANES - rec.shape[0], rec.shape[1]), F32)], axis=0).T
    lane = lax.broadcasted_iota(I32, rec_t.shape, 1)
    gate = lambda row: jnp.sum(jnp.where(lane == row, rec_t, 0.0), axis=-1, keepdims=True)
    out = x_ref[...] + gate(META_G1) * _unpack_bf16_pairs(_load_tiles(y1_ref.at[0]))
    out = out + gate(META_G2) * _unpack_bf16_pairs(_load_tiles(y2_ref.at[0]))
    if final_norm:
        out = _rms(out, fg_ref[...])
    o_ref[...] = out


def _combine(x2, y12, meta, final_g, first_row, partial_out, *, final_norm):
    T, D = x2.shape
    n_rows = meta.shape[1]
    tm = FFN_TM
    first_block = first_row // tm
    rows = lambda i: (first_block + i, 0)
    args = [x2, y12, y12, meta, final_g.reshape(1, -1)]
    in_specs = [pl.BlockSpec((tm, D), rows),
                _tiled_spec(tm, D // 2, lambda i: i, lead=0), _tiled_spec(tm, D // 2, lambda i: i, lead=1),
                pl.BlockSpec((SUBLANES, tm), lambda i: (0, i)), _const_spec((1, D))]
    aliases = {}
    if partial_out is not None:
        args.append(partial_out)
        in_specs.append(pl.BlockSpec(memory_space=pl.ANY))
        aliases = {len(args) - 1: 0}
    return pl.pallas_call(
        functools.partial(_combine_kernel, final_norm=final_norm),
        grid=(n_rows // tm,),
        in_specs=in_specs,
        out_specs=pl.BlockSpec((tm, D), rows),
        out_shape=jax.ShapeDtypeStruct((T, D), F32),
        input_output_aliases=aliases,
        compiler_params=pltpu.CompilerParams(
            dimension_semantics=("arbitrary",), vmem_limit_bytes=VMEM_LIMIT),
        name="moe_combine",
    )(*args)


def _route_and_dispatch(x2, g, router_w, first_row, n_rows):
    tm = MOE_TM
    n_tiles = (n_rows * TOP_K) // tm + N_EXPERTS
    meta, hp, counts = _router(x2, g, router_w, first_row, n_rows)

    cnt = counts[:, 0].astype(I32)
    tiles_e = (cnt + tm - 1) // tm
    tile_end = jnp.cumsum(tiles_e)
    row_start = (tile_end - tiles_e) * tm
    experts = jnp.arange(N_EXPERTS, dtype=I32)[:, None]

    def dest(i_row, r_row):
        start = jnp.sum(jnp.where(meta[i_row].astype(I32) == experts, row_start[:, None], 0), axis=0)
        return start + meta[r_row].astype(I32)

    dests = jnp.stack([dest(META_I1, META_R1), dest(META_I2, META_R2)])
    n_valid = tile_end[-1:]
    tile_idx = jnp.minimum(jnp.arange(n_tiles, dtype=I32), n_valid - 1)
    tile_expert = jnp.minimum(jnp.sum(tile_end[:, None] <= tile_idx, axis=0), N_EXPERTS - 1).astype(I32)
    tile_rows = jnp.clip(cnt[tile_expert] - (tile_idx - (tile_end - tiles_e)[tile_expert]) * tm, 0, tm).astype(I32)
    xs = _sc_dispatch(hp, dests[0], dests[1], n_tiles * tm)
    return xs, (tile_idx, tile_expert, tile_rows, n_valid.astype(I32)), dests, meta


def _moe_layer(x2, g, router_w, wg, wu, wd, final_g, *, final_norm):
    T = x2.shape[0]
    n_rows = T // MOE_ROW_GROUPS
    starts = [grp * n_rows for grp in range(MOE_ROW_GROUPS)]
    routed = [_route_and_dispatch(x2, g, router_w, first_row, n_rows) for first_row in starts]
    ys = [_moe_experts(xs, *tiles, wg, wu, wd) for xs, tiles, _, _ in routed]
    y12s = [_sc_gather(y, dests) for y, (_, _, dests, _) in zip(ys, routed)]
    out = None
    for first_row, y12, (_, _, _, meta) in zip(starts, y12s, routed):
        out = _combine(x2, y12, meta, final_g, first_row, out, final_norm=final_norm)
    return out


def kernel(x, mem, mix_norm_g, w_in, conv_w, conv_b, conv_ln_g, conv_ln_b, sgu_ln_g, sgu_ln_b, sgu_w, sgu_b,
           pool_w, pool_b, pool_scale, w_out, xattn_norm_g, mem_norm_g, xattn_wq, xattn_wk, xattn_wv, xattn_wo,
           ffn_norm_g, ffn_wg, ffn_wu, ffn_wd, router_w, moe_wg, moe_wu, moe_wd, final_norm_g):
    B, S, D = x.shape
    bf = lambda a: a.astype(BF16)
    layer_mats = dict(w_in=w_in, w_out=w_out, wq=xattn_wq, wk=xattn_wk, wv=xattn_wv, wo=xattn_wo)
    precast = {}

    def mat(name, l):
        return precast[name, l] if (name, l) in precast else bf(layer_mats[name][l])

    for l in range(DEPTH):
        sgu_bias = jnp.repeat(sgu_b[l].T, D_SGU // SGU_HEADS, axis=1)
        pool_wbd = jax.scipy.linalg.block_diag(*[pool_w[l, gi] for gi in range(len(POOL_WINDOWS))])
        mix_params = (mix_norm_g[l], mat("w_in", l), conv_w[l], conv_b[l], conv_ln_g[l], conv_ln_b[l],
                      sgu_ln_g[l], sgu_ln_b[l], sgu_w[l], sgu_bias, bf(pool_wbd), pool_b[l].reshape(-1),
                      pool_scale[l], mat("w_out", l))
        k, v = _kv(mem, mem_norm_g[l], mat("wk", l), mat("wv", l))
        x = _mix_attn(x, mix_params, k, v, (xattn_norm_g[l], mat("wq", l), mat("wo", l)))
        x2 = x.reshape(B * S, D)
        j = l // 2
        if l % 2 == 0:
            assert l != DEPTH - 1, "the final RMSNorm is fused into the routed layer's combine kernel"
            names = list(layer_mats)
            split_rows = lambda a: a.reshape(a.shape[0], N_EXPERTS, a.shape[1] // N_EXPERTS, a.shape[2])
            side = [(moe_wg, j), (moe_wu, j), (moe_wd, j)] + [(split_rows(layer_mats[n]), l + 1) for n in names]
            x2, cast = _ffn(x2, ffn_norm_g[l], bf(ffn_wg[j]), bf(ffn_wu[j]), bf(ffn_wd[j]), cast_weights=side)
            moe_bf16 = cast[:3]
            for n, c in zip(names, cast[3:]):
                precast[n, l + 1] = c.reshape(layer_mats[n].shape[1:])
        else:
            x2 = _moe_layer(x2, ffn_norm_g[l], router_w[j], *moe_bf16, final_norm_g, final_norm=l == DEPTH - 1)
        x = x2.reshape(B, S, D)
    return x
```

```python
import functools

import jax
import jax.numpy as jnp
from jax import lax
from jax.experimental import pallas as pl
from jax.experimental.pallas import tpu as pltpu
from jax.experimental.pallas import tpu_sc as plsc

F32 = jnp.float32
BF16 = jnp.bfloat16
U32 = jnp.uint32
I32 = jnp.int32

D_MODEL = 1024
DEPTH = 2
CHUNK = 64
D_CONV = 384
CONV_WIDTH = 31
D_SGU = 384
SGU_HEADS = 4
SGU_CHUNK = 128
D_POOL = 256
POOL_WINDOWS = (2, 4, 8, 16)
POOL_GROUP_DIM = D_POOL // len(POOL_WINDOWS)
D_MIX = D_CONV + D_SGU + D_POOL
D_IN = 2 * D_CONV + 2 * D_SGU + D_POOL
X_HEADS = 4
X_HEAD_DIM = D_MODEL // X_HEADS
N_EXPERTS = 8
TOP_K = 2
EPS = 1e-6
LOG2_E = 1.4426950408889634

LANES = 128
SUBLANES = 8
HIST = 32
MIX_TS = 512
CONV_RB = 64
FFN_TM = 512
FFN_FC = 1408
MXU_WIDTH = 256
ROUTER_TM = 1024
MOE_TM = 512
MOE_ROW_GROUPS = 2
SC_WIN = 128
HI16 = 0xFFFF0000
VMEM_LIMIT = 56 * 1024 * 1024


def _rms(x, g):
    return x * lax.rsqrt(jnp.mean(x * x, axis=-1, keepdims=True) + EPS) * g


def _layer_norm(x, g, b):
    mu = jnp.mean(x, axis=-1, keepdims=True)
    xc = x - mu
    var = jnp.mean(xc * xc, axis=-1, keepdims=True)
    return xc * lax.rsqrt(var + EPS) * g + b


def _dot(a, b):
    return jnp.dot(a, b, preferred_element_type=F32)


def _round_robin(*generators):
    live = list(generators)
    while live:
        for gen in list(live):
            try:
                next(gen)
                yield
            except StopIteration:
                live.remove(gen)


def _mixer_tile(x, s, g_ref, win_ref, convw_ref, convb_ref, clng_ref, clnb_ref,
                slng_ref, slnb_ref, sguw_ref, sgub_ref, poolw_ref, poolb_ref, pscale_ref,
                wout_ref, cbuf, cshift, pb0, pb1, pb2, pb3, overlap):
    ts = MIX_TS
    a_end = 2 * D_CONV
    b_end = a_end + 2 * D_SGU
    h = _rms(x, g_ref[...]).astype(BF16)
    z_a = _dot(h, win_ref[:, :a_end])
    z_rest = []

    def project_rest():
        for lo in range(a_end, D_IN, MXU_WIDTH):
            z_rest.append(_dot(h, win_ref[:, lo:lo + MXU_WIDTH]))
            yield

    pieces = _round_robin(project_rest(), overlap)

    cbuf[HIST:HIST + ts, :] = z_a[:, :D_CONV] * jax.nn.sigmoid(z_a[:, D_CONV:])
    span = ts + HIST - SUBLANES
    for r in range(1, SUBLANES):
        cshift[r - 1, 0:span, :] = cbuf[r:r + span, :]
    ya_blocks = []
    n_blocks = ts // CONV_RB
    for rb in range(n_blocks):
        acc = jnp.zeros((CONV_RB, D_CONV), F32)
        for k in range(CONV_WIDTH):
            start = rb * CONV_RB + HIST - (CONV_WIDTH - 1) + k
            r, base = start % SUBLANES, start - start % SUBLANES
            src = cbuf[base:base + CONV_RB, :] if r == 0 else cshift[r - 1, base:base + CONV_RB, :]
            acc = acc + convw_ref[k:k + 1, :] * src
        ya_blocks.append(acc)
        for _ in range(2 if rb < n_blocks // 2 else 1):
            next(pieces, None)
    for _ in pieces:
        pass
    ya = jnp.concatenate(ya_blocks, axis=0) + convb_ref[...]
    ya = _layer_norm(ya, clng_ref[...], clnb_ref[...])
    ya = ya * jax.nn.sigmoid(ya)
    cbuf[0:HIST, :] = cbuf[ts:ts + HIST, :]
    out = x + _dot(ya.astype(BF16), wout_ref[0:D_CONV, :])

    z = jnp.concatenate(z_rest, axis=1)
    zb = jax.nn.gelu(z[:, :b_end - a_end])
    u = zb[:, :D_SGU]
    v = _layer_norm(zb[:, D_SGU:], slng_ref[...], slnb_ref[...]).astype(BF16)
    blk_r = lax.broadcasted_iota(jnp.int32, (SGU_CHUNK, SGU_CHUNK), 0) // CHUNK
    blk_c = lax.broadcasted_iota(jnp.int32, (SGU_CHUNK, SGU_CHUNK), 1) // CHUNK
    head_of_lane = lax.broadcasted_iota(jnp.int32, (SGU_CHUNK, D_SGU), 1) // (D_SGU // SGU_HEADS)
    w_heads = [jnp.where(blk_r >= blk_c, sguw_ref[hd], 0.0).astype(BF16) for hd in range(SGU_HEADS)]
    w_cat = jnp.concatenate(w_heads, axis=1)
    zero = jnp.zeros((), BF16)
    s_chunks = []
    for c in range(ts // SGU_CHUNK):
        vc = v[c * SGU_CHUNK:(c + 1) * SGU_CHUNK, :]
        v_stack = jnp.concatenate([jnp.where(head_of_lane == hd, vc, zero) for hd in range(SGU_HEADS)], axis=0)
        s_chunks.append(_dot(w_cat, v_stack) + sgub_ref[...])
    yb = u * jnp.concatenate(s_chunks, axis=0)
    out = out + _dot(yb.astype(BF16), wout_ref[D_CONV:D_CONV + D_SGU, :])

    cc = z[:, b_end - a_end:]
    pb0[HIST:HIST + ts, :] = cc
    pb1[0:ts + 24, :] = pb0[8:ts + 32, :] + pb0[7:ts + 31, :]
    pb2[0:ts + 16, :] = pb1[8:ts + 24, :] + pb1[6:ts + 22, :]
    pb3[0:ts + 8, :] = pb2[8:ts + 16, :] + pb2[4:ts + 12, :]
    s16 = pb3[8:ts + 8, :] + pb3[0:ts, :]
    s8 = pb3[8:ts + 8, :]
    s4 = pb2[16:ts + 16, :]
    s2 = pb1[24:ts + 24, :]
    grp = lax.broadcasted_iota(jnp.int32, (ts, D_POOL), 1) // POOL_GROUP_DIM
    wsum = jnp.where(grp == 0, s2, jnp.where(grp == 1, s4, jnp.where(grp == 2, s8, s16)))
    head = max(POOL_WINDOWS)
    inv_win = jnp.where(grp == 0, 0.5, jnp.where(grp == 1, 0.25, jnp.where(grp == 2, 0.125, 0.0625)))
    grp_h = lax.broadcasted_iota(jnp.int32, (head, D_POOL), 1) // POOL_GROUP_DIM
    win = jnp.where(grp_h == 0, 2, jnp.where(grp_h == 1, 4, jnp.where(grp_h == 2, 8, 16)))
    pos = s * ts + lax.broadcasted_iota(jnp.int32, (head, D_POOL), 0)
    cnt = jnp.minimum(pos + 1, win).astype(F32)
    mean = jnp.concatenate([wsum[:head] / cnt, (wsum * inv_win)[head:]], axis=0)
    p = (mean - cc).astype(BF16)
    yc = (_dot(p, poolw_ref[...]) + poolb_ref[...]) * pscale_ref[...]
    pb0[0:HIST, :] = pb0[ts:ts + HIST, :]

    return out + _dot(yc.astype(BF16), wout_ref[D_CONV + D_SGU:D_MIX, :])


def _const_spec(shape):
    zeros = (0,) * len(shape)
    return pl.BlockSpec(shape, lambda *_: zeros)


N_MIX_PARAMS = 14
N_ATT_PARAMS = 5


def _mix_attn_kernel(x_ref, *refs, n_tiles, tiles_per_row):
    mix_refs = refs[:N_MIX_PARAMS]
    att_refs = refs[N_MIX_PARAMS:N_MIX_PARAMS + N_ATT_PARAMS]
    o_ref, mid, cbuf, cshift, pb0, pb1, pb2, pb3 = refs[N_MIX_PARAMS + N_ATT_PARAMS:]
    i = pl.program_id(0)
    s = jnp.minimum(i, n_tiles - 1) % tiles_per_row

    @pl.when(i == 0)
    def _():
        mid[...] = jnp.zeros_like(mid)

    @pl.when(s == 0)
    def _():
        cbuf[0:HIST, :] = jnp.zeros((HIST, D_CONV), F32)
        pb0[0:HIST, :] = jnp.zeros((HIST, D_POOL), F32)

    attention = _xattn_pieces(mid[...], o_ref, *att_refs)
    mid[...] = _mixer_tile(x_ref[0], s, *mix_refs, cbuf, cshift, pb0, pb1, pb2, pb3, overlap=attention)


def _mix_attn(x, mix_params, k, v, att_params):
    B, S, D = x.shape
    M = k.shape[1]
    ts = MIX_TS
    tiles_per_row = S // ts
    n_tiles = B * tiles_per_row
    g_att, wq, wo = att_params
    row = lambda a: a.reshape(1, -1) if a.ndim == 1 else a
    mix_args = [row(a) for a in mix_params]
    att_args = [row(g_att), wq, k, v, wo]
    assert len(mix_args) == N_MIX_PARAMS and len(att_args) == N_ATT_PARAMS

    def mix_tile(i):
        t = jnp.minimum(i, n_tiles - 1)
        return t // tiles_per_row, t % tiles_per_row, 0

    def att_tile(i):
        t = jnp.maximum(i - 1, 0)
        return t // tiles_per_row, t % tiles_per_row, 0

    single = lambda a: pl.BlockSpec(a.shape, lambda i, nd=a.ndim: (0,) * nd, pipeline_mode=pl.Buffered(1))
    kv_spec = pl.BlockSpec((1, M, D), lambda i: (att_tile(i)[0], 0, 0))
    in_specs = [pl.BlockSpec((1, ts, D), mix_tile)] + [single(a) for a in mix_args]
    in_specs += [single(att_args[0]), single(wq), kv_spec, kv_spec, single(wo)]
    return pl.pallas_call(
        functools.partial(_mix_attn_kernel, n_tiles=n_tiles, tiles_per_row=tiles_per_row),
        grid=(n_tiles + 1,),
        in_specs=in_specs,
        out_specs=pl.BlockSpec((1, ts, D), att_tile),
        out_shape=jax.ShapeDtypeStruct((B, S, D), F32),
        scratch_shapes=[pltpu.VMEM((ts, D), F32),
                        pltpu.VMEM((ts + HIST, D_CONV), F32),
                        pltpu.VMEM((SUBLANES - 1, ts + HIST - SUBLANES, D_CONV), F32)]
                       + [pltpu.VMEM((ts + HIST, D_POOL), F32)] * 4,
        compiler_params=pltpu.CompilerParams(
            dimension_semantics=("arbitrary",), vmem_limit_bytes=VMEM_LIMIT),
        name="mix_attn",
    )(x, *mix_args, *att_args)


def _kv_kernel(mem_ref, g_ref, wk_ref, wv_ref, k_ref, v_ref):
    m = _rms(mem_ref[0], g_ref[...]).astype(BF16)
    k_ref[0] = _dot(m, wk_ref[...]).astype(BF16)
    v_ref[0] = _dot(m, wv_ref[...]).astype(BF16)


def _kv(mem, g, wk, wv):
    B, M, D = mem.shape
    return pl.pallas_call(
        _kv_kernel,
        grid=(B,),
        in_specs=[pl.BlockSpec((1, M, D), lambda b: (b, 0, 0)), _const_spec((1, D)),
                  _const_spec((D, D)), _const_spec((D, D))],
        out_specs=[pl.BlockSpec((1, M, D), lambda b: (b, 0, 0))] * 2,
        out_shape=[jax.ShapeDtypeStruct((B, M, D), BF16)] * 2,
        compiler_params=pltpu.CompilerParams(
            dimension_semantics=("arbitrary",), vmem_limit_bytes=VMEM_LIMIT),
        name="mem_kv",
    )(mem, g.reshape(1, -1), wk, wv)


def _xattn_pieces(x, o_ref, g_ref, wq_ref, k_ref, v_ref, wo_ref):
    h = _rms(x, g_ref[...]).astype(BF16)
    heads = []
    for hd in range(X_HEADS):
        sl = slice(hd * X_HEAD_DIM, (hd + 1) * X_HEAD_DIM)
        q = _dot(h, wq_ref[:, sl]).astype(BF16)
        sc = lax.dot_general(q, k_ref[0, :, sl], (((1,), (1,)), ((), ())), preferred_element_type=F32)
        e = jnp.exp2((sc - jnp.max(sc, axis=-1, keepdims=True)) * (X_HEAD_DIM ** -0.5 * LOG2_E))
        heads.append(_dot(e.astype(BF16), v_ref[0, :, sl]) / jnp.sum(e, axis=-1, keepdims=True))
        yield
    o = jnp.concatenate(heads, axis=-1).astype(BF16)
    for c in range(X_HEADS):
        sl = slice(c * X_HEAD_DIM, (c + 1) * X_HEAD_DIM)
        o_ref[0, :, sl] = x[:, sl] + _dot(o, wo_ref[:, sl])
        yield


def _store_tiles(ref, val):
    n = val.shape[0]
    for j in range(ref.shape[0]):
        ref[j, :n, :] = val[:, j * LANES:(j + 1) * LANES]


def _load_tiles(ref, n=None):
    n = ref.shape[1] if n is None else n
    return jnp.concatenate([ref[j, :n, :] for j in range(ref.shape[0])], axis=1)


def _pack_bf16_pairs(v):
    bits = lax.bitcast_convert_type(v.astype(BF16).astype(F32), U32)
    half = bits.shape[1] // 2
    return (bits[:, :half] & jnp.uint32(HI16)) | (bits[:, half:] >> 16)


def _unpack_bf16_pairs(w):
    hi = lax.bitcast_convert_type(w & jnp.uint32(HI16), F32)
    lo = lax.bitcast_convert_type(w << 16, F32)
    return jnp.concatenate([hi, lo], axis=1)


def _tiled_shape(rows, width, dtype):
    return jax.ShapeDtypeStruct((width // LANES, rows, LANES), dtype)


def _tiled_spec(rows, width, row_block, lead=None):
    block = (width // LANES, rows, LANES)
    if lead is None:
        return pl.BlockSpec(block, lambda *a: (0, row_block(*a), 0))
    return pl.BlockSpec((1,) + block, lambda *a: (lead, 0, row_block(*a), 0))


def _piece_index(rows, k, n_rows):
    return jnp.arange(k, dtype=I32)[:, None] * n_rows + rows[..., None, :]


META_I1, META_I2, META_G1, META_G2, META_R1, META_R2 = range(6)


def _router_kernel(x_ref, g_ref, rw_ref, triu_ref, meta_ref, hp_ref, cnt_ref, carry):
    @pl.when(pl.program_id(0) == 0)
    def _():
        carry[...] = jnp.zeros_like(carry)

    h = _rms(x_ref[...], g_ref[...])
    h_hi = h.astype(BF16)
    h_lo = (h - h_hi.astype(F32)).astype(BF16)
    logits = _dot(jnp.concatenate([h_hi, h_hi, h_lo], axis=1), rw_ref[...])

    neg = jnp.float32(-jnp.inf)
    lane = lax.broadcasted_iota(I32, logits.shape, 1)
    lt = jnp.where(lane < N_EXPERTS, logits, neg).T[:N_EXPERTS, :]
    expert = lax.broadcasted_iota(I32, lt.shape, 0)
    m1 = jnp.max(lt, axis=0, keepdims=True)
    i1 = jnp.min(jnp.where(lt == m1, expert, N_EXPERTS), axis=0, keepdims=True)
    lt2 = jnp.where(expert == i1, neg, lt)
    m2 = jnp.max(lt2, axis=0, keepdims=True)
    i2 = jnp.min(jnp.where(lt2 == m2, expert, N_EXPERTS), axis=0, keepdims=True)
    d = jnp.exp(m2 - m1)
    g1 = 1.0 / (1.0 + d)
    g2 = d / (1.0 + d)

    sel1 = expert == i1
    sel2 = expert == i2
    onehot = jnp.where(jnp.logical_or(sel1, sel2), 1.0, 0.0)
    before = _dot(onehot, triu_ref[...]) + carry[:, 0:1]
    r1 = jnp.sum(jnp.where(sel1, before, 0.0), axis=0, keepdims=True)
    r2 = jnp.sum(jnp.where(sel2, before, 0.0), axis=0, keepdims=True)
    carry[...] += jnp.sum(onehot, axis=1, keepdims=True)
    cnt_ref[...] = carry[...]

    rows = [i1.astype(F32), i2.astype(F32), g1, g2, r1, r2]
    meta_ref[...] = jnp.concatenate(rows + [jnp.zeros_like(g1)] * (SUBLANES - len(rows)), axis=0)
    _store_tiles(hp_ref, _pack_bf16_pairs(h))


def _router(x2, g, router_w, first_row, n_rows):
    D = x2.shape[1]
    tm = ROUTER_TM
    first_block = first_row // tm
    assert N_EXPERTS == SUBLANES
    rw = jnp.zeros((D, LANES), F32).at[:, :N_EXPERTS].set(router_w)
    rw_hi = rw.astype(BF16)
    rw_lo = (rw - rw_hi.astype(F32)).astype(BF16)
    rw_split = jnp.concatenate([rw_hi, rw_lo, rw_hi], axis=0)
    triu = jnp.triu(jnp.ones((tm, tm), F32), 1)
    return pl.pallas_call(
        _router_kernel,
        grid=(n_rows // tm,),
        in_specs=[pl.BlockSpec((tm, D), lambda i: (first_block + i, 0)), _const_spec((1, D)),
                  _const_spec((3 * D, LANES)), _const_spec((tm, tm))],
        out_specs=[pl.BlockSpec((SUBLANES, tm), lambda i: (0, i)),
                   _tiled_spec(tm, D // 2, lambda i: i), _const_spec((N_EXPERTS, LANES))],
        out_shape=[jax.ShapeDtypeStruct((SUBLANES, n_rows), F32),
                   _tiled_shape(n_rows, D // 2, U32), jax.ShapeDtypeStruct((N_EXPERTS, LANES), F32)],
        scratch_shapes=[pltpu.VMEM((N_EXPERTS, LANES), F32)],
        compiler_params=pltpu.CompilerParams(
            dimension_semantics=("arbitrary",), vmem_limit_bytes=VMEM_LIMIT),
        name="router",
    )(x2, g.reshape(1, -1), rw_split, triu)


def _sc_mesh():
    return plsc.VectorSubcoreMesh(core_axis_name="core", subcore_axis_name="subcore")


def _sc_dispatch(tiled, dest1, dest2, n_out):
    k, n, _ = tiled.shape
    d1, d2 = (_piece_index(d, k, n_out).reshape(-1) for d in (dest1, dest2))
    return _sc_scatter_pieces(tiled.reshape(k * n, LANES), d1, d2, k * n_out).reshape(k, n_out, LANES)


def _sc_gather(tiled, idx):
    k, n_rows, _ = tiled.shape
    G, n = idx.shape
    out = _sc_gather_pieces(tiled.reshape(k * n_rows, LANES), _piece_index(idx, k, n_rows).reshape(-1))
    return out.reshape(G, k, n, LANES)


def _sc_scatter_pieces(rows, dest1, dest2, n_out):
    T, W = rows.shape
    win = SC_WIN

    @pl.kernel(out_type=jax.ShapeDtypeStruct((n_out, W), rows.dtype), mesh=_sc_mesh(), scratch_types=[])
    def scatter_kernel(x_hbm, i1_hbm, i2_hbm, o_hbm):
        def body(x_vmem, i1_vmem, i2_vmem):
            pltpu.sync_copy(x_vmem, o_hbm.at[i1_vmem.at[0]])
            pltpu.sync_copy(x_vmem, o_hbm.at[i2_vmem.at[0]])

        pltpu.emit_pipeline(
            body,
            grid=(T // win,),
            in_specs=[pl.BlockSpec((win, W), lambda i: (i, 0)),
                      pl.BlockSpec((1, win), lambda i: (0, i)),
                      pl.BlockSpec((1, win), lambda i: (0, i))],
            out_specs=[],
            core_axis_name=("core", "subcore"),
            dimension_semantics=(pltpu.PARALLEL,),
        )(x_hbm, i1_hbm, i2_hbm)

    return scatter_kernel(rows, dest1.reshape(1, T), dest2.reshape(1, T))


def _sc_gather_pieces(rows, idx):
    n = idx.shape[0]
    W = rows.shape[1]
    win = SC_WIN

    @pl.kernel(out_type=jax.ShapeDtypeStruct((n, W), rows.dtype), mesh=_sc_mesh(), scratch_types=[])
    def gather_kernel(x_hbm, i_hbm, o_hbm):
        def body(i_vmem, o_vmem):
            pltpu.sync_copy(x_hbm.at[i_vmem.at[0]], o_vmem)

        pltpu.emit_pipeline(
            body,
            grid=(n // win,),
            in_specs=[pl.BlockSpec((1, win), lambda i: (0, i))],
            out_specs=[pl.BlockSpec((win, W), lambda i: (i, 0))],
            core_axis_name=("core", "subcore"),
            dimension_semantics=(pltpu.PARALLEL,),
        )(i_hbm, o_hbm)

    return gather_kernel(rows, idx.reshape(1, n))


def _swiglu(h, wg, wu, wd):
    gate = _dot(h, wg)
    up = _dot(h, wu)
    return _dot((gate * jax.nn.sigmoid(gate) * up).astype(BF16), wd)


def _swiglu_chunked(h, wg_ref, wu_ref, wd_ref, acc=None):
    F = wg_ref.shape[1]
    n_chunks = -(-F // FFN_FC)
    cols = -(-F // (n_chunks * MXU_WIDTH)) * MXU_WIDTH
    for lo in range(0, F, cols):
        sl = slice(lo, min(lo + cols, F))
        y = _swiglu(h, wg_ref[:, sl], wu_ref[:, sl], wd_ref[sl, :])
        acc = y if acc is None else acc + y
    return acc


def _ffn_kernel(x_ref, g_ref, wg_ref, wu_ref, wd_ref, *refs):
    n_cast = len(refs) // 2
    o_ref = refs[n_cast]
    x = x_ref[...]
    h = _rms(x, g_ref[...]).astype(BF16)
    o_ref[...] = _swiglu_chunked(h, wg_ref, wu_ref, wd_ref, acc=x)
    for src_ref, dst_ref in zip(refs[:n_cast], refs[n_cast + 1:]):
        dst_ref[...] = src_ref[0].astype(BF16)


def _ffn(x2, g, wg, wu, wd, cast_weights=()):
    T, D = x2.shape
    F = wg.shape[1]
    tm = FFN_TM
    n_steps = T // tm
    single = lambda shape: pl.BlockSpec(shape, lambda i: (0,) * len(shape), pipeline_mode=pl.Buffered(1))
    cast_in, cast_out, cast_shapes = [], [], []
    for w, lead in cast_weights:
        _, E, rows, cols = w.shape
        per_group = n_steps // E
        assert n_steps % E == 0 and rows % (per_group * 2 * SUBLANES) == 0
        cast_in.append(pl.BlockSpec((1, 1, rows // per_group, cols),
                                    lambda i, lead=lead, per=per_group: (lead, i // per, i % per, 0)))
        cast_out.append(pl.BlockSpec((1, rows // per_group, cols), lambda i, per=per_group: (i // per, i % per, 0)))
        cast_shapes.append(jax.ShapeDtypeStruct((E, rows, cols), BF16))
    out, *cast = pl.pallas_call(
        _ffn_kernel,
        grid=(n_steps,),
        in_specs=[pl.BlockSpec((tm, D), lambda i: (i, 0)), single((1, D)),
                  single((D, F)), single((D, F)), single((F, D))] + cast_in,
        out_specs=[pl.BlockSpec((tm, D), lambda i: (i, 0))] + cast_out,
        out_shape=[jax.ShapeDtypeStruct((T, D), F32)] + cast_shapes,
        compiler_params=pltpu.CompilerParams(
            dimension_semantics=("arbitrary",), vmem_limit_bytes=VMEM_LIMIT),
        name="swiglu",
    )(x2, g.reshape(1, -1), wg, wu, wd, *[w for w, _ in cast_weights])
    return out, cast


def _moe_kernel(tidx_ref, texp_ref, trows_ref, nvalid_ref, xs_ref, wg_ref, wu_ref, wd_ref, y_ref):
    del tidx_ref, texp_ref
    g = pl.program_id(0)
    tm = xs_ref.shape[1]
    live = g < nvalid_ref[0]
    half_full = trows_ref[g] <= tm // 2

    def run(n):
        h = _unpack_bf16_pairs(_load_tiles(xs_ref, n)).astype(BF16)
        _store_tiles(y_ref, _pack_bf16_pairs(_swiglu_chunked(h, wg_ref.at[0], wu_ref.at[0], wd_ref.at[0])))

    pl.when(jnp.logical_and(live, jnp.logical_not(half_full)))(lambda: run(tm))
    pl.when(jnp.logical_and(live, half_full))(lambda: run(tm // 2))


def _moe_experts(xs, tile_idx, tile_expert, tile_rows, n_valid, wg, wu, wd):
    E, D, F = wg.shape
    tm = MOE_TM
    P = xs.shape[1]
    half = xs.shape[0] * LANES
    row_block = lambda g, ti, te, tr, nv: ti[g]
    w_map = lambda g, ti, te, tr, nv: (te[g], 0, 0)
    return pl.pallas_call(
        _moe_kernel,
        grid_spec=pltpu.PrefetchScalarGridSpec(
            num_scalar_prefetch=4,
            grid=(P // tm,),
            in_specs=[_tiled_spec(tm, half, row_block),
                      pl.BlockSpec((1, D, F), w_map), pl.BlockSpec((1, D, F), w_map),
                      pl.BlockSpec((1, F, D), w_map)],
            out_specs=_tiled_spec(tm, D // 2, row_block)),
        out_shape=_tiled_shape(P, D // 2, U32),
        compiler_params=pltpu.CompilerParams(
            dimension_semantics=("arbitrary",), vmem_limit_bytes=VMEM_LIMIT),
        name="moe_experts",
    )(tile_idx, tile_expert, tile_rows, n_valid, xs, wg, wu, wd)


def _combine_kernel(x_ref, y1_ref, y2_ref, meta_ref, fg_ref, *refs, final_norm):
    o_ref = refs[-1]
    rec = meta_ref[...]
    rec_t = jnp.concatenate([rec, jnp.zeros((LANES - rec.shape[0], rec.shape[1]), F32)], axis=0).T
    lane = lax.broadcasted_iota(I32, rec_t.shape, 1)
    gate = lambda row: jnp.sum(jnp.where(lane == row, rec_t, 0.0), axis=-1, keepdims=True)
    out = x_ref[...] + gate(META_G1) * _unpack_bf16_pairs(_load_tiles(y1_ref.at[0]))
    out = out + gate(META_G2) * _unpack_bf16_pairs(_load_tiles(y2_ref.at[0]))
    if final_norm:
        out = _rms(out, fg_ref[...])
    o_ref[...] = out


def _combine(x2, y12, meta, final_g, first_row, partial_out, *, final_norm):
    T, D = x2.shape
    n_rows = meta.shape[1]
    tm = FFN_TM
    first_block = first_row // tm
    rows = lambda i: (first_block + i, 0)
    args = [x2, y12, y12, meta, final_g.reshape(1, -1)]
    in_specs = [pl.BlockSpec((tm, D), rows),
                _tiled_spec(tm, D // 2, lambda i: i, lead=0), _tiled_spec(tm, D // 2, lambda i: i, lead=1),
                pl.BlockSpec((SUBLANES, tm), lambda i: (0, i)), _const_spec((1, D))]
    aliases = {}
    if partial_out is not None:
        args.append(partial_out)
        in_specs.append(pl.BlockSpec(memory_space=pl.ANY))
        aliases = {len(args) - 1: 0}
    return pl.pallas_call(
        functools.partial(_combine_kernel, final_norm=final_norm),
        grid=(n_rows // tm,),
        in_specs=in_specs,
        out_specs=pl.BlockSpec((tm, D), rows),
        out_shape=jax.ShapeDtypeStruct((T, D), F32),
        input_output_aliases=aliases,
        compiler_params=pltpu.CompilerParams(
            dimension_semantics=("arbitrary",), vmem_limit_bytes=VMEM_LIMIT),
        name="moe_combine",
    )(*args)


def _route_and_dispatch(x2, g, router_w, first_row, n_rows):
    tm = MOE_TM
    n_tiles = (n_rows * TOP_K) // tm + N_EXPERTS
    meta, hp, counts = _router(x2, g, router_w, first_row, n_rows)

    cnt = counts[:, 0].astype(I32)
    tiles_e = (cnt + tm - 1) // tm
    tile_end = jnp.cumsum(tiles_e)
    row_start = (tile_end - tiles_e) * tm
    experts = jnp.arange(N_EXPERTS, dtype=I32)[:, None]

    def dest(i_row, r_row):
        start = jnp.sum(jnp.where(meta[i_row].astype(I32) == experts, row_start[:, None], 0), axis=0)
        return start + meta[r_row].astype(I32)

    dests = jnp.stack([dest(META_I1, META_R1), dest(META_I2, META_R2)])
    n_valid = tile_end[-1:]
    tile_idx = jnp.minimum(jnp.arange(n_tiles, dtype=I32), n_valid - 1)
    tile_expert = jnp.minimum(jnp.sum(tile_end[:, None] <= tile_idx, axis=0), N_EXPERTS - 1).astype(I32)
    tile_rows = jnp.clip(cnt[tile_expert] - (tile_idx - (tile_end - tiles_e)[tile_expert]) * tm, 0, tm).astype(I32)
    xs = _sc_dispatch(hp, dests[0], dests[1], n_tiles * tm)
    return xs, (tile_idx, tile_expert, tile_rows, n_valid.astype(I32)), dests, meta


def _moe_layer(x2, g, router_w, wg, wu, wd, final_g, *, final_norm):
    T = x2.shape[0]
    n_rows = T // MOE_ROW_GROUPS
    starts = [grp * n_rows for grp in range(MOE_ROW_GROUPS)]
    routed = [_route_and_dispatch(x2, g, router_w, first_row, n_rows) for first_row in starts]
    ys = [_moe_experts(xs, *tiles, wg, wu, wd) for xs, tiles, _, _ in routed]
    y12s = [_sc_gather(y, dests) for y, (_, _, dests, _) in zip(ys, routed)]
    out = None
    for first_row, y12, (_, _, _, meta) in zip(starts, y12s, routed):
        out = _combine(x2, y12, meta, final_g, first_row, out, final_norm=final_norm)
    return out


def kernel(x, mem, mix_norm_g, w_in, conv_w, conv_b, conv_ln_g, conv_ln_b, sgu_ln_g, sgu_ln_b, sgu_w, sgu_b,
           pool_w, pool_b, pool_scale, w_out, xattn_norm_g, mem_norm_g, xattn_wq, xattn_wk, xattn_wv, xattn_wo,
           ffn_norm_g, ffn_wg, ffn_wu, ffn_wd, router_w, moe_wg, moe_wu, moe_wd, final_norm_g):
    B, S, D = x.shape
    bf = lambda a: a.astype(BF16)
    layer_mats = dict(w_in=w_in, w_out=w_out, wq=xattn_wq, wk=xattn_wk, wv=xattn_wv, wo=xattn_wo)
    precast = {}

    def mat(name, l):
        return precast[name, l] if (name, l) in precast else bf(layer_mats[name][l])

    for l in range(DEPTH):
        sgu_bias = jnp.repeat(sgu_b[l].T, D_SGU // SGU_HEADS, axis=1)
        pool_wbd = jax.scipy.linalg.block_diag(*[pool_w[l, gi] for gi in range(len(POOL_WINDOWS))])
        mix_params = (mix_norm_g[l], mat("w_in", l), conv_w[l], conv_b[l], conv_ln_g[l], conv_ln_b[l],
                      sgu_ln_g[l], sgu_ln_b[l], sgu_w[l], sgu_bias, bf(pool_wbd), pool_b[l].reshape(-1),
                      pool_scale[l], mat("w_out", l))
        k, v = _kv(mem, mem_norm_g[l], mat("wk", l), mat("wv", l))
        x = _mix_attn(x, mix_params, k, v, (xattn_norm_g[l], mat("wq", l), mat("wo", l)))
        x2 = x.reshape(B * S, D)
        j = l // 2
        if l % 2 == 0:
            assert l != DEPTH - 1, "the final RMSNorm is fused into the routed layer's combine kernel"
            names = list(layer_mats)
            split_rows = lambda a: a.reshape(a.shape[0], N_EXPERTS, a.shape[1] // N_EXPERTS, a.shape[2])
            side = [(moe_wg, j), (moe_wu, j), (moe_wd, j)] + [(split_rows(layer_mats[n]), l + 1) for n in names]
            x2, cast = _ffn(x2, ffn_norm_g[l], bf(ffn_wg[j]), bf(ffn_wu[j]), bf(ffn_wd[j]), cast_weights=side)
            moe_bf16 = cast[:3]
            for n, c in zip(names, cast[3:]):
                precast[n, l + 1] = c.reshape(layer_mats[n].shape[1:])
        else:
            x2 = _moe_layer(x2, ffn_norm_g[l], router_w[j], *moe_bf16, final_norm_g, final_norm=l == DEPTH - 1)
        x = x2.reshape(B, S, D)
    return x
```
